```python
import jax, jax.numpy as jnp
from jax import lax
import numpy as np

D_MODEL = 1024
BATCH = 4
SEQ = 4096
DEPTH = 2
DEC_BATCH = 128
DEC_SEQ = 1
PAST_LEN = 8192
PAGE_SIZE = 128

N_MIXERS = 2
HEAD_DIM = 64
RWKV_HEADS = D_MODEL // HEAD_DIM
DECAY_LORA = 64
AAA_LORA = 64
GATE_LORA = 128
N_MIX_VEC = 6
ATT_HEADS = D_MODEL // HEAD_DIM
KV_HEADS = 4
GROUP = ATT_HEADS // KV_HEADS
WINDOW = 128
ATT_BLOCK = 128
ROPE_DIM = HEAD_DIM // 4
ROPE_THETA = 500000.0
D_FF = 4 * D_MODEL
N_RWKV = (DEPTH + N_MIXERS - 1) // N_MIXERS
N_ATTN = DEPTH // N_MIXERS
NORM_EPS = 1e-6
LNX_EPS = 64e-5
N_ADA = 6

kernel_name = 'rwkv7_swa_sink_hybrid_decode_step'


def rms_norm(x, g):
    xf = x.astype(jnp.float32)
    y = xf * lax.rsqrt(jnp.mean(xf * xf, axis=-1, keepdims=True) + NORM_EPS)
    return (y * g.astype(jnp.float32)).astype(x.dtype)


def ada_params(c, w, b):
    m = jax.nn.silu(c) @ w + b
    m = m.reshape(c.shape[0], N_ADA, 1, D_MODEL)
    return [m[:, i] for i in range(N_ADA)]


def pre_norm(x, g, shift, scale):
    return rms_norm(x, g) * (1 + scale) + shift


def rope_partial(x, pos):
    half = ROPE_DIM // 2
    inv_freq = ROPE_THETA ** (-jnp.arange(half, dtype=jnp.float32) / half)
    ang = pos.astype(jnp.float32)[:, None] * inv_freq[None, :]
    cos = jnp.cos(ang)[:, None, :]
    sin = jnp.sin(ang)[:, None, :]
    xf = x[..., :ROPE_DIM].astype(jnp.float32)
    x1, x2 = xf[..., :half], xf[..., half:]
    rot = jnp.concatenate([x1 * cos - x2 * sin, x2 * cos + x1 * sin], axis=-1).astype(x.dtype)
    return jnp.concatenate([rot, x[..., ROPE_DIM:]], axis=-1)


def sink_softmax(scores, sink):
    s = sink[:, :, None, None]
    m = jnp.maximum(jnp.max(scores, axis=-1, keepdims=True), s)
    p = jnp.exp(scores - m)
    return p / (jnp.sum(p, axis=-1, keepdims=True) + jnp.exp(s - m))


def sq_relu_mlp(h, w_up, w_down):
    return jnp.square(jax.nn.relu(h @ w_up)) @ w_down


def wkv7_step(state, inp):
    r_t, w_t, k_t, v_t, a_t, b_t = inp
    sa = jnp.einsum('bhij,bhj->bhi', state, a_t)
    state = (state * w_t[:, :, None, :] + sa[..., None] * b_t[:, :, None, :]
             + v_t[..., None] * k_t[:, :, None, :])
    y = jnp.einsum('bhij,bhj->bhi', state, r_t)
    return state, y


def rwkv7_time_mix(h, shift_prev, wkv_prev, mix, w_r, w_k, w_v, w_o, w0, w1, w2,
                   a0, a1, a2, g1, g2, k_k, k_a, r_k, lnx_g, lnx_b):
    B, S, D = h.shape
    H, N = RWKV_HEADS, HEAD_DIM
    f32 = jnp.float32
    prev = jnp.concatenate([shift_prev[:, None, :].astype(h.dtype), h[:, :-1]], axis=1)
    xx = prev - h
    xr, xw, xk, xv, xa, xg = [h + xx * mix[i] for i in range(N_MIX_VEC)]
    r = xr @ w_r
    k = xk @ w_k
    v = xv @ w_v
    w = -jax.nn.softplus(-(w0 + jnp.tanh(xw @ w1) @ w2)) - 0.5
    a = jax.nn.sigmoid(a0 + (xa @ a1) @ a2)
    g = jax.nn.sigmoid(xg @ g1) @ g2
    heads = lambda t: t.reshape(B, S, H, N).astype(f32)
    kk = heads(k * k_k)
    kk = kk / jnp.maximum(jnp.sqrt(jnp.sum(kk * kk, axis=-1, keepdims=True)), 1e-12)
    k = k * (1 + (a - 1) * k_a)
    r_h, k_h, v_h, a_h = heads(r), heads(k), heads(v), heads(a)
    decay = jnp.exp(-jnp.exp(heads(w)))
    seq_first = lambda t: jnp.moveaxis(t, 1, 0)
    xs = tuple(seq_first(t) for t in (r_h, decay, k_h, v_h, -kk, kk * a_h))
    state, y = lax.scan(wkv7_step, wkv_prev.astype(f32), xs)
    y = jnp.moveaxis(y, 0, 1)
    mu = jnp.mean(y, axis=-1, keepdims=True)
    var = jnp.mean(jnp.square(y - mu), axis=-1, keepdims=True)
    y = ((y - mu) * lax.rsqrt(var + LNX_EPS) * lnx_g.astype(f32).reshape(H, N)
         + lnx_b.astype(f32).reshape(H, N))
    bonus = jnp.sum(r_h * k_h * r_k.astype(f32), axis=-1, keepdims=True) * v_h
    out = ((y + bonus).reshape(B, S, D).astype(h.dtype) * g) @ w_o
    return out, h[:, -1], state.astype(wkv_prev.dtype)


def qkv_proj(h, w_qkv, b_qkv, pos):
    B, S, _ = h.shape
    qkv = h @ w_qkv + b_qkv
    nq, nk = ATT_HEADS * HEAD_DIM, KV_HEADS * HEAD_DIM
    q = qkv[..., :nq].reshape(B, S, ATT_HEADS, HEAD_DIM)
    k = qkv[..., nq:nq + nk].reshape(B, S, KV_HEADS, HEAD_DIM)
    v = qkv[..., nq + nk:].reshape(B, S, KV_HEADS, HEAD_DIM)
    return rope_partial(q, pos), rope_partial(k, pos), v


def swa_prompt(h, w_qkv, b_qkv, w_o, sink):
    B, S, _ = h.shape
    pos = jnp.arange(S, dtype=jnp.int32)
    q, k, v = qkv_proj(h, w_qkv, b_qkv, pos)
    nb = S // ATT_BLOCK
    qb = q.reshape(B, nb, ATT_BLOCK, KV_HEADS, GROUP, HEAD_DIM)
    kb = k.reshape(B, nb, ATT_BLOCK, KV_HEADS, HEAD_DIM)
    vb = v.reshape(B, nb, ATT_BLOCK, KV_HEADS, HEAD_DIM)

    def with_prev(t):
        prev = jnp.concatenate([jnp.zeros_like(t[:, :1]), t[:, :-1]], axis=1)
        return jnp.concatenate([prev, t], axis=2)

    kw, vw = with_prev(kb), with_prev(vb)
    scores = jnp.einsum('bnqkgd,bnskd->bnkgqs', qb, kw,
                        preferred_element_type=jnp.float32) * (HEAD_DIM ** -0.5)
    blk = jnp.arange(nb, dtype=jnp.int32)[:, None, None]
    qpos = blk * ATT_BLOCK + jnp.arange(ATT_BLOCK, dtype=jnp.int32)[None, :, None]
    kpos = (blk - 1) * ATT_BLOCK + jnp.arange(2 * ATT_BLOCK, dtype=jnp.int32)[None, None, :]
    mask = (kpos <= qpos) & (kpos >= qpos - WINDOW) & (kpos >= 0)
    scores = jnp.where(mask[None, :, None, None], scores, -jnp.inf)
    probs = sink_softmax(scores, sink.astype(jnp.float32).reshape(KV_HEADS, GROUP)).astype(h.dtype)
    o = jnp.einsum('bnkgqs,bnskd->bnqkgd', probs, vw).reshape(B, S, ATT_HEADS * HEAD_DIM)
    w_keep = min(WINDOW, S)
    return o @ w_o, k[:, S - w_keep:], v[:, S - w_keep:]


def swa_sample(h, k_buf, v_buf, w_qkv, b_qkv, w_o, sink):
    B, S, _ = h.shape
    W = k_buf.shape[1]
    qpos = PAST_LEN + jnp.arange(S, dtype=jnp.int32)
    q, k, v = qkv_proj(h, w_qkv, b_qkv, qpos)
    k_all = jnp.concatenate([k_buf.astype(k.dtype), k], axis=1)
    v_all = jnp.concatenate([v_buf.astype(v.dtype), v], axis=1)
    kpos = jnp.concatenate([PAST_LEN - W + jnp.arange(W, dtype=jnp.int32), qpos])
    mask = (kpos[None, :] <= qpos[:, None]) & (kpos[None, :] >= qpos[:, None] - WINDOW)
    qg = q.reshape(B, S, KV_HEADS, GROUP, HEAD_DIM)
    scores = jnp.einsum('bqkgd,bskd->bkgqs', qg, k_all,
                        preferred_element_type=jnp.float32) * (HEAD_DIM ** -0.5)
    scores = jnp.where(mask, scores, -jnp.inf)
    probs = sink_softmax(scores, sink.astype(jnp.float32).reshape(KV_HEADS, GROUP)).astype(h.dtype)
    o = jnp.einsum('bkgqs,bskd->bqkgd', probs, v_all).reshape(B, S, ATT_HEADS * HEAD_DIM)
    return o @ w_o, k_all[:, S:], v_all[:, S:]


def setup_inputs(seed: int = 0) -> dict:
    key = jax.random.key(seed)
    ks = iter(jax.random.split(key, 64))
    f32 = jnp.float32

    def nrm(shape, scale):
        return jax.random.normal(next(ks), shape, f32) * scale

    def uni(shape, lo, hi):
        return jax.random.uniform(next(ks), shape, f32, lo, hi)

    D, H, N = D_MODEL, RWKV_HEADS, HEAD_DIM
    w_buf = min(WINDOW, PAST_LEN)
    qkv_out = (ATT_HEADS + 2 * KV_HEADS) * HEAD_DIM
    return {
        'x_prompt': nrm((BATCH, SEQ, D), 1.0),
        'x_sample': nrm((DEC_BATCH, DEC_SEQ, D), 1.0),
        'c_prompt': nrm((BATCH, D), 1.0),
        'c_sample': nrm((DEC_BATCH, D), 1.0),
        'state_wkv': nrm((N_RWKV, DEC_BATCH, H, N, N), 0.3),
        'state_shift': nrm((N_RWKV, DEC_BATCH, D), 1.0),
        'cache_k': nrm((N_ATTN, DEC_BATCH, w_buf, KV_HEADS, HEAD_DIM), 1.0),
        'cache_v': nrm((N_ATTN, DEC_BATCH, w_buf, KV_HEADS, HEAD_DIM), 1.0),
        'norm1_g': 1.0 + nrm((DEPTH, D), 0.05),
        'norm2_g': 1.0 + nrm((DEPTH, D), 0.05),
        'ada_w': nrm((DEPTH, D, N_ADA * D), 0.5 * D ** -0.5),
        'ada_b': nrm((DEPTH, N_ADA * D), 0.02),
        'mlp_up': nrm((DEPTH, D, D_FF), D ** -0.5),
        'mlp_down': nrm((DEPTH, D_FF, D), D_FF ** -0.5),
        'final_g': 1.0 + nrm((D,), 0.05),
        'rw_mix': uni((N_RWKV, N_MIX_VEC, D), 0.0, 1.0),
        'rw_wr': nrm((N_RWKV, D, D), D ** -0.5),
        'rw_wk': nrm((N_RWKV, D, D), D ** -0.5),
        'rw_wv': nrm((N_RWKV, D, D), D ** -0.5),
        'rw_wo': nrm((N_RWKV, D, D), D ** -0.5),
        'rw_w0': uni((N_RWKV, D), -6.0, 1.0),
        'rw_w1': nrm((N_RWKV, D, DECAY_LORA), D ** -0.5),
        'rw_w2': nrm((N_RWKV, DECAY_LORA, D), 0.5 * DECAY_LORA ** -0.5),
        'rw_a0': nrm((N_RWKV, D), 0.5),
        'rw_a1': nrm((N_RWKV, D, AAA_LORA), D ** -0.5),
        'rw_a2': nrm((N_RWKV, AAA_LORA, D), 0.5 * AAA_LORA ** -0.5),
        'rw_g1': nrm((N_RWKV, D, GATE_LORA), D ** -0.5),
        'rw_g2': nrm((N_RWKV, GATE_LORA, D), GATE_LORA ** -0.5),
        'rw_kk': 0.85 + nrm((N_RWKV, D), 0.05),
        'rw_ka': 1.0 + nrm((N_RWKV, D), 0.05),
        'rw_rk': nrm((N_RWKV, H, N), 0.1),
        'rw_lnx_g': 1.0 + nrm((N_RWKV, D), 0.05),
        'rw_lnx_b': nrm((N_RWKV, D), 0.02),
        'at_wqkv': nrm((N_ATTN, D, qkv_out), D ** -0.5),
        'at_bqkv': nrm((N_ATTN, qkv_out), 0.02),
        'at_wo': nrm((N_ATTN, ATT_HEADS * HEAD_DIM, D), (ATT_HEADS * HEAD_DIM) ** -0.5),
        'at_sink': nrm((N_ATTN, ATT_HEADS), 1.0),
    }


def reference(x_prompt, x_sample, c_prompt, c_sample, state_wkv, state_shift, cache_k, cache_v,
              norm1_g, norm2_g, ada_w, ada_b, mlp_up, mlp_down, final_g,
              rw_mix, rw_wr, rw_wk, rw_wv, rw_wo, rw_w0, rw_w1, rw_w2, rw_a0, rw_a1, rw_a2,
              rw_g1, rw_g2, rw_kk, rw_ka, rw_rk, rw_lnx_g, rw_lnx_b,
              at_wqkv, at_bqkv, at_wo, at_sink):
    xp, xs = x_prompt, x_sample
    bp = xp.shape[0]
    wkv_p, wkv_s, sh_p, sh_s = [], [], [], []
    k_p, k_s, v_p, v_s = [], [], [], []
    for i in range(DEPTH):
        mp = ada_params(c_prompt, ada_w[i], ada_b[i])
        ms = ada_params(c_sample, ada_w[i], ada_b[i])
        hp = pre_norm(xp, norm1_g[i], mp[0], mp[1])
        hs = pre_norm(xs, norm1_g[i], ms[0], ms[1])
        j = i // N_MIXERS
        if i % N_MIXERS == 0:
            prm = (rw_mix[j], rw_wr[j], rw_wk[j], rw_wv[j], rw_wo[j], rw_w0[j], rw_w1[j], rw_w2[j],
                   rw_a0[j], rw_a1[j], rw_a2[j], rw_g1[j], rw_g2[j], rw_kk[j], rw_ka[j], rw_rk[j],
                   rw_lnx_g[j], rw_lnx_b[j])
            zero_shift = jnp.zeros((bp, D_MODEL), hp.dtype)
            zero_wkv = jnp.zeros((bp, RWKV_HEADS, HEAD_DIM, HEAD_DIM), hp.dtype)
            op, shift_new_p, wkv_new_p = rwkv7_time_mix(hp, zero_shift, zero_wkv, *prm)
            os_, shift_new_s, wkv_new_s = rwkv7_time_mix(hs, state_shift[j], state_wkv[j], *prm)
            sh_p.append(shift_new_p)
            wkv_p.append(wkv_new_p)
            sh_s.append(shift_new_s)
            wkv_s.append(wkv_new_s)
        else:
            op, kn_p, vn_p = swa_prompt(hp, at_wqkv[j], at_bqkv[j], at_wo[j], at_sink[j])
            os_, kn_s, vn_s = swa_sample(hs, cache_k[j], cache_v[j], at_wqkv[j], at_bqkv[j],
                                        at_wo[j], at_sink[j])
            k_p.append(kn_p)
            v_p.append(vn_p)
            k_s.append(kn_s)
            v_s.append(vn_s)
        xp = xp + mp[2] * op
        xs = xs + ms[2] * os_
        hp = pre_norm(xp, norm2_g[i], mp[3], mp[4])
        hs = pre_norm(xs, norm2_g[i], ms[3], ms[4])
        xp = xp + mp[5] * sq_relu_mlp(hp, mlp_up[i], mlp_down[i])
        xs = xs + ms[5] * sq_relu_mlp(hs, mlp_up[i], mlp_down[i])
    y_prompt = rms_norm(xp, final_g)
    y_sample = rms_norm(xs, final_g)
    return (y_prompt, y_sample, jnp.stack(wkv_p), jnp.stack(wkv_s), jnp.stack(sh_p), jnp.stack(sh_s),
            jnp.stack(k_p), jnp.stack(k_s), jnp.stack(v_p), jnp.stack(v_s))
```

```python
import functools

import jax
import jax.numpy as jnp
from jax import lax
from jax.experimental import pallas as pl
from jax.experimental.pallas import tpu as pltpu

F32 = jnp.float32
BF16 = jnp.bfloat16

HEAD_DIM = 64
KV_HEADS = 4
GROUP = 4
WINDOW = 128
ATT_BLOCK = 128
ROPE_DIM = HEAD_DIM // 4
ROPE_THETA = 500000.0
PAST_LEN = 8192
NORM_EPS = 1e-6
LNX_EPS = 64e-5
N_ADA = 6

LANE = 128
V7X_VMEM_BYTES = 64 * 1024 * 1024
VMEM_LIMIT = V7X_VMEM_BYTES * 7 // 8

CHUNK = 64
HPG = 4
GW = HPG * HEAD_DIM


def _bf(x):
    return x.astype(BF16)


def _dot(a, b):
    return jnp.dot(a, b, preferred_element_type=F32)


def _dot_nt(a, b):
    return lax.dot_general(a, b, (((1,), (1,)), ((), ())), preferred_element_type=F32)


def _rms(x, g):
    ms = jnp.mean(x * x, axis=-1, keepdims=True)
    return x * lax.rsqrt(ms + NORM_EPS) * g


def _prenorm(x, g, shift, scale):
    return _rms(x, g) * (1.0 + scale) + shift


def _mod(mod_ref, per_row, idx, d):
    if per_row:
        return mod_ref[:, idx * d:(idx + 1) * d]
    return mod_ref[0, idx:idx + 1, :]


def _pieces(x, n):
    out = []
    rem = x
    for i in range(n):
        p = rem.astype(BF16)
        out.append(p)
        if i + 1 < n:
            rem = rem - p.astype(F32)
    return out


def _mm(xp, yp, dn):
    n = max(len(xp), len(yp))
    acc = None
    for i, xi in enumerate(xp):
        for j, yj in enumerate(yp):
            if i + j < n:
                t = lax.dot_general(xi, yj, dn, preferred_element_type=F32)
                acc = t if acc is None else acc + t
    return acc


_NN = (((1,), (0,)), ((), ()))
_NT = (((1,), (1,)), ((), ()))


def _ada_kernel(c_ref, w_ref, b_ref, o_ref):
    c = c_ref[...]
    s = c * jax.nn.sigmoid(c)
    o_ref[0] = _dot(_bf(s), _bf(w_ref[0])) + b_ref[0]


def _ada(c_all, ada_w, ada_b):
    depth, d, n = ada_w.shape
    m = c_all.shape[0]
    tn = 512
    return pl.pallas_call(
        _ada_kernel,
        grid=(depth, n // tn),
        in_specs=[
            pl.BlockSpec((m, d), lambda i, j: (0, 0)),
            pl.BlockSpec((1, d, tn), lambda i, j: (i, 0, j)),
            pl.BlockSpec((1, 1, tn), lambda i, j: (i, 0, j)),
        ],
        out_specs=pl.BlockSpec((1, m, tn), lambda i, j: (i, 0, j)),
        out_shape=jax.ShapeDtypeStruct((depth, m, n), F32),
        compiler_params=pltpu.CompilerParams(dimension_semantics=("parallel", "parallel")),
        name="ada",
    )(c_all, ada_w, ada_b.reshape(depth, 1, n))


def _softplus(x):
    return jnp.maximum(x, 0.0) + jnp.log1p(jnp.exp(-jnp.abs(x)))


def _rwkv_proj_kernel(per_row, tiles_per_seq,
                      x_ref, xp_ref, mod_ref, ng_ref, mix_ref, wr_ref, wk_ref, wv_ref,
                      w1_ref, w2_ref, w0_ref, a1_ref, a2_ref, a0_ref, g1_ref, g2_ref,
                      kk_ref, ka_ref, ones_ref,
                      r_o, k_o, v_o, lw_o, kk_o, a_o, g_o, h_o):
    d = x_ref.shape[1]
    tm = x_ref.shape[0]
    ng = ng_ref[...]
    shift = _mod(mod_ref, per_row, 0, d)
    scale = _mod(mod_ref, per_row, 1, d)
    h = _prenorm(x_ref[...], ng, shift, scale)
    if per_row:
        prev = xp_ref[...]
        h_o[...] = h
    else:
        hp = _prenorm(xp_ref[...], ng, shift, scale)
        first = (pl.program_id(0) % tiles_per_seq) == 0
        prow = jnp.where(first, 0.0, hp[7:8, :])
        rowid = lax.broadcasted_iota(jnp.int32, h.shape, 0)
        prev = jnp.where(rowid == 0, prow, pltpu.roll(h, 1, axis=0))
        h_o[0] = h[tm - 8:tm, :]
    xx = prev - h

    def mixed(j):
        return _bf(h + xx * mix_ref[j:j + 1, :])

    r = _dot(mixed(0), wr_ref[...])
    wl = w0_ref[...] + _dot(_bf(jnp.tanh(_dot(mixed(1), w1_ref[...]))), w2_ref[...])
    k = _dot(mixed(2), wk_ref[...])
    v = _dot(mixed(3), wv_ref[...])
    a = jax.nn.sigmoid(a0_ref[...] + _dot(_bf(_dot(mixed(4), a1_ref[...])), a2_ref[...]))
    g = _dot(_bf(jax.nn.sigmoid(_dot(mixed(5), g1_ref[...]))), g2_ref[...])
    w = -_softplus(-wl) - 0.5
    kk = k * kk_ref[...]
    ss = _dot(_bf(kk * kk), ones_ref[...])
    kk = kk / jnp.maximum(jnp.sqrt(ss), 1e-12)
    r_o[...] = r
    k_o[...] = k * (1.0 + (a - 1.0) * ka_ref[...])
    v_o[...] = v
    lw_o[...] = -jnp.exp(w)
    kk_o[...] = kk
    a_o[...] = a
    g_o[...] = g


def _const_spec(shape):
    nd = len(shape)
    return pl.BlockSpec(shape, lambda *_: (0,) * nd, pipeline_mode=pl.Buffered(1))


def _rwkv_proj(x, prev_rows, mod, ng, p, tm, seq_len):
    t, d = x.shape
    per_row = prev_rows is not None
    nt = t // tm
    tiles_per_seq = 1 if per_row else seq_len // tm
    row = lambda i: (i, 0)
    if per_row:
        xp, xp_spec = prev_rows, pl.BlockSpec((tm, d), row)
        mod_spec = pl.BlockSpec((tm, N_ADA * d), row)
        h_shape, h_spec = jax.ShapeDtypeStruct((t, d), F32), pl.BlockSpec((tm, d), row)
    else:
        xp, xp_spec = x, pl.BlockSpec((8, d), lambda i: (jnp.maximum(i * (tm // 8) - 1, 0), 0))
        mod_spec = pl.BlockSpec((1, N_ADA, d), lambda i: (i // tiles_per_seq, 0, 0))
        h_shape, h_spec = jax.ShapeDtypeStruct((nt, 8, d), F32), pl.BlockSpec((1, 8, d), lambda i: (i, 0, 0))
    consts = [ng, p["mix"], p["wr"], p["wk"], p["wv"], p["w1"], p["w2"], p["w0"], p["a1"], p["a2"], p["a0"],
              p["g1"], p["g2"], p["kk"], p["ka"], p["ones"]]
    act = jax.ShapeDtypeStruct((t, d), F32)
    return pl.pallas_call(
        functools.partial(_rwkv_proj_kernel, per_row, tiles_per_seq),
        grid=(nt,),
        in_specs=[pl.BlockSpec((tm, d), row), xp_spec, mod_spec] + [_const_spec(c.shape) for c in consts],
        out_specs=[pl.BlockSpec((tm, d), row)] * 7 + [h_spec],
        out_shape=[act] * 7 + [h_shape],
        compiler_params=pltpu.CompilerParams(dimension_semantics=("parallel",), vmem_limit_bytes=VMEM_LIMIT),
        name="rwkv_proj_step" if per_row else "rwkv_proj_seq",
    )(x, xp, mod, *consts)


def _head_of_lane(shape, axis):
    return lax.broadcasted_iota(jnp.int32, shape, axis) // HEAD_DIM


def _bd(pieces, diag_mask):
    return [jnp.where(diag_mask, jnp.concatenate([p] * HPG, axis=0), jnp.zeros((), p.dtype)) for p in pieces]


def _wkv_prep_chunk(r, k, v, lw, kk, a, masks, np_):
    tri, diag, strict, incl, eye, blk8, levels, lane_head = masks

    def pp(x, y):
        return _mm(_pieces(x, np_), _bd(_pieces(y, np_), diag), _NN)

    def pabt(x, y):
        return _mm(_pieces(x, np_), _bd(_pieces(y, np_), diag), _NT)

    def patb(x, y):
        gram = _mm(_pieces(x.T, np_), _pieces(y, np_), _NN)
        out = gram[0:HEAD_DIM, :]
        for hh in range(1, HPG):
            out = jnp.where(lane_head == hh, gram[hh * HEAD_DIM:(hh + 1) * HEAD_DIM, :], out)
        return out

    c = _mm([tri], _pieces(lw, 3), _NN)
    c_prev = c - lw
    c_last = c[CHUNK - 1:CHUNK, :]
    e_neg = jnp.exp(-c)
    e_end = jnp.exp(c_last - c)
    av = -kk
    bv = kk * a
    a_t = av * jnp.exp(c_prev)
    r_t = r * jnp.exp(c)
    b_t = bv * e_neg
    k_t = k * e_neg
    b_e = bv * e_end
    k_e = k * e_end

    ar = jnp.concatenate([a_t, r_t], axis=0)
    xb = pabt(ar, b_t)
    xk = pabt(ar, k_t)
    a_ab = jnp.where(strict, xb[:CHUNK], 0.0)
    a_rb = jnp.where(incl, xb[CHUNK:], 0.0)
    a_ak = jnp.where(strict, xk[:CHUNK], 0.0)
    a_rk = jnp.where(incl, xk[CHUNK:], 0.0)

    n0 = jnp.where(blk8, a_ab, 0.0)
    n2 = pp(n0, n0)
    n4 = pp(n2, n2)
    tinv = pp(pp(eye + n0, eye + n2), eye + n4)
    for lvl in levels:
        off = jnp.where(lvl, a_ab, 0.0)
        tinv = tinv + pp(pp(tinv, off), tinv)

    w1 = pp(tinv, a_t)
    w2 = pp(tinv, pp(a_ak, v))
    qc = r_t + pp(a_rb, w1)
    y0 = pp(a_rb, w2) + pp(a_rk, v)
    mc = patb(w1, b_e)
    cc = patb(w2, b_e) + patb(v, k_e)
    return qc, y0, mc, cc, c_last


def _wkv_masks():
    t = lax.broadcasted_iota(jnp.int32, (CHUNK, GW), 0)
    s = lax.broadcasted_iota(jnp.int32, (CHUNK, GW), 1) % HEAD_DIM
    tri = (lax.broadcasted_iota(jnp.int32, (CHUNK, CHUNK), 1)
           <= lax.broadcasted_iota(jnp.int32, (CHUNK, CHUNK), 0)).astype(BF16)
    diag = _head_of_lane((GW, GW), 0) == _head_of_lane((GW, GW), 1)
    strict = s < t
    incl = s <= t
    eye = (s == t).astype(F32)
    blk8 = strict & ((s // 8) == (t // 8))
    levels = []
    b = 8
    while b < CHUNK:
        levels.append(((s // (2 * b)) == (t // (2 * b))) & ((s // b) != (t // b)) & strict)
        b *= 2
    lane_head = _head_of_lane((HEAD_DIM, GW), 1)
    return tri, diag, strict, incl, eye, blk8, levels, lane_head


def _wkv_prep_kernel(nsub, np_, r_ref, k_ref, v_ref, lw_ref, kk_ref, a_ref,
                     qc_o, y0_o, mc_o, cc_o, cl_o):
    masks = _wkv_masks()
    for u in range(nsub):
        sl = slice(u * CHUNK, (u + 1) * CHUNK)
        qc, y0, mc, cc, cl = _wkv_prep_chunk(r_ref[sl, :], k_ref[sl, :], v_ref[sl, :], lw_ref[sl, :],
                                             kk_ref[sl, :], a_ref[sl, :], masks, np_)
        qc_o[sl, :] = qc
        y0_o[sl, :] = y0
        mc_o[sl, :] = mc
        cc_o[sl, :] = cc
        cl_o[u] = cl


def _wkv_scan_kernel(nsub, np_, qc_ref, y0_ref, mc_ref, cc_ref, cl_ref, y_o, s_o, s_scr):
    step = pl.program_id(1)

    @pl.when(step == 0)
    def _():
        s_scr[...] = jnp.zeros_like(s_scr)

    diag = _head_of_lane((GW, GW), 0) == _head_of_lane((GW, GW), 1)
    s = s_scr[...]
    for u in range(nsub):
        sl = slice(u * CHUNK, (u + 1) * CHUNK)
        sp = _pieces(s, np_)
        y_o[sl, :] = _mm(_pieces(qc_ref[sl, :], np_), _bd(sp, diag), _NT) + y0_ref[sl, :]
        s = s * jnp.exp(cl_ref[u]) + _mm(sp, _bd(_pieces(mc_ref[sl, :], np_), diag), _NN) + cc_ref[sl, :]
    s_scr[...] = s

    @pl.when(step == pl.num_programs(1) - 1)
    def _():
        s_o[0, 0] = s


def _wkv_seq(r, k, v, lw, kk, a, batch, seq_len):
    t, d = r.shape
    nsub = 8
    rows = nsub * CHUNK
    ng = d // GW
    nblk = t // rows
    blk = pl.BlockSpec((rows, GW), lambda i, j: (i, j))
    act = jax.ShapeDtypeStruct((t, d), F32)
    cl_spec = pl.BlockSpec((nsub, 1, GW), lambda i, j: (i, 0, j))
    qc, y0, mc, cc, cl = pl.pallas_call(
        functools.partial(_wkv_prep_kernel, nsub, 3),
        grid=(nblk, ng),
        in_specs=[blk] * 6,
        out_specs=[blk] * 4 + [cl_spec],
        out_shape=[act] * 4 + [jax.ShapeDtypeStruct((t // CHUNK, 1, d), F32)],
        compiler_params=pltpu.CompilerParams(dimension_semantics=("parallel", "parallel"),
                                             vmem_limit_bytes=VMEM_LIMIT),
        name="wkv_prep",
    )(r, k, v, lw, kk, a)

    steps = seq_len // rows
    sblk = pl.BlockSpec((rows, GW), lambda i, j: ((i // ng) * steps + j, i % ng))
    scl = pl.BlockSpec((nsub, 1, GW), lambda i, j: ((i // ng) * steps + j, 0, i % ng))
    y, state = pl.pallas_call(
        functools.partial(_wkv_scan_kernel, nsub, 3),
        grid=(batch * ng, steps),
        in_specs=[sblk] * 4 + [scl],
        out_specs=[sblk, pl.BlockSpec((1, 1, HEAD_DIM, GW), lambda i, j: (i // ng, i % ng, 0, 0))],
        out_shape=[act, jax.ShapeDtypeStruct((batch, ng, HEAD_DIM, GW), F32)],
        scratch_shapes=[pltpu.VMEM((HEAD_DIM, GW), F32)],
        compiler_params=pltpu.CompilerParams(dimension_semantics=("parallel", "arbitrary"),
                                             vmem_limit_bytes=VMEM_LIMIT),
        name="wkv_scan",
    )(qc, y0, mc, cc, cl)
    return y, state


def _wkv_step_kernel(s_ref, r_ref, w_ref, k_ref, v_ref, a_ref, b_ref, so_ref, y_ref):
    n = HEAD_DIM
    eye = (lax.broadcasted_iota(jnp.int32, (n, n), 0) == lax.broadcasted_iota(jnp.int32, (n, n), 1)).astype(F32)
    s = s_ref[...]
    sa = jnp.sum(s * a_ref[...], axis=-1, keepdims=True)
    vcol = jnp.sum(eye * v_ref[...], axis=-1, keepdims=True)
    s = s * w_ref[...] + sa * b_ref[...] + vcol * k_ref[...]
    so_ref[...] = s
    ycol = jnp.sum(s * r_ref[...], axis=-1, keepdims=True)
    y_ref[...] = jnp.sum(eye * ycol, axis=-2, keepdims=True)


def _wkv_step(state, r, decay, k, v, a, b):
    n = state.shape[0]
    nb = 32
    sspec = pl.BlockSpec((nb, HEAD_DIM, HEAD_DIM), lambda i: (i, 0, 0))
    vspec = pl.BlockSpec((nb, 1, HEAD_DIM), lambda i: (i, 0, 0))
    return pl.pallas_call(
        _wkv_step_kernel,
        grid=(n // nb,),
        in_specs=[sspec] + [vspec] * 6,
        out_specs=[sspec, vspec],
        out_shape=[jax.ShapeDtypeStruct(state.shape, F32), jax.ShapeDtypeStruct((n, 1, HEAD_DIM), F32)],
        compiler_params=pltpu.CompilerParams(dimension_semantics=("parallel",)),
        name="wkv_step",
    )(state, r, decay, k, v, a, b)


def _post_kernel(rwkv, final, per_row, *refs):
    refs = list(refs)
    x_ref = refs.pop(0)
    z_ref = refs.pop(0)
    if rwkv:
        r_ref, k_ref, v_ref, g_ref = refs[:4]
        refs = refs[4:]
    mod_ref = refs.pop(0)
    if rwkv:
        lg_ref, lb_ref, rk_ref, ones_ref = refs[:4]
        refs = refs[4:]
    wo_ref, n2_ref, up_ref, down_ref = refs[:4]
    refs = refs[4:]
    if final:
        fg_ref = refs.pop(0)
    x_o = refs.pop(0)
    d = x_ref.shape[1]

    z = z_ref[...]
    if rwkv:
        ones = ones_ref[...]
        inv_n = 1.0 / HEAD_DIM

        def head_sum(val, n):
            return _mm(_pieces(val, n), [ones], _NN)

        mu = head_sum(z, 2) * inv_n
        dz = z - mu
        var = head_sum(dz * dz, 1) * inv_n
        yn = dz * lax.rsqrt(var + LNX_EPS) * lg_ref[...] + lb_ref[...]
        v = v_ref[...]
        bonus = head_sum(r_ref[...] * k_ref[...] * rk_ref[...], 2) * v
        z = (yn + bonus) * g_ref[...]
    out = _dot(_bf(z), wo_ref[...])
    x1 = x_ref[...] + _mod(mod_ref, per_row, 2, d) * out
    h2 = _bf(_prenorm(x1, n2_ref[...], _mod(mod_ref, per_row, 3, d), _mod(mod_ref, per_row, 4, d)))
    dff = up_ref.shape[1]
    fc = 1024
    acc = None
    for c in range(dff // fc):
        hid = jnp.maximum(_dot(h2, up_ref[:, c * fc:(c + 1) * fc]), 0.0)
        part = _dot(_bf(hid * hid), down_ref[c * fc:(c + 1) * fc, :])
        acc = part if acc is None else acc + part
    x2 = x1 + _mod(mod_ref, per_row, 5, d) * acc
    x_o[...] = x2
    if final:
        refs[0][...] = _rms(x2, fg_ref[...])


def _post(x, z, extra, mod, per_row, rw, wo, n2g, up, down, final_g, tm, seq_len):
    t, d = x.shape
    rwkv = rw is not None
    final = final_g is not None
    row = lambda i: (i, 0)
    tile = pl.BlockSpec((tm, d), row)
    tiles_per_seq = 1 if per_row else seq_len // tm
    mod_spec = (pl.BlockSpec((tm, N_ADA * d), row) if per_row
                else pl.BlockSpec((1, N_ADA, d), lambda i: (i // tiles_per_seq, 0, 0)))
    args = [x, z]
    specs = [tile, tile]
    if rwkv:
        args += list(extra)
        specs += [tile] * 4
    args.append(mod)
    specs.append(mod_spec)
    consts = (list(rw) if rwkv else []) + [wo, n2g, up, down] + ([final_g] if final else [])
    args += consts
    specs += [_const_spec(c.shape) for c in consts]
    act = jax.ShapeDtypeStruct((t, d), F32)
    outs = pl.pallas_call(
        functools.partial(_post_kernel, rwkv, final, per_row),
        grid=(t // tm,),
        in_specs=specs,
        out_specs=[tile] * (2 if final else 1),
        out_shape=[act] * (2 if final else 1),
        compiler_params=pltpu.CompilerParams(dimension_semantics=("parallel",), vmem_limit_bytes=VMEM_LIMIT),
        name=("post_rwkv" if rwkv else "post_attn") + ("_step" if per_row else "_seq"),
    )(*args)
    return outs


def _rope_table_kernel(base, step, invf_ref, c_o, sm_o, sp_o):
    shape = c_o.shape
    row = lax.broadcasted_iota(jnp.int32, shape, 0) + pl.program_id(0) * shape[0]
    pos = (base + step * row).astype(F32)
    ang = pos * invf_ref[...]
    lane = lax.broadcasted_iota(jnp.int32, shape, 1) % HEAD_DIM
    cos = jnp.cos(ang)
    sin = jnp.sin(ang)
    half = ROPE_DIM // 2
    c_o[...] = jnp.where(lane < ROPE_DIM, cos, 1.0)
    sm_o[...] = jnp.where(lane < half, -sin, 0.0)
    sp_o[...] = jnp.where((lane >= half) & (lane < ROPE_DIM), sin, 0.0)


def _rope_tables(n, base, step):
    half = ROPE_DIM // 2
    inv_freq = ROPE_THETA ** (-jnp.arange(half, dtype=F32) / half)
    lane = jnp.arange(LANE) % HEAD_DIM
    invf = jnp.where(lane < ROPE_DIM, inv_freq[lane % half], 0.0).reshape(1, LANE).astype(F32)
    tr = min(n, 512)
    spec = pl.BlockSpec((tr, LANE), lambda i: (i, 0))
    shp = jax.ShapeDtypeStruct((n, LANE), F32)
    return pl.pallas_call(
        functools.partial(_rope_table_kernel, base, step),
        grid=(n // tr,),
        in_specs=[pl.BlockSpec((1, LANE), lambda i: (0, 0))],
        out_specs=[spec] * 3,
        out_shape=[shp] * 3,
        compiler_params=pltpu.CompilerParams(dimension_semantics=("parallel",)),
        name="rope_tables",
    )(invf)


def _qkv_kernel(per_row, x_ref, mod_ref, ng_ref, w_ref, b_ref, c_ref, sm_ref, sp_ref, q_o, k_o, v_o):
    d = x_ref.shape[1]
    h = _prenorm(x_ref[...], ng_ref[...], _mod(mod_ref, per_row, 0, d), _mod(mod_ref, per_row, 1, d))
    qkv = _dot(_bf(h), w_ref[...]) + b_ref[...]
    c, sm, sp = c_ref[...], sm_ref[...], sp_ref[...]
    nq, nk = q_o.shape[1], k_o.shape[1]

    def rope_into(o_ref, off, width):
        for j in range(width // LANE):
            zc = qkv[:, off + j * LANE:off + (j + 1) * LANE]
            o_ref[:, j * LANE:(j + 1) * LANE] = (zc * c + pltpu.roll(zc, LANE - ROPE_DIM // 2, axis=1) * sm
                                                 + pltpu.roll(zc, ROPE_DIM // 2, axis=1) * sp)

    rope_into(q_o, 0, nq)
    rope_into(k_o, nq, nk)
    v_o[...] = qkv[:, nq + nk:]


def _qkv(x, mod, per_row, ng, w, b, tables, tm, seq_len):
    t, d = x.shape
    nkv = KV_HEADS * HEAD_DIM
    row = lambda i: (i, 0)
    tiles_per_seq = 1 if per_row else seq_len // tm
    mod_spec = (pl.BlockSpec((tm, N_ADA * d), row) if per_row
                else pl.BlockSpec((1, N_ADA, d), lambda i: (i // tiles_per_seq, 0, 0)))
    tab_spec = pl.BlockSpec((tm, LANE), lambda i: (i % tiles_per_seq, 0))
    return pl.pallas_call(
        functools.partial(_qkv_kernel, per_row),
        grid=(t // tm,),
        in_specs=[pl.BlockSpec((tm, d), row), mod_spec, _const_spec(ng.shape), _const_spec(w.shape),
                  _const_spec(b.shape)] + [tab_spec] * 3,
        out_specs=[pl.BlockSpec((tm, d), row), pl.BlockSpec((tm, nkv), row), pl.BlockSpec((tm, nkv), row)],
        out_shape=[jax.ShapeDtypeStruct((t, d), F32), jax.ShapeDtypeStruct((t, nkv), F32),
                   jax.ShapeDtypeStruct((t, nkv), F32)],
        compiler_params=pltpu.CompilerParams(dimension_semantics=("parallel",), vmem_limit_bytes=VMEM_LIMIT),
        name="qkv_step" if per_row else "qkv_seq",
    )(x, mod, ng, w, b, *tables)


def _sink_softmax(s, sink):
    m = jnp.maximum(jnp.max(s, axis=-1, keepdims=True), sink)
    p = jnp.exp(s - m)
    return p / (jnp.sum(p, axis=-1, keepdims=True) + jnp.exp(sink - m))


def _swa_seq_kernel(q_ref, kc_ref, kp_ref, vc_ref, vp_ref, sink_ref, o_ref):
    n = pl.program_id(1)
    q = q_ref[...]
    kall = jnp.concatenate([kp_ref[...], kc_ref[...]], axis=0)
    vall = jnp.concatenate([vp_ref[...], vc_ref[...]], axis=0)
    blk = ATT_BLOCK
    i = lax.broadcasted_iota(jnp.int32, (blk, 2 * blk), 0)
    j = lax.broadcasted_iota(jnp.int32, (blk, 2 * blk), 1)
    mask = (j >= i + (blk - WINDOW)) & (j <= i + blk) & ((n > 0) | (j >= blk))
    for kvh in range(KV_HEADS):
        ksl = slice(kvh * HEAD_DIM, (kvh + 1) * HEAD_DIM)
        kh = _bf(kall[:, ksl])
        vh = _bf(vall[:, ksl])
        for g in range(GROUP):
            hq = kvh * GROUP + g
            qsl = slice(hq * HEAD_DIM, (hq + 1) * HEAD_DIM)
            s = _dot_nt(_bf(q[:, qsl]), kh) * (HEAD_DIM ** -0.5)
            s = jnp.where(mask, s, -jnp.inf)
            pr = _sink_softmax(s, sink_ref[0:1, hq:hq + 1])
            o_ref[:, qsl] = _dot(_bf(pr), vh)


def _swa_seq(q, k, v, sink, batch, seq_len):
    t, d = q.shape
    nkv = k.shape[1]
    nb = seq_len // ATT_BLOCK
    cur = lambda b, n: (b * nb + n, 0)
    prv = lambda b, n: (b * nb + jnp.maximum(n - 1, 0), 0)
    return pl.pallas_call(
        _swa_seq_kernel,
        grid=(batch, nb),
        in_specs=[pl.BlockSpec((ATT_BLOCK, d), cur),
                  pl.BlockSpec((ATT_BLOCK, nkv), cur), pl.BlockSpec((ATT_BLOCK, nkv), prv),
                  pl.BlockSpec((ATT_BLOCK, nkv), cur), pl.BlockSpec((ATT_BLOCK, nkv), prv),
                  pl.BlockSpec(sink.shape, lambda b, n: (0, 0))],
        out_specs=pl.BlockSpec((ATT_BLOCK, d), cur),
        out_shape=jax.ShapeDtypeStruct((t, d), F32),
        compiler_params=pltpu.CompilerParams(dimension_semantics=("parallel", "parallel")),
        name="swa_seq",
    )(q, k, k, v, v, sink)


def _swa_step_kernel(q_ref, kn_ref, vn_ref, kc_ref, vc_ref, sink_ref, o_ref, ko_ref, vo_ref):
    w = kc_ref.shape[1]
    kc = kc_ref[...]
    vc = vc_ref[...]
    kn = kn_ref[...]
    vn = vn_ref[...]
    q = q_ref[...]
    scale = HEAD_DIM ** -0.5
    for kvh in range(KV_HEADS):
        ksl = slice(kvh * HEAD_DIM, (kvh + 1) * HEAD_DIM)
        kh, vh = kc[:, :, ksl], vc[:, :, ksl]
        knh, vnh = kn[:, :, ksl], vn[:, :, ksl]
        qh = q[:, kvh * GROUP:(kvh + 1) * GROUP, :]
        s_old = jnp.einsum("bgd,bkd->bgk", _bf(qh), _bf(kh), preferred_element_type=F32) * scale
        s_new = jnp.sum(_bf(qh).astype(F32) * _bf(knh).astype(F32), axis=-1, keepdims=True) * scale
        sink = sink_ref[kvh]
        m = jnp.maximum(jnp.maximum(jnp.max(s_old, axis=-1, keepdims=True), s_new), sink)
        p_old = jnp.exp(s_old - m)
        p_new = jnp.exp(s_new - m)
        den = jnp.sum(p_old, axis=-1, keepdims=True) + p_new + jnp.exp(sink - m)
        o = jnp.einsum("bgk,bkd->bgd", _bf(p_old / den), _bf(vh), preferred_element_type=F32)
        o = o + _bf(p_new / den).astype(F32) * _bf(vnh).astype(F32)
        o_ref[:, kvh * GROUP:(kvh + 1) * GROUP, :] = o
    rowid = lax.broadcasted_iota(jnp.int32, kc.shape, 1)
    ko_ref[...] = jnp.where(rowid == w - 1, kn, pltpu.roll(kc, w - 1, axis=1))
    vo_ref[...] = jnp.where(rowid == w - 1, vn, pltpu.roll(vc, w - 1, axis=1))


def _swa_step(q, kn, vn, cache_k, cache_v, sink):
    b, nh, hd = q.shape
    w, nkv = cache_k.shape[1], cache_k.shape[2]
    nb = 8
    i3 = lambda i: (i, 0, 0)
    cspec = pl.BlockSpec((nb, w, nkv), i3)
    nspec = pl.BlockSpec((nb, 1, nkv), i3)
    qspec = pl.BlockSpec((nb, nh, hd), i3)
    return pl.pallas_call(
        _swa_step_kernel,
        grid=(b // nb,),
        in_specs=[qspec, nspec, nspec, cspec, cspec, pl.BlockSpec(sink.shape, lambda i: (0, 0, 0))],
        out_specs=[qspec, cspec, cspec],
        out_shape=[jax.ShapeDtypeStruct(q.shape, F32), jax.ShapeDtypeStruct(cache_k.shape, F32),
                   jax.ShapeDtypeStruct(cache_v.shape, F32)],
        compiler_params=pltpu.CompilerParams(dimension_semantics=("parallel",)),
        name="swa_step",
    )(q, kn, vn, cache_k, cache_v, sink)


def kernel(x_prompt, x_sample, c_prompt, c_sample, state_wkv, state_shift, cache_k, cache_v, norm1_g, norm2_g, ada_w, ada_b, mlp_up, mlp_down, final_g, rw_mix, rw_wr, rw_wk, rw_wv, rw_wo, rw_w0, rw_w1, rw_w2, rw_a0, rw_a1, rw_a2, rw_g1, rw_g2, rw_kk, rw_ka, rw_rk, rw_lnx_g, rw_lnx_b, at_wqkv, at_bqkv, at_wo, at_sink):
    bp, seq, d = x_prompt.shape
    bs = x_sample.shape[0]
    assert x_sample.shape[1] == 1 and d % GW == 0 and seq % (8 * CHUNK) == 0
    nh = d // HEAD_DIM
    tp = bp * seq
    row = lambda vec: vec.reshape(1, -1)

    n_c = bp + bs
    pad = (-n_c) % 8
    c_all = jnp.concatenate([c_prompt, c_sample, jnp.zeros((pad, d), F32)], axis=0)
    ada = _ada(c_all, ada_w, ada_b)
    mod_p = [ada[i, :bp].reshape(bp, N_ADA, d) for i in range(2)]
    mod_s = [ada[i, bp:bp + bs] for i in range(2)]

    xp = x_prompt.reshape(tp, d)
    xs = x_sample.reshape(bs, d)

    head_id = jnp.arange(d) // HEAD_DIM
    ones_bd = (head_id[:, None] == head_id[None, :]).astype(BF16)

    rwp = dict(mix=rw_mix[0], wr=_bf(rw_wr[0]), wk=_bf(rw_wk[0]), wv=_bf(rw_wv[0]),
               w1=_bf(rw_w1[0]), w2=_bf(rw_w2[0]), w0=row(rw_w0[0]),
               a1=_bf(rw_a1[0]), a2=_bf(rw_a2[0]), a0=row(rw_a0[0]),
               g1=_bf(rw_g1[0]), g2=_bf(rw_g2[0]), kk=row(rw_kk[0]), ka=row(rw_ka[0]), ones=ones_bd)
    n1 = row(norm1_g[0])
    post_rw = (row(rw_lnx_g[0]), row(rw_lnx_b[0]), row(rw_rk[0]), ones_bd)
    wo0, up0, down0 = _bf(rw_wo[0]), _bf(mlp_up[0]), _bf(mlp_down[0])
    n2 = row(norm2_g[0])

    r, k, v, lw, kk, a, g, hlast = _rwkv_proj(xp, None, mod_p[0], n1, rwp, 256, seq)
    tiles = seq // 256
    shift_p = hlast.reshape(bp, tiles, 8, d)[:, -1, -1]
    y, st = _wkv_seq(r, k, v, lw, kk, a, bp, seq)
    wkv_p = st.reshape(bp, d // GW, HEAD_DIM, HPG, HEAD_DIM).transpose(0, 1, 3, 2, 4).reshape(bp, nh, HEAD_DIM, HEAD_DIM)
    (xp,) = _post(xp, y, (r, k, v, g), mod_p[0], False, post_rw, wo0, n2, up0, down0, None, 256, seq)

    rs, ks, vs, lws, kks, as_, gs, shift_s = _rwkv_proj(xs, state_shift[0], mod_s[0], n1, rwp, bs, 1)
    vec = lambda z: z.reshape(bs * nh, 1, HEAD_DIM)
    st_s, ys = _wkv_step(state_wkv[0].reshape(bs * nh, HEAD_DIM, HEAD_DIM), vec(rs), vec(jnp.exp(lws)),
                         vec(ks), vec(vs), vec(-kks), vec(kks * as_))
    wkv_s = st_s.reshape(bs, nh, HEAD_DIM, HEAD_DIM)
    (xs,) = _post(xs, ys.reshape(bs, d), (rs, ks, vs, gs), mod_s[0], True, post_rw, wo0, n2, up0, down0, None, bs, 1)

    wqkv, bqkv, wo1 = _bf(at_wqkv[0]), row(at_bqkv[0]), _bf(at_wo[0])
    n1, n2 = row(norm1_g[1]), row(norm2_g[1])
    up1, down1 = _bf(mlp_up[1]), _bf(mlp_down[1])
    fg = row(final_g)
    nkv = KV_HEADS * HEAD_DIM

    q, kq, vq = _qkv(xp, mod_p[1], False, n1, wqkv, bqkv, _rope_tables(seq, 0, 1), 256, seq)
    o = _swa_seq(q, kq, vq, row(at_sink[0]), bp, seq)
    keep = min(WINDOW, seq)
    k_p = kq.reshape(bp, seq, KV_HEADS, HEAD_DIM)[:, seq - keep:]
    v_p = vq.reshape(bp, seq, KV_HEADS, HEAD_DIM)[:, seq - keep:]
    xp, y_p = _post(xp, o, None, mod_p[1], False, None, wo1, n2, up1, down1, fg, 256, seq)

    qs, kn, vn = _qkv(xs, mod_s[1], True, n1, wqkv, bqkv, _rope_tables(bs, PAST_LEN, 0), bs, 1)
    w_buf = cache_k.shape[2]
    os_, k_s, v_s = _swa_step(qs.reshape(bs, nh, HEAD_DIM), kn.reshape(bs, 1, nkv), vn.reshape(bs, 1, nkv),
                              cache_k[0].reshape(bs, w_buf, nkv), cache_v[0].reshape(bs, w_buf, nkv),
                              at_sink[0].reshape(KV_HEADS, GROUP, 1))
    xs, y_s = _post(xs, os_.reshape(bs, d), None, mod_s[1], True, None, wo1, n2, up1, down1, fg, bs, 1)

    cshape = (1, bs, w_buf, KV_HEADS, HEAD_DIM)
    return (y_p.reshape(bp, seq, d), y_s.reshape(bs, 1, d),
            wkv_p[None], wkv_s[None], shift_p[None], shift_s[None],
            k_p[None], k_s.reshape(cshape), v_p[None], v_s.reshape(cshape))
```

```python
import functools

import jax
import jax.numpy as jnp
from jax import lax
from jax.experimental import pallas as pl
from jax.experimental.pallas import tpu as pltpu

F32 = jnp.float32
BF16 = jnp.bfloat16

HEAD_DIM = 64
KV_HEADS = 4
GROUP = 4
WINDOW = 128
ATT_BLOCK = 128
ROPE_DIM = HEAD_DIM // 4
ROPE_THETA = 500000.0
PAST_LEN = 8192
NORM_EPS = 1e-6
LNX_EPS = 64e-5
N_ADA = 6

LANE = 128
V7X_VMEM_BYTES = 64 * 1024 * 1024
VMEM_LIMIT = V7X_VMEM_BYTES * 7 // 8

CHUNK = 64
HPG = 4
GW = HPG * HEAD_DIM


def _bf(x):
    return x.astype(BF16)


def _dot(a, b):
    return jnp.dot(a, b, preferred_element_type=F32)


def _dot_nt(a, b):
    return lax.dot_general(a, b, (((1,), (1,)), ((), ())), preferred_element_type=F32)


def _rms(x, g):
    ms = jnp.mean(x * x, axis=-1, keepdims=True)
    return x * lax.rsqrt(ms + NORM_EPS) * g


def _prenorm(x, g, shift, scale):
    return _rms(x, g) * (1.0 + scale) + shift


def _mod(mod_ref, per_row, idx, d):
    if per_row:
        return mod_ref[:, idx * d:(idx + 1) * d]
    return mod_ref[0, idx:idx + 1, :]


def _pieces(x, n):
    out = []
    rem = x
    for i in range(n):
        p = rem.astype(BF16)
        out.append(p)
        if i + 1 < n:
            rem = rem - p.astype(F32)
    return out


def _mm(xp, yp, dn):
    n = max(len(xp), len(yp))
    acc = None
    for i, xi in enumerate(xp):
        for j, yj in enumerate(yp):
            if i + j < n:
                t = lax.dot_general(xi, yj, dn, preferred_element_type=F32)
                acc = t if acc is None else acc + t
    return acc


_NN = (((1,), (0,)), ((), ()))
_NT = (((1,), (1,)), ((), ()))


def _head_sum(val, ones, n):
    gw = ones.shape[0]
    cols = []
    for j in range(val.shape[1] // gw):
        cols.append(_mm(_pieces(val[:, j * gw:(j + 1) * gw], n), [ones], _NN))
    return jnp.concatenate(cols, axis=1)


def _ada_kernel(c_ref, w_ref, b_ref, o_ref):
    c = c_ref[...]
    s = c * jax.nn.sigmoid(c)
    o_ref[0] = _dot(_bf(s), _bf(w_ref[0])) + b_ref[0]


def _ada(c_all, ada_w, ada_b):
    depth, d, n = ada_w.shape
    m = c_all.shape[0]
    tn = 512
    return pl.pallas_call(
        _ada_kernel,
        grid=(depth, n // tn),
        in_specs=[
            pl.BlockSpec((m, d), lambda i, j: (0, 0)),
            pl.BlockSpec((1, d, tn), lambda i, j: (i, 0, j)),
            pl.BlockSpec((1, 1, tn), lambda i, j: (i, 0, j)),
        ],
        out_specs=pl.BlockSpec((1, m, tn), lambda i, j: (i, 0, j)),
        out_shape=jax.ShapeDtypeStruct((depth, m, n), F32),
        compiler_params=pltpu.CompilerParams(dimension_semantics=("parallel", "parallel")),
        name="ada",
    )(c_all, ada_w, ada_b.reshape(depth, 1, n))


def _softplus(x):
    return jnp.maximum(x, 0.0) + jnp.log1p(jnp.exp(-jnp.abs(x)))


def _rwkv_proj_kernel(per_row, tiles_per_seq,
                      x_ref, xp_ref, mod_ref, ng_ref, mix_ref, wr_ref, wk_ref, wv_ref,
                      w1_ref, w2_ref, w0_ref, a1_ref, a2_ref, a0_ref, g1_ref, g2_ref,
                      kk_ref, ka_ref, ones_ref,
                      r_o, k_o, v_o, lw_o, kk_o, a_o, g_o, h_o):
    d = x_ref.shape[1]
    tm = x_ref.shape[0]
    ng = ng_ref[...]
    shift = _mod(mod_ref, per_row, 0, d)
    scale = _mod(mod_ref, per_row, 1, d)
    h = _prenorm(x_ref[...], ng, shift, scale)
    if per_row:
        prev = xp_ref[...]
        h_o[...] = h
    else:
        hp = _prenorm(xp_ref[...], ng, shift, scale)
        first = (pl.program_id(0) % tiles_per_seq) == 0
        prow = jnp.where(first, 0.0, hp[7:8, :])
        rowid = lax.broadcasted_iota(jnp.int32, h.shape, 0)
        prev = jnp.where(rowid == 0, prow, pltpu.roll(h, 1, axis=0))
        h_o[0] = h[tm - 8:tm, :]
    xx = prev - h

    def mixed(j):
        return _bf(h + xx * mix_ref[j:j + 1, :])

    r = _dot(mixed(0), wr_ref[...])
    wl = w0_ref[...] + _dot(_bf(jnp.tanh(_dot(mixed(1), w1_ref[...]))), w2_ref[...])
    k = _dot(mixed(2), wk_ref[...])
    v = _dot(mixed(3), wv_ref[...])
    a = jax.nn.sigmoid(a0_ref[...] + _dot(_bf(_dot(mixed(4), a1_ref[...])), a2_ref[...]))
    g = _dot(_bf(jax.nn.sigmoid(_dot(mixed(5), g1_ref[...]))), g2_ref[...])
    w = -_softplus(-wl) - 0.5
    kk = k * kk_ref[...]
    ss = _head_sum(kk * kk, ones_ref[...], 1)
    kk = kk / jnp.maximum(jnp.sqrt(ss), 1e-12)
    r_o[...] = r
    k_o[...] = k * (1.0 + (a - 1.0) * ka_ref[...])
    v_o[...] = v
    lw_o[...] = -jnp.exp(w)
    kk_o[...] = kk
    a_o[...] = a
    g_o[...] = g


def _const_spec(shape):
    nd = len(shape)
    return pl.BlockSpec(shape, lambda *_: (0,) * nd, pipeline_mode=pl.Buffered(1))


def _rwkv_proj(x, prev_rows, mod, ng, p, tm, seq_len):
    t, d = x.shape
    per_row = prev_rows is not None
    nt = t // tm
    tiles_per_seq = 1 if per_row else seq_len // tm
    row = lambda i: (i, 0)
    if per_row:
        xp, xp_spec = prev_rows, pl.BlockSpec((tm, d), row)
        mod_spec = pl.BlockSpec((tm, N_ADA * d), row)
        h_shape, h_spec = jax.ShapeDtypeStruct((t, d), F32), pl.BlockSpec((tm, d), row)
    else:
        xp, xp_spec = x, pl.BlockSpec((8, d), lambda i: (jnp.maximum(i * (tm // 8) - 1, 0), 0))
        mod_spec = pl.BlockSpec((1, N_ADA, d), lambda i: (i // tiles_per_seq, 0, 0))
        h_shape, h_spec = jax.ShapeDtypeStruct((nt, 8, d), F32), pl.BlockSpec((1, 8, d), lambda i: (i, 0, 0))
    consts = [ng, p["mix"], p["wr"], p["wk"], p["wv"], p["w1"], p["w2"], p["w0"], p["a1"], p["a2"], p["a0"],
              p["g1"], p["g2"], p["kk"], p["ka"], p["ones"]]
    act = jax.ShapeDtypeStruct((t, d), F32)
    return pl.pallas_call(
        functools.partial(_rwkv_proj_kernel, per_row, tiles_per_seq),
        grid=(nt,),
        in_specs=[pl.BlockSpec((tm, d), row), xp_spec, mod_spec] + [_const_spec(c.shape) for c in consts],
        out_specs=[pl.BlockSpec((tm, d), row)] * 7 + [h_spec],
        out_shape=[act] * 7 + [h_shape],
        compiler_params=pltpu.CompilerParams(dimension_semantics=("parallel",), vmem_limit_bytes=VMEM_LIMIT),
        name="rwkv_proj_step" if per_row else "rwkv_proj_seq",
    )(x, xp, mod, *consts)


def _head_of_lane(shape, axis):
    return lax.broadcasted_iota(jnp.int32, shape, axis) // HEAD_DIM


def _bd(p, diag_mask):
    return jnp.where(diag_mask, jnp.concatenate([p] * HPG, axis=0), jnp.zeros((), p.dtype))


def _wkv_prep_chunks(rs, ks, vs, lws, kks, as_, masks):
    tri, diag, strict, incl, eye, blk8, levels, lane_head = masks
    n = len(rs)

    def each(f, *args):
        return [f(*[a[i] for a in args]) for i in range(n)]

    def pp(xs, ys):
        rhs = each(lambda y: _bd(_bf(y), diag), ys)
        return each(lambda x, b: _dot(_bf(x), b), xs, rhs)

    def pabt(xs, ys):
        rhs = each(lambda y: _bd(_bf(y), diag), ys)
        return each(lambda x, b: _dot_nt(_bf(x), b), xs, rhs)

    def pick_diag(gram):
        out = gram[0:HEAD_DIM, :]
        for hh in range(1, HPG):
            out = jnp.where(lane_head == hh, gram[hh * HEAD_DIM:(hh + 1) * HEAD_DIM, :], out)
        return out

    def patb(xs, ys):
        lhs = each(lambda x: _bf(x.T), xs)
        return each(lambda xt, y: pick_diag(_dot(xt, _bf(y))), lhs, ys)

    def add(xs, ys):
        return each(lambda x, y: x + y, xs, ys)

    def masked(m, xs):
        return each(lambda x: jnp.where(m, x, 0.0), xs)

    c = each(lambda lw: _mm([tri], _pieces(lw, 3), _NN), lws)
    c_last = each(lambda ci: ci[CHUNK - 1:CHUNK, :], c)
    e_neg = each(lambda ci: jnp.exp(-ci), c)
    e_end = each(lambda ci, cl: jnp.exp(cl - ci), c, c_last)
    bv = each(lambda kk, a: kk * a, kks, as_)
    a_t = each(lambda kk, ci, lw: -kk * jnp.exp(ci - lw), kks, c, lws)
    r_t = each(lambda r, ci: r * jnp.exp(ci), rs, c)
    b_t = each(lambda b, e: b * e, bv, e_neg)
    k_t = each(lambda k, e: k * e, ks, e_neg)
    b_e = each(lambda b, e: b * e, bv, e_end)
    k_e = each(lambda k, e: k * e, ks, e_end)

    ar = each(lambda x, y: jnp.concatenate([x, y], axis=0), a_t, r_t)
    xb = pabt(ar, b_t)
    xk = pabt(ar, k_t)
    a_ab = each(lambda x: jnp.where(strict, x[:CHUNK], 0.0), xb)
    a_rb = each(lambda x: jnp.where(incl, x[CHUNK:], 0.0), xb)
    a_ak = each(lambda x: jnp.where(strict, x[:CHUNK], 0.0), xk)
    a_rk = each(lambda x: jnp.where(incl, x[CHUNK:], 0.0), xk)

    n0 = masked(blk8, a_ab)
    n2 = pp(n0, n0)
    n4 = pp(n2, n2)
    plus_eye = lambda xs: each(lambda x: eye + x, xs)
    tinv = pp(pp(plus_eye(n0), plus_eye(n2)), plus_eye(n4))
    for lvl in levels:
        tinv = add(tinv, pp(pp(tinv, masked(lvl, a_ab)), tinv))

    w1 = pp(tinv, a_t)
    w2 = pp(tinv, pp(a_ak, vs))
    qc = add(r_t, pp(a_rb, w1))
    y0 = add(pp(a_rb, w2), pp(a_rk, vs))
    mc = patb(w1, b_e)
    cc = add(patb(w2, b_e), patb(vs, k_e))
    return qc, y0, mc, cc, c_last


def _wkv_masks():
    t = lax.broadcasted_iota(jnp.int32, (CHUNK, GW), 0)
    s = lax.broadcasted_iota(jnp.int32, (CHUNK, GW), 1) % HEAD_DIM
    tri = (lax.broadcasted_iota(jnp.int32, (CHUNK, CHUNK), 1)
           <= lax.broadcasted_iota(jnp.int32, (CHUNK, CHUNK), 0)).astype(BF16)
    diag = _head_of_lane((GW, GW), 0) == _head_of_lane((GW, GW), 1)
    strict = s < t
    incl = s <= t
    eye = (s == t).astype(F32)
    blk8 = strict & ((s // 8) == (t // 8))
    levels = []
    b = 8
    while b < CHUNK:
        levels.append(((s // (2 * b)) == (t // (2 * b))) & ((s // b) != (t // b)) & strict)
        b *= 2
    lane_head = _head_of_lane((HEAD_DIM, GW), 1)
    return tri, diag, strict, incl, eye, blk8, levels, lane_head


def _wkv_prep_kernel(nsub, r_ref, k_ref, v_ref, lw_ref, kk_ref, a_ref,
                     qc_o, y0_o, mc_o, cc_o, cl_o):
    masks = _wkv_masks()
    sls = [slice(u * CHUNK, (u + 1) * CHUNK) for u in range(nsub)]
    chunks = lambda ref: [ref[sl, :] for sl in sls]
    qc, y0, mc, cc, cl = _wkv_prep_chunks(chunks(r_ref), chunks(k_ref), chunks(v_ref), chunks(lw_ref),
                                          chunks(kk_ref), chunks(a_ref), masks)
    for u, sl in enumerate(sls):
        qc_o[sl, :] = qc[u]
        y0_o[sl, :] = y0[u]
        mc_o[sl, :] = mc[u]
        cc_o[sl, :] = cc[u]
        cl_o[u] = cl[u]


def _wkv_scan_kernel(nsub, ng, qc_ref, y0_ref, mc_ref, cc_ref, cl_ref, y_o, s_o, s_scr):
    step = pl.program_id(1)

    @pl.when(step == 0)
    def _():
        s_scr[...] = jnp.zeros_like(s_scr)

    diag = _head_of_lane((GW, GW), 0) == _head_of_lane((GW, GW), 1)
    states = [s_scr[:, g * GW:(g + 1) * GW] for g in range(ng)]
    gls = [slice(g * GW, (g + 1) * GW) for g in range(ng)]
    for u in range(nsub):
        sl = slice(u * CHUNK, (u + 1) * CHUNK)
        sb = [_bf(s) for s in states]
        upd = [_dot(sb[g], _bd(_bf(mc_ref[sl, gl]), diag)) for g, gl in enumerate(gls)]
        ys = [_dot_nt(_bf(qc_ref[sl, gl]), _bd(sb[g], diag)) for g, gl in enumerate(gls)]
        for g, gl in enumerate(gls):
            states[g] = states[g] * jnp.exp(cl_ref[u, :, gl]) + upd[g] + cc_ref[sl, gl]
            y_o[sl, gl] = ys[g] + y0_ref[sl, gl]
    for g in range(ng):
        s_scr[:, g * GW:(g + 1) * GW] = states[g]

    @pl.when(step == pl.num_programs(1) - 1)
    def _():
        for g in range(ng):
            s_o[0, g] = states[g]


def _wkv_seq(r, k, v, lw, kk, a, batch, seq_len):
    t, d = r.shape
    nsub = 8
    rows = nsub * CHUNK
    ng = d // GW
    nblk = t // rows
    blk = pl.BlockSpec((rows, GW), lambda i, j: (i, j))
    act = jax.ShapeDtypeStruct((t, d), F32)
    cl_spec = pl.BlockSpec((nsub, 1, GW), lambda i, j: (i, 0, j))
    qc, y0, mc, cc, cl = pl.pallas_call(
        functools.partial(_wkv_prep_kernel, nsub),
        grid=(nblk, ng),
        in_specs=[blk] * 6,
        out_specs=[blk] * 4 + [cl_spec],
        out_shape=[act] * 4 + [jax.ShapeDtypeStruct((t // CHUNK, 1, d), F32)],
        compiler_params=pltpu.CompilerParams(dimension_semantics=("parallel", "parallel"),
                                             vmem_limit_bytes=VMEM_LIMIT),
        name="wkv_prep",
    )(r, k, v, lw, kk, a)

    steps = seq_len // rows
    sblk = pl.BlockSpec((rows, d), lambda i, j: (i * steps + j, 0))
    scl = pl.BlockSpec((nsub, 1, d), lambda i, j: (i * steps + j, 0, 0))
    y, state = pl.pallas_call(
        functools.partial(_wkv_scan_kernel, nsub, ng),
        grid=(batch, steps),
        in_specs=[sblk] * 4 + [scl],
        out_specs=[sblk, pl.BlockSpec((1, ng, HEAD_DIM, GW), lambda i, j: (i, 0, 0, 0))],
        out_shape=[act, jax.ShapeDtypeStruct((batch, ng, HEAD_DIM, GW), F32)],
        scratch_shapes=[pltpu.VMEM((HEAD_DIM, d), F32)],
        compiler_params=pltpu.CompilerParams(dimension_semantics=("parallel", "arbitrary"),
                                             vmem_limit_bytes=VMEM_LIMIT),
        name="wkv_scan",
    )(qc, y0, mc, cc, cl)
    return y, state


def _wkv_step_kernel(s_ref, r_ref, w_ref, k_ref, v_ref, a_ref, b_ref, so_ref, y_ref):
    n = HEAD_DIM
    eye = (lax.broadcasted_iota(jnp.int32, (n, n), 0) == lax.broadcasted_iota(jnp.int32, (n, n), 1)).astype(F32)
    s = s_ref[...]
    sa = jnp.sum(s * a_ref[...], axis=-1, keepdims=True)
    vcol = jnp.sum(eye * v_ref[...], axis=-1, keepdims=True)
    s = s * w_ref[...] + sa * b_ref[...] + vcol * k_ref[...]
    so_ref[...] = s
    ycol = jnp.sum(s * r_ref[...], axis=-1, keepdims=True)
    y_ref[...] = jnp.sum(eye * ycol, axis=-2, keepdims=True)


def _wkv_step(state, r, decay, k, v, a, b):
    n = state.shape[0]
    nb = 32
    sspec = pl.BlockSpec((nb, HEAD_DIM, HEAD_DIM), lambda i: (i, 0, 0))
    vspec = pl.BlockSpec((nb, 1, HEAD_DIM), lambda i: (i, 0, 0))
    return pl.pallas_call(
        _wkv_step_kernel,
        grid=(n // nb,),
        in_specs=[sspec] + [vspec] * 6,
        out_specs=[sspec, vspec],
        out_shape=[jax.ShapeDtypeStruct(state.shape, F32), jax.ShapeDtypeStruct((n, 1, HEAD_DIM), F32)],
        compiler_params=pltpu.CompilerParams(dimension_semantics=("parallel",)),
        name="wkv_step",
    )(state, r, decay, k, v, a, b)


def _post_kernel(rwkv, final, per_row, *refs):
    refs = list(refs)
    x_ref = refs.pop(0)
    z_ref = refs.pop(0)
    if rwkv:
        r_ref, k_ref, v_ref, g_ref = refs[:4]
        refs = refs[4:]
    mod_ref = refs.pop(0)
    if rwkv:
        lg_ref, lb_ref, rk_ref, ones_ref = refs[:4]
        refs = refs[4:]
    wo_ref, n2_ref, up_ref, down_ref = refs[:4]
    refs = refs[4:]
    if final:
        fg_ref = refs.pop(0)
    x_o = refs.pop(0)
    d = x_ref.shape[1]

    z = z_ref[...]
    if rwkv:
        ones = ones_ref[...]
        inv_n = 1.0 / HEAD_DIM

        mu = _head_sum(z, ones, 2) * inv_n
        dz = z - mu
        var = _head_sum(dz * dz, ones, 1) * inv_n
        yn = dz * lax.rsqrt(var + LNX_EPS) * lg_ref[...] + lb_ref[...]
        v = v_ref[...]
        bonus = _head_sum(r_ref[...] * k_ref[...] * rk_ref[...], ones, 2) * v
        z = (yn + bonus) * g_ref[...]
    out = _dot(_bf(z), wo_ref[...])
    x1 = x_ref[...] + _mod(mod_ref, per_row, 2, d) * out
    h2 = _bf(_prenorm(x1, n2_ref[...], _mod(mod_ref, per_row, 3, d), _mod(mod_ref, per_row, 4, d)))
    dff = up_ref.shape[1]
    fc = 1024
    acc = None
    for c in range(dff // fc):
        hid = jnp.maximum(_dot(h2, up_ref[:, c * fc:(c + 1) * fc]), 0.0)
        part = _dot(_bf(hid * hid), down_ref[c * fc:(c + 1) * fc, :])
        acc = part if acc is None else acc + part
    x2 = x1 + _mod(mod_ref, per_row, 5, d) * acc
    x_o[...] = x2
    if final:
        refs[0][...] = _rms(x2, fg_ref[...])


def _post(x, z, extra, mod, per_row, rw, wo, n2g, up, down, final_g, tm, seq_len):
    t, d = x.shape
    rwkv = rw is not None
    final = final_g is not None
    row = lambda i: (i, 0)
    tile = pl.BlockSpec((tm, d), row)
    tiles_per_seq = 1 if per_row else seq_len // tm
    mod_spec = (pl.BlockSpec((tm, N_ADA * d), row) if per_row
                else pl.BlockSpec((1, N_ADA, d), lambda i: (i // tiles_per_seq, 0, 0)))
    args = [x, z]
    specs = [tile, tile]
    if rwkv:
        args += list(extra)
        specs += [tile] * 4
    args.append(mod)
    specs.append(mod_spec)
    consts = (list(rw) if rwkv else []) + [wo, n2g, up, down] + ([final_g] if final else [])
    args += consts
    specs += [_const_spec(c.shape) for c in consts]
    act = jax.ShapeDtypeStruct((t, d), F32)
    outs = pl.pallas_call(
        functools.partial(_post_kernel, rwkv, final, per_row),
        grid=(t // tm,),
        in_specs=specs,
        out_specs=[tile] * (2 if final else 1),
        out_shape=[act] * (2 if final else 1),
        compiler_params=pltpu.CompilerParams(dimension_semantics=("parallel",), vmem_limit_bytes=VMEM_LIMIT),
        name=("post_rwkv" if rwkv else "post_attn") + ("_step" if per_row else "_seq"),
    )(*args)
    return outs


def _rope_table_kernel(base, step, invf_ref, c_o, sm_o, sp_o):
    shape = c_o.shape
    row = lax.broadcasted_iota(jnp.int32, shape, 0) + pl.program_id(0) * shape[0]
    pos = (base + step * row).astype(F32)
    ang = pos * invf_ref[...]
    lane = lax.broadcasted_iota(jnp.int32, shape, 1) % HEAD_DIM
    cos = jnp.cos(ang)
    sin = jnp.sin(ang)
    half = ROPE_DIM // 2
    c_o[...] = jnp.where(lane < ROPE_DIM, cos, 1.0)
    sm_o[...] = jnp.where(lane < half, -sin, 0.0)
    sp_o[...] = jnp.where((lane >= half) & (lane < ROPE_DIM), sin, 0.0)


def _rope_tables(n, base, step):
    half = ROPE_DIM // 2
    inv_freq = ROPE_THETA ** (-jnp.arange(half, dtype=F32) / half)
    lane = jnp.arange(LANE) % HEAD_DIM
    invf = jnp.where(lane < ROPE_DIM, inv_freq[lane % half], 0.0).reshape(1, LANE).astype(F32)
    tr = min(n, 512)
    spec = pl.BlockSpec((tr, LANE), lambda i: (i, 0))
    shp = jax.ShapeDtypeStruct((n, LANE), F32)
    return pl.pallas_call(
        functools.partial(_rope_table_kernel, base, step),
        grid=(n // tr,),
        in_specs=[pl.BlockSpec((1, LANE), lambda i: (0, 0))],
        out_specs=[spec] * 3,
        out_shape=[shp] * 3,
        compiler_params=pltpu.CompilerParams(dimension_semantics=("parallel",)),
        name="rope_tables",
    )(invf)


def _qkv_kernel(per_row, x_ref, mod_ref, ng_ref, w_ref, b_ref, c_ref, sm_ref, sp_ref, q_o, k_o, v_o):
    d = x_ref.shape[1]
    h = _prenorm(x_ref[...], ng_ref[...], _mod(mod_ref, per_row, 0, d), _mod(mod_ref, per_row, 1, d))
    qkv = _dot(_bf(h), w_ref[...]) + b_ref[...]
    c, sm, sp = c_ref[...], sm_ref[...], sp_ref[...]
    nq, nk = q_o.shape[1], k_o.shape[1]

    def rope_into(o_ref, off, width):
        for j in range(width // LANE):
            zc = qkv[:, off + j * LANE:off + (j + 1) * LANE]
            o_ref[:, j * LANE:(j + 1) * LANE] = (zc * c + pltpu.roll(zc, LANE - ROPE_DIM // 2, axis=1) * sm
                                                 + pltpu.roll(zc, ROPE_DIM // 2, axis=1) * sp)

    rope_into(q_o, 0, nq)
    rope_into(k_o, nq, nk)
    v_o[...] = qkv[:, nq + nk:]


def _qkv(x, mod, per_row, ng, w, b, tables, tm, seq_len):
    t, d = x.shape
    nkv = KV_HEADS * HEAD_DIM
    row = lambda i: (i, 0)
    tiles_per_seq = 1 if per_row else seq_len // tm
    mod_spec = (pl.BlockSpec((tm, N_ADA * d), row) if per_row
                else pl.BlockSpec((1, N_ADA, d), lambda i: (i // tiles_per_seq, 0, 0)))
    tab_spec = pl.BlockSpec((tm, LANE), lambda i: (i % tiles_per_seq, 0))
    return pl.pallas_call(
        functools.partial(_qkv_kernel, per_row),
        grid=(t // tm,),
        in_specs=[pl.BlockSpec((tm, d), row), mod_spec, _const_spec(ng.shape), _const_spec(w.shape),
                  _const_spec(b.shape)] + [tab_spec] * 3,
        out_specs=[pl.BlockSpec((tm, d), row), pl.BlockSpec((tm, nkv), row), pl.BlockSpec((tm, nkv), row)],
        out_shape=[jax.ShapeDtypeStruct((t, d), F32), jax.ShapeDtypeStruct((t, nkv), F32),
                   jax.ShapeDtypeStruct((t, nkv), F32)],
        compiler_params=pltpu.CompilerParams(dimension_semantics=("parallel",), vmem_limit_bytes=VMEM_LIMIT),
        name="qkv_step" if per_row else "qkv_seq",
    )(x, mod, ng, w, b, *tables)


def _sink_softmax(s, sink):
    m = jnp.maximum(jnp.max(s, axis=-1, keepdims=True), sink)
    p = jnp.exp(s - m)
    return p / (jnp.sum(p, axis=-1, keepdims=True) + jnp.exp(sink - m))


def _swa_seq_kernel(q_ref, kc_ref, kp_ref, vc_ref, vp_ref, sink_ref, o_ref):
    n = pl.program_id(1)
    q = q_ref[...]
    kall = jnp.concatenate([kp_ref[...], kc_ref[...]], axis=0)
    vall = jnp.concatenate([vp_ref[...], vc_ref[...]], axis=0)
    blk = ATT_BLOCK
    i = lax.broadcasted_iota(jnp.int32, (blk, 2 * blk), 0)
    j = lax.broadcasted_iota(jnp.int32, (blk, 2 * blk), 1)
    mask = (j >= i + (blk - WINDOW)) & (j <= i + blk) & ((n > 0) | (j >= blk))
    for kvh in range(KV_HEADS):
        ksl = slice(kvh * HEAD_DIM, (kvh + 1) * HEAD_DIM)
        kh = _bf(kall[:, ksl])
        vh = _bf(vall[:, ksl])
        for g in range(GROUP):
            hq = kvh * GROUP + g
            qsl = slice(hq * HEAD_DIM, (hq + 1) * HEAD_DIM)
            s = _dot_nt(_bf(q[:, qsl]), kh) * (HEAD_DIM ** -0.5)
            s = jnp.where(mask, s, -jnp.inf)
            pr = _sink_softmax(s, sink_ref[0:1, hq:hq + 1])
            o_ref[:, qsl] = _dot(_bf(pr), vh)


def _swa_seq(q, k, v, sink, batch, seq_len):
    t, d = q.shape
    nkv = k.shape[1]
    nb = seq_len // ATT_BLOCK
    cur = lambda b, n: (b * nb + n, 0)
    prv = lambda b, n: (b * nb + jnp.maximum(n - 1, 0), 0)
    return pl.pallas_call(
        _swa_seq_kernel,
        grid=(batch, nb),
        in_specs=[pl.BlockSpec((ATT_BLOCK, d), cur),
                  pl.BlockSpec((ATT_BLOCK, nkv), cur), pl.BlockSpec((ATT_BLOCK, nkv), prv),
                  pl.BlockSpec((ATT_BLOCK, nkv), cur), pl.BlockSpec((ATT_BLOCK, nkv), prv),
                  pl.BlockSpec(sink.shape, lambda b, n: (0, 0))],
        out_specs=pl.BlockSpec((ATT_BLOCK, d), cur),
        out_shape=jax.ShapeDtypeStruct((t, d), F32),
        compiler_params=pltpu.CompilerParams(dimension_semantics=("parallel", "parallel")),
        name="swa_seq",
    )(q, k, k, v, v, sink)


def _swa_step_kernel(q_ref, kn_ref, vn_ref, kc_ref, vc_ref, sink_ref, o_ref, ko_ref, vo_ref):
    w = kc_ref.shape[1]
    kc = kc_ref[...]
    vc = vc_ref[...]
    kn = kn_ref[...]
    vn = vn_ref[...]
    q = q_ref[...]
    scale = HEAD_DIM ** -0.5
    for kvh in range(KV_HEADS):
        ksl = slice(kvh * HEAD_DIM, (kvh + 1) * HEAD_DIM)
        kh, vh = kc[:, :, ksl], vc[:, :, ksl]
        knh, vnh = kn[:, :, ksl], vn[:, :, ksl]
        qh = q[:, kvh * GROUP:(kvh + 1) * GROUP, :]
        s_old = jnp.einsum("bgd,bkd->bgk", _bf(qh), _bf(kh), preferred_element_type=F32) * scale
        s_new = jnp.sum(_bf(qh).astype(F32) * _bf(knh).astype(F32), axis=-1, keepdims=True) * scale
        sink = sink_ref[kvh]
        m = jnp.maximum(jnp.maximum(jnp.max(s_old, axis=-1, keepdims=True), s_new), sink)
        p_old = jnp.exp(s_old - m)
        p_new = jnp.exp(s_new - m)
        den = jnp.sum(p_old, axis=-1, keepdims=True) + p_new + jnp.exp(sink - m)
        o = jnp.einsum("bgk,bkd->bgd", _bf(p_old / den), _bf(vh), preferred_element_type=F32)
        o = o + _bf(p_new / den).astype(F32) * _bf(vnh).astype(F32)
        o_ref[:, kvh * GROUP:(kvh + 1) * GROUP, :] = o
    rowid = lax.broadcasted_iota(jnp.int32, kc.shape, 1)
    ko_ref[...] = jnp.where(rowid == w - 1, kn, pltpu.roll(kc, w - 1, axis=1))
    vo_ref[...] = jnp.where(rowid == w - 1, vn, pltpu.roll(vc, w - 1, axis=1))


def _swa_step(q, kn, vn, cache_k, cache_v, sink):
    b, nh, hd = q.shape
    w, nkv = cache_k.shape[1], cache_k.shape[2]
    nb = 8
    i3 = lambda i: (i, 0, 0)
    cspec = pl.BlockSpec((nb, w, nkv), i3)
    nspec = pl.BlockSpec((nb, 1, nkv), i3)
    qspec = pl.BlockSpec((nb, nh, hd), i3)
    return pl.pallas_call(
        _swa_step_kernel,
        grid=(b // nb,),
        in_specs=[qspec, nspec, nspec, cspec, cspec, pl.BlockSpec(sink.shape, lambda i: (0, 0, 0))],
        out_specs=[qspec, cspec, cspec],
        out_shape=[jax.ShapeDtypeStruct(q.shape, F32), jax.ShapeDtypeStruct(cache_k.shape, F32),
                   jax.ShapeDtypeStruct(cache_v.shape, F32)],
        compiler_params=pltpu.CompilerParams(dimension_semantics=("parallel",)),
        name="swa_step",
    )(q, kn, vn, cache_k, cache_v, sink)


def kernel(x_prompt, x_sample, c_prompt, c_sample, state_wkv, state_shift, cache_k, cache_v, norm1_g, norm2_g, ada_w, ada_b, mlp_up, mlp_down, final_g, rw_mix, rw_wr, rw_wk, rw_wv, rw_wo, rw_w0, rw_w1, rw_w2, rw_a0, rw_a1, rw_a2, rw_g1, rw_g2, rw_kk, rw_ka, rw_rk, rw_lnx_g, rw_lnx_b, at_wqkv, at_bqkv, at_wo, at_sink):
    bp, seq, d = x_prompt.shape
    bs = x_sample.shape[0]
    assert x_sample.shape[1] == 1 and d % GW == 0 and seq % (8 * CHUNK) == 0
    nh = d // HEAD_DIM
    tp = bp * seq
    row = lambda vec: vec.reshape(1, -1)

    n_c = bp + bs
    pad = (-n_c) % 8
    c_all = jnp.concatenate([c_prompt, c_sample, jnp.zeros((pad, d), F32)], axis=0)
    ada = _ada(c_all, ada_w, ada_b)
    mod_p = [ada[i, :bp].reshape(bp, N_ADA, d) for i in range(2)]
    mod_s = [ada[i, bp:bp + bs] for i in range(2)]

    xp = x_prompt.reshape(tp, d)
    xs = x_sample.reshape(bs, d)

    head_id = jnp.arange(GW) // HEAD_DIM
    ones_bd = (head_id[:, None] == head_id[None, :]).astype(BF16)

    rwp = dict(mix=rw_mix[0], wr=_bf(rw_wr[0]), wk=_bf(rw_wk[0]), wv=_bf(rw_wv[0]),
               w1=_bf(rw_w1[0]), w2=_bf(rw_w2[0]), w0=row(rw_w0[0]),
               a1=_bf(rw_a1[0]), a2=_bf(rw_a2[0]), a0=row(rw_a0[0]),
               g1=_bf(rw_g1[0]), g2=_bf(rw_g2[0]), kk=row(rw_kk[0]), ka=row(rw_ka[0]), ones=ones_bd)
    n1 = row(norm1_g[0])
    post_rw = (row(rw_lnx_g[0]), row(rw_lnx_b[0]), row(rw_rk[0]), ones_bd)
    wo0, up0, down0 = _bf(rw_wo[0]), _bf(mlp_up[0]), _bf(mlp_down[0])
    n2 = row(norm2_g[0])

    r, k, v, lw, kk, a, g, hlast = _rwkv_proj(xp, None, mod_p[0], n1, rwp, 256, seq)
    tiles = seq // 256
    shift_p = hlast.reshape(bp, tiles, 8, d)[:, -1, -1]
    y, st = _wkv_seq(r, k, v, lw, kk, a, bp, seq)
    wkv_p = st.reshape(bp, d // GW, HEAD_DIM, HPG, HEAD_DIM).transpose(0, 1, 3, 2, 4).reshape(bp, nh, HEAD_DIM, HEAD_DIM)
    (xp,) = _post(xp, y, (r, k, v, g), mod_p[0], False, post_rw, wo0, n2, up0, down0, None, 256, seq)

    rs, ks, vs, lws, kks, as_, gs, shift_s = _rwkv_proj(xs, state_shift[0], mod_s[0], n1, rwp, bs, 1)
    vec = lambda z: z.reshape(bs * nh, 1, HEAD_DIM)
    st_s, ys = _wkv_step(state_wkv[0].reshape(bs * nh, HEAD_DIM, HEAD_DIM), vec(rs), vec(jnp.exp(lws)),
                         vec(ks), vec(vs), vec(-kks), vec(kks * as_))
    wkv_s = st_s.reshape(bs, nh, HEAD_DIM, HEAD_DIM)
    (xs,) = _post(xs, ys.reshape(bs, d), (rs, ks, vs, gs), mod_s[0], True, post_rw, wo0, n2, up0, down0, None, bs, 1)

    wqkv, bqkv, wo1 = _bf(at_wqkv[0]), row(at_bqkv[0]), _bf(at_wo[0])
    n1, n2 = row(norm1_g[1]), row(norm2_g[1])
    up1, down1 = _bf(mlp_up[1]), _bf(mlp_down[1])
    fg = row(final_g)
    nkv = KV_HEADS * HEAD_DIM

    q, kq, vq = _qkv(xp, mod_p[1], False, n1, wqkv, bqkv, _rope_tables(seq, 0, 1), 256, seq)
    o = _swa_seq(q, kq, vq, row(at_sink[0]), bp, seq)
    keep = min(WINDOW, seq)
    k_p = kq.reshape(bp, seq, KV_HEADS, HEAD_DIM)[:, seq - keep:]
    v_p = vq.reshape(bp, seq, KV_HEADS, HEAD_DIM)[:, seq - keep:]
    xp, y_p = _post(xp, o, None, mod_p[1], False, None, wo1, n2, up1, down1, fg, 256, seq)

    qs, kn, vn = _qkv(xs, mod_s[1], True, n1, wqkv, bqkv, _rope_tables(bs, PAST_LEN, 0), bs, 1)
    w_buf = cache_k.shape[2]
    os_, k_s, v_s = _swa_step(qs.reshape(bs, nh, HEAD_DIM), kn.reshape(bs, 1, nkv), vn.reshape(bs, 1, nkv),
                              cache_k[0].reshape(bs, w_buf, nkv), cache_v[0].reshape(bs, w_buf, nkv),
                              at_sink[0].reshape(KV_HEADS, GROUP, 1))
    xs, y_s = _post(xs, os_.reshape(bs, d), None, mod_s[1], True, None, wo1, n2, up1, down1, fg, bs, 1)

    cshape = (1, bs, w_buf, KV_HEADS, HEAD_DIM)
    return (y_p.reshape(bp, seq, d), y_s.reshape(bs, 1, d),
            wkv_p[None], wkv_s[None], shift_p[None], shift_s[None],
            k_p[None], k_s.reshape(cshape), v_p[None], v_s.reshape(cshape))
```

```python
import functools
import math

import jax
import jax.numpy as jnp
from jax import lax
from jax.experimental import pallas as pl
from jax.experimental.pallas import tpu as pltpu

F32 = jnp.float32
BF16 = jnp.bfloat16

HEAD_DIM = 64
KV_HEADS = 4
GROUP = 4
WINDOW = 128
ATT_BLOCK = 128
ROPE_DIM = HEAD_DIM // 4
ROPE_THETA = 500000.0
PAST_LEN = 8192
NORM_EPS = 1e-6
LNX_EPS = 64e-5
N_ADA = 6

LANE = 128
V7X_VMEM_BYTES = 64 * 1024 * 1024
VMEM_LIMIT = V7X_VMEM_BYTES * 7 // 8

CHUNK = 64
HPG = 4
GW = HPG * HEAD_DIM


def _bf(x):
    return x.astype(BF16)


def _dot(a, b):
    return jnp.dot(a, b, preferred_element_type=F32)


def _dot_nt(a, b):
    return lax.dot_general(a, b, (((1,), (1,)), ((), ())), preferred_element_type=F32)


def _rms(x, g):
    ms = jnp.mean(x * x, axis=-1, keepdims=True)
    return x * lax.rsqrt(ms + NORM_EPS) * g


def _prenorm(x, g, shift, scale):
    return _rms(x, g) * (1.0 + scale) + shift


def _mod(mod_ref, per_row, idx, d):
    if per_row:
        return mod_ref[:, idx * d:(idx + 1) * d]
    return mod_ref[0, idx:idx + 1, :]


def _pieces(x, n):
    out = []
    rem = x
    for i in range(n):
        p = rem.astype(BF16)
        out.append(p)
        if i + 1 < n:
            rem = rem - p.astype(F32)
    return out


def _mm(xp, yp, dn):
    n = max(len(xp), len(yp))
    acc = None
    for i, xi in enumerate(xp):
        for j, yj in enumerate(yp):
            if i + j < n:
                t = lax.dot_general(xi, yj, dn, preferred_element_type=F32)
                acc = t if acc is None else acc + t
    return acc


_NN = (((1,), (0,)), ((), ()))
_NT = (((1,), (1,)), ((), ()))


def _head_sum(val, ones, n):
    gw = ones.shape[0]
    cols = []
    for j in range(val.shape[1] // gw):
        cols.append(_mm(_pieces(val[:, j * gw:(j + 1) * gw], n), [ones], _NN))
    return jnp.concatenate(cols, axis=1)


def _ada_kernel(c_ref, w_ref, b_ref, o_ref):
    c = c_ref[...]
    s = c * jax.nn.sigmoid(c)
    o_ref[0] = _dot(_bf(s), _bf(w_ref[0])) + b_ref[0]


def _ada(c_all, ada_w, ada_b):
    depth, d, n = ada_w.shape
    m = c_all.shape[0]
    tn = 512
    return pl.pallas_call(
        _ada_kernel,
        grid=(depth, n // tn),
        in_specs=[
            pl.BlockSpec((m, d), lambda i, j: (0, 0)),
            pl.BlockSpec((1, d, tn), lambda i, j: (i, 0, j)),
            pl.BlockSpec((1, 1, tn), lambda i, j: (i, 0, j)),
        ],
        out_specs=pl.BlockSpec((1, m, tn), lambda i, j: (i, 0, j)),
        out_shape=jax.ShapeDtypeStruct((depth, m, n), F32),
        compiler_params=pltpu.CompilerParams(dimension_semantics=("parallel", "parallel")),
        name="ada",
    )(c_all, ada_w, ada_b.reshape(depth, 1, n))


def _rwkv_proj_kernel(per_row, tiles_per_seq,
                      x_ref, xp_ref, mod_ref, ng_ref, mix_ref, wr_ref, wk_ref, wv_ref,
                      w1_ref, w2_ref, w0_ref, a1_ref, a2_ref, a0_ref, g1_ref, g2_ref,
                      kk_ref, ka_ref, ones_ref,
                      r_o, k_o, v_o, lw_o, kk_o, a_o, g_o, h_o):
    d = x_ref.shape[1]
    tm = x_ref.shape[0]
    ng = ng_ref[...]
    shift = _mod(mod_ref, per_row, 0, d)
    scale = _mod(mod_ref, per_row, 1, d)
    h = _prenorm(x_ref[...], ng, shift, scale)
    if per_row:
        prev = xp_ref[...]
        h_o[...] = h
    else:
        hp = _prenorm(xp_ref[...], ng, shift, scale)
        first = (pl.program_id(0) % tiles_per_seq) == 0
        prow = jnp.where(first, 0.0, hp[7:8, :])
        rowid = lax.broadcasted_iota(jnp.int32, h.shape, 0)
        prev = jnp.where(rowid == 0, prow, pltpu.roll(h, 1, axis=0))
        h_o[0] = h[tm - 8:tm, :]
    xx = prev - h

    def mixed(j):
        return _bf(h + xx * mix_ref[j:j + 1, :])

    r = _dot(mixed(0), wr_ref[...])
    wl = w0_ref[...] + _dot(_bf(jnp.tanh(_dot(mixed(1), w1_ref[...]))), w2_ref[...])
    k = _dot(mixed(2), wk_ref[...])
    v = _dot(mixed(3), wv_ref[...])
    a = jax.nn.sigmoid(a0_ref[...] + _dot(_bf(_dot(mixed(4), a1_ref[...])), a2_ref[...]))
    g = _dot(_bf(jax.nn.sigmoid(_dot(mixed(5), g1_ref[...]))), g2_ref[...])
    kk = k * kk_ref[...]
    ss = _head_sum(kk * kk, ones_ref[...], 1)
    kk = kk / jnp.maximum(jnp.sqrt(ss), 1e-12)
    r_o[...] = r
    k_o[...] = k * (1.0 + (a - 1.0) * ka_ref[...])
    v_o[...] = v
    lw_o[...] = jax.nn.sigmoid(wl) * (-math.exp(-0.5))
    kk_o[...] = kk
    a_o[...] = a
    g_o[...] = g


def _const_spec(shape):
    nd = len(shape)
    return pl.BlockSpec(shape, lambda *_: (0,) * nd, pipeline_mode=pl.Buffered(1))


def _rwkv_proj(x, prev_rows, mod, ng, p, tm, seq_len):
    t, d = x.shape
    per_row = prev_rows is not None
    nt = t // tm
    tiles_per_seq = 1 if per_row else seq_len // tm
    row = lambda i: (i, 0)
    if per_row:
        xp, xp_spec = prev_rows, pl.BlockSpec((tm, d), row)
        mod_spec = pl.BlockSpec((tm, N_ADA * d), row)
        h_shape, h_spec = jax.ShapeDtypeStruct((t, d), F32), pl.BlockSpec((tm, d), row)
    else:
        xp, xp_spec = x, pl.BlockSpec((8, d), lambda i: (jnp.maximum(i * (tm // 8) - 1, 0), 0))
        mod_spec = pl.BlockSpec((1, N_ADA, d), lambda i: (i // tiles_per_seq, 0, 0))
        h_shape, h_spec = jax.ShapeDtypeStruct((nt, 8, d), F32), pl.BlockSpec((1, 8, d), lambda i: (i, 0, 0))
    consts = [ng, p["mix"], p["wr"], p["wk"], p["wv"], p["w1"], p["w2"], p["w0"], p["a1"], p["a2"], p["a0"],
              p["g1"], p["g2"], p["kk"], p["ka"], p["ones"]]
    act = jax.ShapeDtypeStruct((t, d), F32)
    return pl.pallas_call(
        functools.partial(_rwkv_proj_kernel, per_row, tiles_per_seq),
        grid=(nt,),
        in_specs=[pl.BlockSpec((tm, d), row), xp_spec, mod_spec] + [_const_spec(c.shape) for c in consts],
        out_specs=[pl.BlockSpec((tm, d), row)] * 7 + [h_spec],
        out_shape=[act] * 7 + [h_shape],
        compiler_params=pltpu.CompilerParams(dimension_semantics=("parallel",), vmem_limit_bytes=VMEM_LIMIT),
        name="rwkv_proj_step" if per_row else "rwkv_proj_seq",
    )(x, xp, mod, *consts)


def _head_of_lane(shape, axis):
    return lax.broadcasted_iota(jnp.int32, shape, axis) // HEAD_DIM


def _bd(p, diag_mask):
    return jnp.where(diag_mask, jnp.concatenate([p] * HPG, axis=0), jnp.zeros((), p.dtype))


def _wkv_prep_chunks(rs, ks, vs, lws, kks, as_, masks):
    tri, diag, strict, incl, eye, blk8, levels, lane_head = masks
    n = len(rs)

    def each(f, *args):
        return [f(*[a[i] for a in args]) for i in range(n)]

    def pp(xs, ys):
        rhs = each(lambda y: _bd(_bf(y), diag), ys)
        return each(lambda x, b: _dot(_bf(x), b), xs, rhs)

    def pabt(xs, ys):
        rhs = each(lambda y: _bd(_bf(y), diag), ys)
        return each(lambda x, b: _dot_nt(_bf(x), b), xs, rhs)

    def pick_diag(gram):
        out = gram[0:HEAD_DIM, :]
        for hh in range(1, HPG):
            out = jnp.where(lane_head == hh, gram[hh * HEAD_DIM:(hh + 1) * HEAD_DIM, :], out)
        return out

    def patb(xs, ys):
        lhs = each(lambda x: _bf(x.T), xs)
        return each(lambda xt, y: pick_diag(_dot(xt, _bf(y))), lhs, ys)

    def add(xs, ys):
        return each(lambda x, y: x + y, xs, ys)

    def masked(m, xs):
        return each(lambda x: jnp.where(m, x, 0.0), xs)

    c = each(lambda lw: _mm([tri], _pieces(lw, 3), _NN), lws)
    c_last = each(lambda ci: ci[CHUNK - 1:CHUNK, :], c)
    e_neg = each(lambda ci: jnp.exp(-ci), c)
    e_end = each(lambda ci, cl: jnp.exp(cl - ci), c, c_last)
    bv = each(lambda kk, a: kk * a, kks, as_)
    a_t = each(lambda kk, ci, lw: -kk * jnp.exp(ci - lw), kks, c, lws)
    r_t = each(lambda r, ci: r * jnp.exp(ci), rs, c)
    b_t = each(lambda b, e: b * e, bv, e_neg)
    k_t = each(lambda k, e: k * e, ks, e_neg)
    b_e = each(lambda b, e: b * e, bv, e_end)
    k_e = each(lambda k, e: k * e, ks, e_end)

    ar = each(lambda x, y: jnp.concatenate([x, y], axis=0), a_t, r_t)
    xb = pabt(ar, b_t)
    xk = pabt(ar, k_t)
    a_ab = each(lambda x: jnp.where(strict, x[:CHUNK], 0.0), xb)
    a_rb = each(lambda x: jnp.where(incl, x[CHUNK:], 0.0), xb)
    a_ak = each(lambda x: jnp.where(strict, x[:CHUNK], 0.0), xk)
    a_rk = each(lambda x: jnp.where(incl, x[CHUNK:], 0.0), xk)

    n0 = masked(blk8, a_ab)
    n2 = pp(n0, n0)
    n4 = pp(n2, n2)
    plus_eye = lambda xs: each(lambda x: eye + x, xs)
    tinv = pp(pp(plus_eye(n0), plus_eye(n2)), plus_eye(n4))
    for lvl in levels:
        tinv = add(tinv, pp(pp(tinv, masked(lvl, a_ab)), tinv))

    w1 = pp(tinv, a_t)
    w2 = pp(tinv, pp(a_ak, vs))
    qc = add(r_t, pp(a_rb, w1))
    y0 = add(pp(a_rb, w2), pp(a_rk, vs))
    mc = patb(w1, b_e)
    cc = add(patb(w2, b_e), patb(vs, k_e))
    return qc, y0, mc, cc, c_last


def _wkv_masks():
    t = lax.broadcasted_iota(jnp.int32, (CHUNK, GW), 0)
    s = lax.broadcasted_iota(jnp.int32, (CHUNK, GW), 1) % HEAD_DIM
    tri = (lax.broadcasted_iota(jnp.int32, (CHUNK, CHUNK), 1)
           <= lax.broadcasted_iota(jnp.int32, (CHUNK, CHUNK), 0)).astype(BF16)
    diag = _head_of_lane((GW, GW), 0) == _head_of_lane((GW, GW), 1)
    strict = s < t
    incl = s <= t
    eye = (s == t).astype(F32)
    blk8 = strict & ((s // 8) == (t // 8))
    levels = []
    b = 8
    while b < CHUNK:
        levels.append(((s // (2 * b)) == (t // (2 * b))) & ((s // b) != (t // b)) & strict)
        b *= 2
    lane_head = _head_of_lane((HEAD_DIM, GW), 1)
    return tri, diag, strict, incl, eye, blk8, levels, lane_head


def _wkv_prep_kernel(nsub, r_ref, k_ref, v_ref, lw_ref, kk_ref, a_ref,
                     qc_o, y0_o, mc_o, cc_o, cl_o):
    masks = _wkv_masks()
    sls = [slice(u * CHUNK, (u + 1) * CHUNK) for u in range(nsub)]
    chunks = lambda ref: [ref[sl, :] for sl in sls]
    qc, y0, mc, cc, cl = _wkv_prep_chunks(chunks(r_ref), chunks(k_ref), chunks(v_ref), chunks(lw_ref),
                                          chunks(kk_ref), chunks(a_ref), masks)
    for u, sl in enumerate(sls):
        qc_o[sl, :] = qc[u]
        y0_o[sl, :] = y0[u]
        mc_o[sl, :] = mc[u]
        cc_o[sl, :] = cc[u]
        cl_o[u] = cl[u]


def _wkv_scan_kernel(nsub, ng, qc_ref, y0_ref, mc_ref, cc_ref, cl_ref, y_o, s_o, s_scr):
    step = pl.program_id(1)

    @pl.when(step == 0)
    def _():
        s_scr[...] = jnp.zeros_like(s_scr)

    diag = _head_of_lane((GW, GW), 0) == _head_of_lane((GW, GW), 1)
    states = [s_scr[:, g * GW:(g + 1) * GW] for g in range(ng)]
    gls = [slice(g * GW, (g + 1) * GW) for g in range(ng)]
    for u in range(nsub):
        sl = slice(u * CHUNK, (u + 1) * CHUNK)
        sb = [_bf(s) for s in states]
        upd = [_dot(sb[g], _bd(_bf(mc_ref[sl, gl]), diag)) for g, gl in enumerate(gls)]
        ys = [_dot_nt(_bf(qc_ref[sl, gl]), _bd(sb[g], diag)) for g, gl in enumerate(gls)]
        for g, gl in enumerate(gls):
            states[g] = states[g] * jnp.exp(cl_ref[u, :, gl]) + upd[g] + cc_ref[sl, gl]
            y_o[sl, gl] = ys[g] + y0_ref[sl, gl]
    for g in range(ng):
        s_scr[:, g * GW:(g + 1) * GW] = states[g]

    @pl.when(step == pl.num_programs(1) - 1)
    def _():
        for g in range(ng):
            s_o[0, g] = states[g]


def _wkv_seq(r, k, v, lw, kk, a, batch, seq_len):
    t, d = r.shape
    nsub = 8
    rows = nsub * CHUNK
    ng = d // GW
    nblk = t // rows
    blk = pl.BlockSpec((rows, GW), lambda i, j: (i, j))
    act = jax.ShapeDtypeStruct((t, d), F32)
    cl_spec = pl.BlockSpec((nsub, 1, GW), lambda i, j: (i, 0, j))
    qc, y0, mc, cc, cl = pl.pallas_call(
        functools.partial(_wkv_prep_kernel, nsub),
        grid=(nblk, ng),
        in_specs=[blk] * 6,
        out_specs=[blk] * 4 + [cl_spec],
        out_shape=[act] * 4 + [jax.ShapeDtypeStruct((t // CHUNK, 1, d), F32)],
        compiler_params=pltpu.CompilerParams(dimension_semantics=("parallel", "parallel"),
                                             vmem_limit_bytes=VMEM_LIMIT),
        name="wkv_prep",
    )(r, k, v, lw, kk, a)

    steps = seq_len // rows
    sblk = pl.BlockSpec((rows, d), lambda i, j: (i * steps + j, 0))
    scl = pl.BlockSpec((nsub, 1, d), lambda i, j: (i * steps + j, 0, 0))
    y, state = pl.pallas_call(
        functools.partial(_wkv_scan_kernel, nsub, ng),
        grid=(batch, steps),
        in_specs=[sblk] * 4 + [scl],
        out_specs=[sblk, pl.BlockSpec((1, ng, HEAD_DIM, GW), lambda i, j: (i, 0, 0, 0))],
        out_shape=[act, jax.ShapeDtypeStruct((batch, ng, HEAD_DIM, GW), F32)],
        scratch_shapes=[pltpu.VMEM((HEAD_DIM, d), F32)],
        compiler_params=pltpu.CompilerParams(dimension_semantics=("parallel", "arbitrary"),
                                             vmem_limit_bytes=VMEM_LIMIT),
        name="wkv_scan",
    )(qc, y0, mc, cc, cl)
    return y, state


def _wkv_step_kernel(s_ref, r_ref, w_ref, k_ref, v_ref, a_ref, b_ref, so_ref, y_ref):
    n = HEAD_DIM
    eye = (lax.broadcasted_iota(jnp.int32, (n, n), 0) == lax.broadcasted_iota(jnp.int32, (n, n), 1)).astype(F32)
    s = s_ref[...]
    sa = jnp.sum(s * a_ref[...], axis=-1, keepdims=True)
    vcol = jnp.sum(eye * v_ref[...], axis=-1, keepdims=True)
    s = s * w_ref[...] + sa * b_ref[...] + vcol * k_ref[...]
    so_ref[...] = s
    ycol = jnp.sum(s * r_ref[...], axis=-1, keepdims=True)
    y_ref[...] = jnp.sum(eye * ycol, axis=-2, keepdims=True)


def _wkv_step(state, r, decay, k, v, a, b):
    n = state.shape[0]
    nb = 32
    sspec = pl.BlockSpec((nb, HEAD_DIM, HEAD_DIM), lambda i: (i, 0, 0))
    vspec = pl.BlockSpec((nb, 1, HEAD_DIM), lambda i: (i, 0, 0))
    return pl.pallas_call(
        _wkv_step_kernel,
        grid=(n // nb,),
        in_specs=[sspec] + [vspec] * 6,
        out_specs=[sspec, vspec],
        out_shape=[jax.ShapeDtypeStruct(state.shape, F32), jax.ShapeDtypeStruct((n, 1, HEAD_DIM), F32)],
        compiler_params=pltpu.CompilerParams(dimension_semantics=("parallel",)),
        name="wkv_step",
    )(state, r, decay, k, v, a, b)


def _post_kernel(rwkv, final, per_row, *refs):
    refs = list(refs)
    x_ref = refs.pop(0)
    z_ref = refs.pop(0)
    if rwkv:
        r_ref, k_ref, v_ref, g_ref = refs[:4]
        refs = refs[4:]
    mod_ref = refs.pop(0)
    if rwkv:
        lg_ref, lb_ref, rk_ref, ones_ref = refs[:4]
        refs = refs[4:]
    wo_ref, n2_ref, up_ref, down_ref = refs[:4]
    refs = refs[4:]
    if final:
        fg_ref = refs.pop(0)
    x_o = refs.pop(0)
    d = x_ref.shape[1]

    z = z_ref[...]
    if rwkv:
        ones = ones_ref[...]
        inv_n = 1.0 / HEAD_DIM

        mu = _head_sum(z, ones, 2) * inv_n
        dz = z - mu
        var = _head_sum(dz * dz, ones, 1) * inv_n
        yn = dz * lax.rsqrt(var + LNX_EPS) * lg_ref[...] + lb_ref[...]
        v = v_ref[...]
        bonus = _head_sum(r_ref[...] * k_ref[...] * rk_ref[...], ones, 2) * v
        z = (yn + bonus) * g_ref[...]
    out = _dot(_bf(z), wo_ref[...])
    x1 = x_ref[...] + _mod(mod_ref, per_row, 2, d) * out
    h2 = _bf(_prenorm(x1, n2_ref[...], _mod(mod_ref, per_row, 3, d), _mod(mod_ref, per_row, 4, d)))
    dff = up_ref.shape[1]
    fc = 1024
    acc = None
    for c in range(dff // fc):
        hid = jnp.maximum(_dot(h2, up_ref[:, c * fc:(c + 1) * fc]), 0.0)
        part = _dot(_bf(hid * hid), down_ref[c * fc:(c + 1) * fc, :])
        acc = part if acc is None else acc + part
    x2 = x1 + _mod(mod_ref, per_row, 5, d) * acc
    x_o[...] = x2
    if final:
        refs[0][...] = _rms(x2, fg_ref[...])


def _post(x, z, extra, mod, per_row, rw, wo, n2g, up, down, final_g, tm, seq_len):
    t, d = x.shape
    rwkv = rw is not None
    final = final_g is not None
    row = lambda i: (i, 0)
    tile = pl.BlockSpec((tm, d), row)
    tiles_per_seq = 1 if per_row else seq_len // tm
    mod_spec = (pl.BlockSpec((tm, N_ADA * d), row) if per_row
                else pl.BlockSpec((1, N_ADA, d), lambda i: (i // tiles_per_seq, 0, 0)))
    args = [x, z]
    specs = [tile, tile]
    if rwkv:
        args += list(extra)
        specs += [tile] * 4
    args.append(mod)
    specs.append(mod_spec)
    consts = (list(rw) if rwkv else []) + [wo, n2g, up, down] + ([final_g] if final else [])
    args += consts
    specs += [_const_spec(c.shape) for c in consts]
    act = jax.ShapeDtypeStruct((t, d), F32)
    outs = pl.pallas_call(
        functools.partial(_post_kernel, rwkv, final, per_row),
        grid=(t // tm,),
        in_specs=specs,
        out_specs=[tile] * (2 if final else 1),
        out_shape=[act] * (2 if final else 1),
        compiler_params=pltpu.CompilerParams(dimension_semantics=("parallel",), vmem_limit_bytes=VMEM_LIMIT),
        name=("post_rwkv" if rwkv else "post_attn") + ("_step" if per_row else "_seq"),
    )(*args)
    return outs


def _rope_table_kernel(base, step, invf_ref, c_o, sm_o, sp_o):
    shape = c_o.shape
    row = lax.broadcasted_iota(jnp.int32, shape, 0) + pl.program_id(0) * shape[0]
    pos = (base + step * row).astype(F32)
    ang = pos * invf_ref[...]
    lane = lax.broadcasted_iota(jnp.int32, shape, 1) % HEAD_DIM
    cos = jnp.cos(ang)
    sin = jnp.sin(ang)
    half = ROPE_DIM // 2
    c_o[...] = jnp.where(lane < ROPE_DIM, cos, 1.0)
    sm_o[...] = jnp.where(lane < half, -sin, 0.0)
    sp_o[...] = jnp.where((lane >= half) & (lane < ROPE_DIM), sin, 0.0)


def _rope_tables(n, base, step):
    half = ROPE_DIM // 2
    inv_freq = ROPE_THETA ** (-jnp.arange(half, dtype=F32) / half)
    lane = jnp.arange(LANE) % HEAD_DIM
    invf = jnp.where(lane < ROPE_DIM, inv_freq[lane % half], 0.0).reshape(1, LANE).astype(F32)
    tr = min(n, 512)
    spec = pl.BlockSpec((tr, LANE), lambda i: (i, 0))
    shp = jax.ShapeDtypeStruct((n, LANE), F32)
    return pl.pallas_call(
        functools.partial(_rope_table_kernel, base, step),
        grid=(n // tr,),
        in_specs=[pl.BlockSpec((1, LANE), lambda i: (0, 0))],
        out_specs=[spec] * 3,
        out_shape=[shp] * 3,
        compiler_params=pltpu.CompilerParams(dimension_semantics=("parallel",)),
        name="rope_tables",
    )(invf)


def _qkv_kernel(per_row, x_ref, mod_ref, ng_ref, w_ref, b_ref, c_ref, sm_ref, sp_ref, q_o, k_o, v_o):
    d = x_ref.shape[1]
    h = _prenorm(x_ref[...], ng_ref[...], _mod(mod_ref, per_row, 0, d), _mod(mod_ref, per_row, 1, d))
    qkv = _dot(_bf(h), w_ref[...]) + b_ref[...]
    c, sm, sp = c_ref[...], sm_ref[...], sp_ref[...]
    nq, nk = q_o.shape[1], k_o.shape[1]

    def rope_into(o_ref, off, width):
        for j in range(width // LANE):
            zc = qkv[:, off + j * LANE:off + (j + 1) * LANE]
            o_ref[:, j * LANE:(j + 1) * LANE] = (zc * c + pltpu.roll(zc, LANE - ROPE_DIM // 2, axis=1) * sm
                                                 + pltpu.roll(zc, ROPE_DIM // 2, axis=1) * sp)

    rope_into(q_o, 0, nq)
    rope_into(k_o, nq, nk)
    v_o[...] = qkv[:, nq + nk:]


def _qkv(x, mod, per_row, ng, w, b, tables, tm, seq_len):
    t, d = x.shape
    nkv = KV_HEADS * HEAD_DIM
    row = lambda i: (i, 0)
    tiles_per_seq = 1 if per_row else seq_len // tm
    mod_spec = (pl.BlockSpec((tm, N_ADA * d), row) if per_row
                else pl.BlockSpec((1, N_ADA, d), lambda i: (i // tiles_per_seq, 0, 0)))
    tab_spec = pl.BlockSpec((tm, LANE), lambda i: (i % tiles_per_seq, 0))
    return pl.pallas_call(
        functools.partial(_qkv_kernel, per_row),
        grid=(t // tm,),
        in_specs=[pl.BlockSpec((tm, d), row), mod_spec, _const_spec(ng.shape), _const_spec(w.shape),
                  _const_spec(b.shape)] + [tab_spec] * 3,
        out_specs=[pl.BlockSpec((tm, d), row), pl.BlockSpec((tm, nkv), row), pl.BlockSpec((tm, nkv), row)],
        out_shape=[jax.ShapeDtypeStruct((t, d), F32), jax.ShapeDtypeStruct((t, nkv), F32),
                   jax.ShapeDtypeStruct((t, nkv), F32)],
        compiler_params=pltpu.CompilerParams(dimension_semantics=("parallel",), vmem_limit_bytes=VMEM_LIMIT),
        name="qkv_step" if per_row else "qkv_seq",
    )(x, mod, ng, w, b, *tables)


def _sink_softmax(s, sink):
    m = jnp.maximum(jnp.max(s, axis=-1, keepdims=True), sink)
    p = jnp.exp(s - m)
    return p / (jnp.sum(p, axis=-1, keepdims=True) + jnp.exp(sink - m))


def _swa_seq_kernel(q_ref, kc_ref, kp_ref, vc_ref, vp_ref, sink_ref, o_ref):
    n = pl.program_id(1)
    q = q_ref[...]
    kall = jnp.concatenate([kp_ref[...], kc_ref[...]], axis=0)
    vall = jnp.concatenate([vp_ref[...], vc_ref[...]], axis=0)
    blk = ATT_BLOCK
    i = lax.broadcasted_iota(jnp.int32, (blk, 2 * blk), 0)
    j = lax.broadcasted_iota(jnp.int32, (blk, 2 * blk), 1)
    mask = (j >= i + (blk - WINDOW)) & (j <= i + blk) & ((n > 0) | (j >= blk))
    heads = range(KV_HEADS * GROUP)
    hsl = [slice(h * HEAD_DIM, (h + 1) * HEAD_DIM) for h in heads]
    kh = [_bf(kall[:, hsl[kvh]]) for kvh in range(KV_HEADS)]
    vh = [_bf(vall[:, hsl[kvh]]) for kvh in range(KV_HEADS)]
    qh = [_bf(q[:, hsl[h]]) for h in heads]
    s = [_dot_nt(qh[h], kh[h // GROUP]) * (HEAD_DIM ** -0.5) for h in heads]
    s = [jnp.where(mask, s[h], -jnp.inf) for h in heads]
    sink = [sink_ref[0:1, h:h + 1] for h in heads]
    m = [jnp.maximum(jnp.max(s[h], axis=-1, keepdims=True), sink[h]) for h in heads]
    p = [jnp.exp(s[h] - m[h]) for h in heads]
    inv = [1.0 / (jnp.sum(p[h], axis=-1, keepdims=True) + jnp.exp(sink[h] - m[h])) for h in heads]
    o = [_dot(_bf(p[h] * inv[h]), vh[h // GROUP]) for h in heads]
    for h in heads:
        o_ref[:, hsl[h]] = o[h]


def _swa_seq(q, k, v, sink, batch, seq_len):
    t, d = q.shape
    nkv = k.shape[1]
    nb = seq_len // ATT_BLOCK
    cur = lambda b, n: (b * nb + n, 0)
    prv = lambda b, n: (b * nb + jnp.maximum(n - 1, 0), 0)
    return pl.pallas_call(
        _swa_seq_kernel,
        grid=(batch, nb),
        in_specs=[pl.BlockSpec((ATT_BLOCK, d), cur),
                  pl.BlockSpec((ATT_BLOCK, nkv), cur), pl.BlockSpec((ATT_BLOCK, nkv), prv),
                  pl.BlockSpec((ATT_BLOCK, nkv), cur), pl.BlockSpec((ATT_BLOCK, nkv), prv),
                  pl.BlockSpec(sink.shape, lambda b, n: (0, 0))],
        out_specs=pl.BlockSpec((ATT_BLOCK, d), cur),
        out_shape=jax.ShapeDtypeStruct((t, d), F32),
        compiler_params=pltpu.CompilerParams(dimension_semantics=("parallel", "parallel")),
        name="swa_seq",
    )(q, k, k, v, v, sink)


def _swa_step_kernel(q_ref, kn_ref, vn_ref, kc_ref, vc_ref, sink_ref, o_ref, ko_ref, vo_ref):
    w = kc_ref.shape[1]
    kc = kc_ref[...]
    vc = vc_ref[...]
    kn = kn_ref[...]
    vn = vn_ref[...]
    q = q_ref[...]
    scale = HEAD_DIM ** -0.5
    for kvh in range(KV_HEADS):
        ksl = slice(kvh * HEAD_DIM, (kvh + 1) * HEAD_DIM)
        kh, vh = kc[:, :, ksl], vc[:, :, ksl]
        knh, vnh = kn[:, :, ksl], vn[:, :, ksl]
        qh = q[:, kvh * GROUP:(kvh + 1) * GROUP, :]
        s_old = jnp.einsum("bgd,bkd->bgk", _bf(qh), _bf(kh), preferred_element_type=F32) * scale
        s_new = jnp.sum(_bf(qh).astype(F32) * _bf(knh).astype(F32), axis=-1, keepdims=True) * scale
        sink = sink_ref[kvh]
        m = jnp.maximum(jnp.maximum(jnp.max(s_old, axis=-1, keepdims=True), s_new), sink)
        p_old = jnp.exp(s_old - m)
        p_new = jnp.exp(s_new - m)
        den = jnp.sum(p_old, axis=-1, keepdims=True) + p_new + jnp.exp(sink - m)
        o = jnp.einsum("bgk,bkd->bgd", _bf(p_old / den), _bf(vh), preferred_element_type=F32)
        o = o + _bf(p_new / den).astype(F32) * _bf(vnh).astype(F32)
        o_ref[:, kvh * GROUP:(kvh + 1) * GROUP, :] = o
    rowid = lax.broadcasted_iota(jnp.int32, kc.shape, 1)
    ko_ref[...] = jnp.where(rowid == w - 1, kn, pltpu.roll(kc, w - 1, axis=1))
    vo_ref[...] = jnp.where(rowid == w - 1, vn, pltpu.roll(vc, w - 1, axis=1))


def _swa_step(q, kn, vn, cache_k, cache_v, sink):
    b, nh, hd = q.shape
    w, nkv = cache_k.shape[1], cache_k.shape[2]
    nb = 8
    i3 = lambda i: (i, 0, 0)
    cspec = pl.BlockSpec((nb, w, nkv), i3)
    nspec = pl.BlockSpec((nb, 1, nkv), i3)
    qspec = pl.BlockSpec((nb, nh, hd), i3)
    return pl.pallas_call(
        _swa_step_kernel,
        grid=(b // nb,),
        in_specs=[qspec, nspec, nspec, cspec, cspec, pl.BlockSpec(sink.shape, lambda i: (0, 0, 0))],
        out_specs=[qspec, cspec, cspec],
        out_shape=[jax.ShapeDtypeStruct(q.shape, F32), jax.ShapeDtypeStruct(cache_k.shape, F32),
                   jax.ShapeDtypeStruct(cache_v.shape, F32)],
        compiler_params=pltpu.CompilerParams(dimension_semantics=("parallel",)),
        name="swa_step",
    )(q, kn, vn, cache_k, cache_v, sink)


def kernel(x_prompt, x_sample, c_prompt, c_sample, state_wkv, state_shift, cache_k, cache_v, norm1_g, norm2_g, ada_w, ada_b, mlp_up, mlp_down, final_g, rw_mix, rw_wr, rw_wk, rw_wv, rw_wo, rw_w0, rw_w1, rw_w2, rw_a0, rw_a1, rw_a2, rw_g1, rw_g2, rw_kk, rw_ka, rw_rk, rw_lnx_g, rw_lnx_b, at_wqkv, at_bqkv, at_wo, at_sink):
    bp, seq, d = x_prompt.shape
    bs = x_sample.shape[0]
    assert x_sample.shape[1] == 1 and d % GW == 0 and seq % (8 * CHUNK) == 0
    nh = d // HEAD_DIM
    tp = bp * seq
    row = lambda vec: vec.reshape(1, -1)

    n_c = bp + bs
    pad = (-n_c) % 8
    c_all = jnp.concatenate([c_prompt, c_sample, jnp.zeros((pad, d), F32)], axis=0)
    ada = _ada(c_all, ada_w, ada_b)
    mod_p = [ada[i, :bp].reshape(bp, N_ADA, d) for i in range(2)]
    mod_s = [ada[i, bp:bp + bs] for i in range(2)]

    xp = x_prompt.reshape(tp, d)
    xs = x_sample.reshape(bs, d)

    head_id = jnp.arange(GW) // HEAD_DIM
    ones_bd = (head_id[:, None] == head_id[None, :]).astype(BF16)

    rwp = dict(mix=rw_mix[0], wr=_bf(rw_wr[0]), wk=_bf(rw_wk[0]), wv=_bf(rw_wv[0]),
               w1=_bf(rw_w1[0]), w2=_bf(rw_w2[0]), w0=row(rw_w0[0]),
               a1=_bf(rw_a1[0]), a2=_bf(rw_a2[0]), a0=row(rw_a0[0]),
               g1=_bf(rw_g1[0]), g2=_bf(rw_g2[0]), kk=row(rw_kk[0]), ka=row(rw_ka[0]), ones=ones_bd)
    n1 = row(norm1_g[0])
    post_rw = (row(rw_lnx_g[0]), row(rw_lnx_b[0]), row(rw_rk[0]), ones_bd)
    wo0, up0, down0 = _bf(rw_wo[0]), _bf(mlp_up[0]), _bf(mlp_down[0])
    n2 = row(norm2_g[0])

    r, k, v, lw, kk, a, g, hlast = _rwkv_proj(xp, None, mod_p[0], n1, rwp, 256, seq)
    tiles = seq // 256
    shift_p = hlast.reshape(bp, tiles, 8, d)[:, -1, -1]
    y, st = _wkv_seq(r, k, v, lw, kk, a, bp, seq)
    wkv_p = st.reshape(bp, d // GW, HEAD_DIM, HPG, HEAD_DIM).transpose(0, 1, 3, 2, 4).reshape(bp, nh, HEAD_DIM, HEAD_DIM)
    (xp,) = _post(xp, y, (r, k, v, g), mod_p[0], False, post_rw, wo0, n2, up0, down0, None, 256, seq)

    rs, ks, vs, lws, kks, as_, gs, shift_s = _rwkv_proj(xs, state_shift[0], mod_s[0], n1, rwp, bs, 1)
    vec = lambda z: z.reshape(bs * nh, 1, HEAD_DIM)
    st_s, ys = _wkv_step(state_wkv[0].reshape(bs * nh, HEAD_DIM, HEAD_DIM), vec(rs), vec(jnp.exp(lws)),
                         vec(ks), vec(vs), vec(-kks), vec(kks * as_))
    wkv_s = st_s.reshape(bs, nh, HEAD_DIM, HEAD_DIM)
    (xs,) = _post(xs, ys.reshape(bs, d), (rs, ks, vs, gs), mod_s[0], True, post_rw, wo0, n2, up0, down0, None, bs, 1)

    wqkv, bqkv, wo1 = _bf(at_wqkv[0]), row(at_bqkv[0]), _bf(at_wo[0])
    n1, n2 = row(norm1_g[1]), row(norm2_g[1])
    up1, down1 = _bf(mlp_up[1]), _bf(mlp_down[1])
    fg = row(final_g)
    nkv = KV_HEADS * HEAD_DIM

    q, kq, vq = _qkv(xp, mod_p[1], False, n1, wqkv, bqkv, _rope_tables(seq, 0, 1), 256, seq)
    o = _swa_seq(q, kq, vq, row(at_sink[0]), bp, seq)
    keep = min(WINDOW, seq)
    k_p = kq.reshape(bp, seq, nkv)[:, seq - keep:].reshape(bp, keep, KV_HEADS, HEAD_DIM)
    v_p = vq.reshape(bp, seq, nkv)[:, seq - keep:].reshape(bp, keep, KV_HEADS, HEAD_DIM)
    xp, y_p = _post(xp, o, None, mod_p[1], False, None, wo1, n2, up1, down1, fg, 256, seq)

    qs, kn, vn = _qkv(xs, mod_s[1], True, n1, wqkv, bqkv, _rope_tables(bs, PAST_LEN, 0), bs, 1)
    w_buf = cache_k.shape[2]
    os_, k_s, v_s = _swa_step(qs.reshape(bs, nh, HEAD_DIM), kn.reshape(bs, 1, nkv), vn.reshape(bs, 1, nkv),
                              cache_k[0].reshape(bs, w_buf, nkv), cache_v[0].reshape(bs, w_buf, nkv),
                              at_sink[0].reshape(KV_HEADS, GROUP, 1))
    xs, y_s = _post(xs, os_.reshape(bs, d), None, mod_s[1], True, None, wo1, n2, up1, down1, fg, bs, 1)

    cshape = (1, bs, w_buf, KV_HEADS, HEAD_DIM)
    return (y_p.reshape(bp, seq, d), y_s.reshape(bs, 1, d),
            wkv_p[None], wkv_s[None], shift_p[None], shift_s[None],
            k_p[None], k_s.reshape(cshape), v_p[None], v_s.reshape(cshape))
```

```python
import functools
import math

import jax
import jax.numpy as jnp
from jax import lax
from jax.experimental import pallas as pl
from jax.experimental.pallas import tpu as pltpu

F32 = jnp.float32
BF16 = jnp.bfloat16

HEAD_DIM = 64
KV_HEADS = 4
GROUP = 4
WINDOW = 128
ATT_BLOCK = 128
ROPE_DIM = HEAD_DIM // 4
ROPE_THETA = 500000.0
PAST_LEN = 8192
NORM_EPS = 1e-6
LNX_EPS = 64e-5
N_ADA = 6

LANE = 128
V7X_VMEM_BYTES = 64 * 1024 * 1024
VMEM_LIMIT = V7X_VMEM_BYTES * 7 // 8

CHUNK = 64
HPG = 4
GW = HPG * HEAD_DIM


def _bf(x):
    return x.astype(BF16)


def _dot(a, b):
    return jnp.dot(a, b, preferred_element_type=F32)


def _dot_nt(a, b):
    return lax.dot_general(a, b, (((1,), (1,)), ((), ())), preferred_element_type=F32)


def _rms(x, g):
    ms = jnp.mean(x * x, axis=-1, keepdims=True)
    return x * lax.rsqrt(ms + NORM_EPS) * g


def _prenorm(x, g, shift, scale):
    return _rms(x, g) * (1.0 + scale) + shift


def _mod(mod_ref, per_row, idx, d):
    if per_row:
        return mod_ref[:, idx * d:(idx + 1) * d]
    return mod_ref[0, idx:idx + 1, :]


def _pieces(x, n):
    out = []
    rem = x
    for i in range(n):
        p = rem.astype(BF16)
        out.append(p)
        if i + 1 < n:
            rem = rem - p.astype(F32)
    return out


def _mm(xp, yp, dn):
    n = max(len(xp), len(yp))
    acc = None
    for i, xi in enumerate(xp):
        for j, yj in enumerate(yp):
            if i + j < n:
                t = lax.dot_general(xi, yj, dn, preferred_element_type=F32)
                acc = t if acc is None else acc + t
    return acc


_NN = (((1,), (0,)), ((), ()))
_NT = (((1,), (1,)), ((), ()))


def _head_sum(val, ones, n):
    gw = ones.shape[0]
    cols = []
    for j in range(val.shape[1] // gw):
        cols.append(_mm(_pieces(val[:, j * gw:(j + 1) * gw], n), [ones], _NN))
    return jnp.concatenate(cols, axis=1)


def _ada_kernel(c_ref, w_ref, b_ref, o_ref):
    c = c_ref[...]
    s = c * jax.nn.sigmoid(c)
    o_ref[0] = _dot(_bf(s), _bf(w_ref[0])) + b_ref[0]


def _ada(c_all, ada_w, ada_b):
    depth, d, n = ada_w.shape
    m = c_all.shape[0]
    tn = 512
    return pl.pallas_call(
        _ada_kernel,
        grid=(depth, n // tn),
        in_specs=[
            pl.BlockSpec((m, d), lambda i, j: (0, 0)),
            pl.BlockSpec((1, d, tn), lambda i, j: (i, 0, j)),
            pl.BlockSpec((1, 1, tn), lambda i, j: (i, 0, j)),
        ],
        out_specs=pl.BlockSpec((1, m, tn), lambda i, j: (i, 0, j)),
        out_shape=jax.ShapeDtypeStruct((depth, m, n), F32),
        compiler_params=pltpu.CompilerParams(dimension_semantics=("parallel", "parallel")),
        name="ada",
    )(c_all, ada_w, ada_b.reshape(depth, 1, n))


def _rwkv_proj_kernel(per_row, tiles_per_seq,
                      x_ref, xp_ref, mod_ref, ng_ref, mix_ref, wr_ref, wk_ref, wv_ref,
                      w1_ref, w2_ref, w0_ref, a1_ref, a2_ref, a0_ref, g1_ref, g2_ref,
                      kk_ref, ka_ref, ones_ref,
                      r_o, k_o, v_o, lw_o, kk_o, a_o, g_o, h_o):
    d = x_ref.shape[1]
    tm = x_ref.shape[0]
    ng = ng_ref[...]
    shift = _mod(mod_ref, per_row, 0, d)
    scale = _mod(mod_ref, per_row, 1, d)
    h = _prenorm(x_ref[...], ng, shift, scale)
    if per_row:
        prev = xp_ref[...]
        h_o[...] = h
    else:
        hp = _prenorm(xp_ref[...], ng, shift, scale)
        first = (pl.program_id(0) % tiles_per_seq) == 0
        prow = jnp.where(first, 0.0, hp[7:8, :])
        rowid = lax.broadcasted_iota(jnp.int32, h.shape, 0)
        prev = jnp.where(rowid == 0, prow, pltpu.roll(h, 1, axis=0))
        h_o[0] = h[tm - 8:tm, :]
    xx = prev - h

    def mixed(j):
        return _bf(h + xx * mix_ref[j:j + 1, :])

    r = _dot(mixed(0), wr_ref[...])
    wl = w0_ref[...] + _dot(_bf(jnp.tanh(_dot(mixed(1), w1_ref[...]))), w2_ref[...])
    k = _dot(mixed(2), wk_ref[...])
    v = _dot(mixed(3), wv_ref[...])
    a = jax.nn.sigmoid(a0_ref[...] + _dot(_bf(_dot(mixed(4), a1_ref[...])), a2_ref[...]))
    g = _dot(_bf(jax.nn.sigmoid(_dot(mixed(5), g1_ref[...]))), g2_ref[...])
    kk = k * kk_ref[...]
    ss = _head_sum(kk * kk, ones_ref[...], 1)
    kk = kk / jnp.maximum(jnp.sqrt(ss), 1e-12)
    r_o[...] = r
    k_o[...] = k * (1.0 + (a - 1.0) * ka_ref[...])
    v_o[...] = v
    lw_o[...] = jax.nn.sigmoid(wl) * (-math.exp(-0.5))
    kk_o[...] = kk
    a_o[...] = a
    g_o[...] = g


def _const_spec(shape):
    nd = len(shape)
    return pl.BlockSpec(shape, lambda *_: (0,) * nd, pipeline_mode=pl.Buffered(1))


def _rwkv_proj(x, prev_rows, mod, ng, p, tm, seq_len):
    t, d = x.shape
    per_row = prev_rows is not None
    nt = t // tm
    tiles_per_seq = 1 if per_row else seq_len // tm
    row = lambda i: (i, 0)
    if per_row:
        xp, xp_spec = prev_rows, pl.BlockSpec((tm, d), row)
        mod_spec = pl.BlockSpec((tm, N_ADA * d), row)
        h_shape, h_spec = jax.ShapeDtypeStruct((t, d), F32), pl.BlockSpec((tm, d), row)
    else:
        xp, xp_spec = x, pl.BlockSpec((8, d), lambda i: (jnp.maximum(i * (tm // 8) - 1, 0), 0))
        mod_spec = pl.BlockSpec((1, N_ADA, d), lambda i: (i // tiles_per_seq, 0, 0))
        h_shape, h_spec = jax.ShapeDtypeStruct((nt, 8, d), F32), pl.BlockSpec((1, 8, d), lambda i: (i, 0, 0))
    consts = [ng, p["mix"], p["wr"], p["wk"], p["wv"], p["w1"], p["w2"], p["w0"], p["a1"], p["a2"], p["a0"],
              p["g1"], p["g2"], p["kk"], p["ka"], p["ones"]]
    act = jax.ShapeDtypeStruct((t, d), F32)
    return pl.pallas_call(
        functools.partial(_rwkv_proj_kernel, per_row, tiles_per_seq),
        grid=(nt,),
        in_specs=[pl.BlockSpec((tm, d), row), xp_spec, mod_spec] + [_const_spec(c.shape) for c in consts],
        out_specs=[pl.BlockSpec((tm, d), row)] * 7 + [h_spec],
        out_shape=[act] * 7 + [h_shape],
        compiler_params=pltpu.CompilerParams(dimension_semantics=("parallel",), vmem_limit_bytes=VMEM_LIMIT),
        name="rwkv_proj_step" if per_row else "rwkv_proj_seq",
    )(x, xp, mod, *consts)


def _head_of_lane(shape, axis):
    return lax.broadcasted_iota(jnp.int32, shape, axis) // HEAD_DIM


def _bd(p, low_half):
    zero = jnp.zeros((), p.dtype)
    zeros = jnp.zeros((p.shape[0], LANE), p.dtype)
    per_col = LANE // HEAD_DIM
    ncol = GW // LANE
    blocks = []
    for h in range(HPG):
        c = h // per_col
        col = p[:, c * LANE:(c + 1) * LANE]
        col = jnp.where(low_half, col, zero) if h % per_col == 0 else jnp.where(low_half, zero, col)
        blocks.append(jnp.concatenate([col if j == c else zeros for j in range(ncol)], axis=1))
    return jnp.concatenate(blocks, axis=0)


def _wkv_prep_chunks(rs, ks, vs, lws, kks, as_, masks):
    tri, diag, strict, incl, eye, blk8, levels, lane_head = masks
    n = len(rs)

    def each(f, *args):
        return [f(*[a[i] for a in args]) for i in range(n)]

    def pp(xs, ys):
        rhs = each(lambda y: _bd(_bf(y), diag), ys)
        return each(lambda x, b: _dot(_bf(x), b), xs, rhs)

    def pabt(xs, ys):
        rhs = each(lambda y: _bd(_bf(y), diag), ys)
        return each(lambda x, b: _dot_nt(_bf(x), b), xs, rhs)

    def pick_diag(gram):
        out = gram[0:HEAD_DIM, :]
        for hh in range(1, HPG):
            out = jnp.where(lane_head == hh, gram[hh * HEAD_DIM:(hh + 1) * HEAD_DIM, :], out)
        return out

    def patb(xs, ys):
        lhs = each(lambda x: _bf(x.T), xs)
        return each(lambda xt, y: pick_diag(_dot(xt, _bf(y))), lhs, ys)

    def add(xs, ys):
        return each(lambda x, y: x + y, xs, ys)

    def masked(m, xs):
        return each(lambda x: jnp.where(m, x, 0.0), xs)

    c = each(lambda lw: _mm([tri], _pieces(lw, 2), _NN), lws)
    c_last = each(lambda ci: ci[CHUNK - 1:CHUNK, :], c)
    e_neg = each(lambda ci: jnp.exp(-ci), c)
    e_end = each(lambda ci, cl: jnp.exp(cl - ci), c, c_last)
    bv = each(lambda kk, a: kk * a, kks, as_)
    a_t = each(lambda kk, ci, lw: _bf(-kk * jnp.exp(ci - lw)), kks, c, lws)
    r_t = each(lambda r, ci: r * jnp.exp(ci), rs, c)
    b_t = each(lambda b, e: _bf(b * e), bv, e_neg)
    k_t = each(lambda k, e: _bf(k * e), ks, e_neg)
    b_e = each(lambda b, e: _bf(b * e), bv, e_end)
    k_e = each(lambda k, e: _bf(k * e), ks, e_end)

    ar = each(lambda x, y: jnp.concatenate([x, _bf(y)], axis=0), a_t, r_t)
    xb = pabt(ar, b_t)
    xk = pabt(ar, k_t)
    a_ab = each(lambda x: jnp.where(strict, x[:CHUNK], 0.0), xb)
    a_rb = each(lambda x: jnp.where(incl, x[CHUNK:], 0.0), xb)
    a_ak = each(lambda x: jnp.where(strict, x[:CHUNK], 0.0), xk)
    a_rk = each(lambda x: jnp.where(incl, x[CHUNK:], 0.0), xk)

    n0 = masked(blk8, a_ab)
    n2 = pp(n0, n0)
    n4 = pp(n2, n2)
    plus_eye = lambda xs: each(lambda x: eye + x, xs)
    tinv = pp(pp(plus_eye(n0), plus_eye(n2)), plus_eye(n4))
    b = 8
    for lvl in levels:
        lower = lambda x: jnp.concatenate([x[r0:r0 + b] for r0 in range(b, CHUNK, 2 * b)], axis=0)
        z = pp(pp(each(lower, tinv), masked(lvl, a_ab)), tinv)

        def merged(t, zi):
            rows = []
            for j, r0 in enumerate(range(0, CHUNK, 2 * b)):
                rows += [t[r0:r0 + b], t[r0 + b:r0 + 2 * b] + zi[j * b:(j + 1) * b]]
            return jnp.concatenate(rows, axis=0)

        tinv = each(merged, tinv, z)
        b *= 2

    w1 = pp(tinv, a_t)
    w2 = pp(tinv, pp(a_ak, vs))
    qc = add(r_t, pp(a_rb, w1))
    y0 = add(pp(a_rb, w2), pp(a_rk, vs))
    mc = patb(w1, b_e)
    stack = lambda x, y: jnp.concatenate([x, y], axis=0)
    cc = patb(each(stack, w2, vs), each(stack, b_e, k_e))
    return qc, y0, mc, cc, c_last


def _wkv_masks():
    t = lax.broadcasted_iota(jnp.int32, (CHUNK, GW), 0)
    s = lax.broadcasted_iota(jnp.int32, (CHUNK, GW), 1) % HEAD_DIM
    tri = (lax.broadcasted_iota(jnp.int32, (CHUNK, CHUNK), 1)
           <= lax.broadcasted_iota(jnp.int32, (CHUNK, CHUNK), 0)).astype(BF16)
    diag = lax.broadcasted_iota(jnp.int32, (CHUNK, LANE), 1) < HEAD_DIM
    strict = s < t
    incl = s <= t
    eye = (s == t).astype(F32)
    blk8 = strict & ((s // 8) == (t // 8))
    levels = []
    b = 8
    while b < CHUNK:
        levels.append(((s // (2 * b)) == (t // (2 * b))) & ((s // b) != (t // b)) & strict)
        b *= 2
    lane_head = _head_of_lane((HEAD_DIM, GW), 1)
    return tri, diag, strict, incl, eye, blk8, levels, lane_head


def _rwkv_epilogue(y, r, k, v, g, lg, lb, rk, ones):
    inv_n = 1.0 / HEAD_DIM
    mu = _head_sum(y, ones, 1) * inv_n
    dy = y - mu
    var = _head_sum(dy * dy, ones, 1) * inv_n
    yn = dy * lax.rsqrt(var + LNX_EPS) * lg + lb
    bonus = _head_sum(r * k * rk, ones, 1) * v
    return (yn + bonus) * g


def _wkv_seq_kernel(nsub, ng, r_ref, k_ref, v_ref, lw_ref, kk_ref, a_ref, y_o, s_o, s_scr):
    step = pl.program_id(1)

    @pl.when(step == 0)
    def _():
        s_scr[...] = jnp.zeros_like(s_scr)

    masks = _wkv_masks()
    diag = masks[1]
    sls = [slice(u * CHUNK, (u + 1) * CHUNK) for u in range(nsub)]
    gls = [slice(g * GW, (g + 1) * GW) for g in range(ng)]
    items = [(u, g) for u in range(nsub) for g in range(ng)]
    chunks = lambda ref: [ref[sls[u], gls[g]] for u, g in items]
    qc, y0, mc, cc, cl = _wkv_prep_chunks(chunks(r_ref), chunks(k_ref), chunks(v_ref), chunks(lw_ref),
                                          chunks(kk_ref), chunks(a_ref), masks)

    states = [s_scr[:, gl] for gl in gls]
    for u in range(nsub):
        sb = [_bf(s) for s in states]
        upd = [_dot(sb[g], _bd(_bf(mc[u * ng + g]), diag)) for g in range(ng)]
        yy = [_dot_nt(_bf(qc[u * ng + g]), _bd(sb[g], diag)) for g in range(ng)]
        for g in range(ng):
            i = u * ng + g
            states[g] = states[g] * jnp.exp(cl[i]) + upd[g] + cc[i]
            y_o[sls[u], gls[g]] = yy[g] + y0[i]
    for g, gl in enumerate(gls):
        s_scr[:, gl] = states[g]

    @pl.when(step == pl.num_programs(1) - 1)
    def _():
        for g in range(ng):
            s_o[0, g] = states[g]


def _wkv_seq(r, k, v, lw, kk, a, batch, seq_len):
    t, d = r.shape
    nsub = 4
    rows = nsub * CHUNK
    ng = d // GW
    steps = seq_len // rows
    blk = pl.BlockSpec((rows, d), lambda i, j: (i * steps + j, 0))
    return pl.pallas_call(
        functools.partial(_wkv_seq_kernel, nsub, ng),
        grid=(batch, steps),
        in_specs=[blk] * 6,
        out_specs=[blk, pl.BlockSpec((1, ng, HEAD_DIM, GW), lambda i, j: (i, 0, 0, 0))],
        out_shape=[jax.ShapeDtypeStruct((t, d), F32), jax.ShapeDtypeStruct((batch, ng, HEAD_DIM, GW), F32)],
        scratch_shapes=[pltpu.VMEM((HEAD_DIM, d), F32)],
        compiler_params=pltpu.CompilerParams(dimension_semantics=("parallel", "arbitrary"),
                                             vmem_limit_bytes=VMEM_LIMIT),
        name="wkv_seq",
    )(r, k, v, lw, kk, a)


def _wkv_step_kernel(s_ref, r_ref, w_ref, k_ref, v_ref, a_ref, b_ref, so_ref, y_ref):
    n = HEAD_DIM
    eye = (lax.broadcasted_iota(jnp.int32, (n, n), 0) == lax.broadcasted_iota(jnp.int32, (n, n), 1)).astype(F32)
    s = s_ref[...]
    sa = jnp.sum(s * a_ref[...], axis=-1, keepdims=True)
    vcol = jnp.sum(eye * v_ref[...], axis=-1, keepdims=True)
    s = s * w_ref[...] + sa * b_ref[...] + vcol * k_ref[...]
    so_ref[...] = s
    ycol = jnp.sum(s * r_ref[...], axis=-1, keepdims=True)
    y_ref[...] = jnp.sum(eye * ycol, axis=-2, keepdims=True)


def _wkv_step(state, r, decay, k, v, a, b):
    n = state.shape[0]
    nb = 32
    sspec = pl.BlockSpec((nb, HEAD_DIM, HEAD_DIM), lambda i: (i, 0, 0))
    vspec = pl.BlockSpec((nb, 1, HEAD_DIM), lambda i: (i, 0, 0))
    return pl.pallas_call(
        _wkv_step_kernel,
        grid=(n // nb,),
        in_specs=[sspec] + [vspec] * 6,
        out_specs=[sspec, vspec],
        out_shape=[jax.ShapeDtypeStruct(state.shape, F32), jax.ShapeDtypeStruct((n, 1, HEAD_DIM), F32)],
        compiler_params=pltpu.CompilerParams(dimension_semantics=("parallel",)),
        name="wkv_step",
    )(state, r, decay, k, v, a, b)


def _post_kernel(rwkv, final, per_row, *refs):
    refs = list(refs)
    x_ref = refs.pop(0)
    z_ref = refs.pop(0)
    if rwkv:
        r_ref, k_ref, v_ref, g_ref = refs[:4]
        refs = refs[4:]
    mod_ref = refs.pop(0)
    if rwkv:
        lg_ref, lb_ref, rk_ref, ones_ref = refs[:4]
        refs = refs[4:]
    wo_ref, n2_ref, up_ref, down_ref = refs[:4]
    refs = refs[4:]
    if final:
        fg_ref = refs.pop(0)
    x_o = refs.pop(0)
    d = x_ref.shape[1]

    z = z_ref[...]
    if rwkv:
        z = _rwkv_epilogue(z, r_ref[...], k_ref[...], v_ref[...], g_ref[...],
                           lg_ref[...], lb_ref[...], rk_ref[...], ones_ref[...])
    out = _dot(_bf(z), wo_ref[...])
    x1 = x_ref[...] + _mod(mod_ref, per_row, 2, d) * out
    h2 = _bf(_prenorm(x1, n2_ref[...], _mod(mod_ref, per_row, 3, d), _mod(mod_ref, per_row, 4, d)))
    dff = up_ref.shape[1]
    fc = 1024
    acc = None
    for c in range(dff // fc):
        hid = jnp.maximum(_dot(h2, up_ref[:, c * fc:(c + 1) * fc]), 0.0)
        part = _dot(_bf(hid * hid), down_ref[c * fc:(c + 1) * fc, :])
        acc = part if acc is None else acc + part
    x2 = x1 + _mod(mod_ref, per_row, 5, d) * acc
    x_o[...] = x2
    if final:
        refs[0][...] = _rms(x2, fg_ref[...])


def _post(x, z, extra, mod, per_row, rw, wo, n2g, up, down, final_g, tm, seq_len):
    t, d = x.shape
    rwkv = rw is not None
    final = final_g is not None
    row = lambda i: (i, 0)
    tile = pl.BlockSpec((tm, d), row)
    tiles_per_seq = 1 if per_row else seq_len // tm
    mod_spec = (pl.BlockSpec((tm, N_ADA * d), row) if per_row
                else pl.BlockSpec((1, N_ADA, d), lambda i: (i // tiles_per_seq, 0, 0)))
    args = [x, z]
    specs = [tile, tile]
    if rwkv:
        args += list(extra)
        specs += [tile] * 4
    args.append(mod)
    specs.append(mod_spec)
    consts = (list(rw) if rwkv else []) + [wo, n2g, up, down] + ([final_g] if final else [])
    args += consts
    specs += [_const_spec(c.shape) for c in consts]
    act = jax.ShapeDtypeStruct((t, d), F32)
    outs = pl.pallas_call(
        functools.partial(_post_kernel, rwkv, final, per_row),
        grid=(t // tm,),
        in_specs=specs,
        out_specs=[tile] * (2 if final else 1),
        out_shape=[act] * (2 if final else 1),
        compiler_params=pltpu.CompilerParams(dimension_semantics=("parallel",), vmem_limit_bytes=VMEM_LIMIT),
        name=("post_rwkv" if rwkv else "post") + ("_final" if final else "") + ("_step" if per_row else "_seq"),
    )(*args)
    return outs


def _rope_table_kernel(base, step, invf_ref, c_o, sm_o, sp_o):
    shape = c_o.shape
    row = lax.broadcasted_iota(jnp.int32, shape, 0) + pl.program_id(0) * shape[0]
    pos = (base + step * row).astype(F32)
    ang = pos * invf_ref[...]
    lane = lax.broadcasted_iota(jnp.int32, shape, 1) % HEAD_DIM
    cos = jnp.cos(ang)
    sin = jnp.sin(ang)
    half = ROPE_DIM // 2
    c_o[...] = jnp.where(lane < ROPE_DIM, cos, 1.0)
    sm_o[...] = jnp.where(lane < half, -sin, 0.0)
    sp_o[...] = jnp.where((lane >= half) & (lane < ROPE_DIM), sin, 0.0)


def _rope_tables(n, base, step):
    half = ROPE_DIM // 2
    inv_freq = ROPE_THETA ** (-jnp.arange(half, dtype=F32) / half)
    lane = jnp.arange(LANE) % HEAD_DIM
    invf = jnp.where(lane < ROPE_DIM, inv_freq[lane % half], 0.0).reshape(1, LANE).astype(F32)
    tr = min(n, 512)
    spec = pl.BlockSpec((tr, LANE), lambda i: (i, 0))
    shp = jax.ShapeDtypeStruct((n, LANE), F32)
    return pl.pallas_call(
        functools.partial(_rope_table_kernel, base, step),
        grid=(n // tr,),
        in_specs=[pl.BlockSpec((1, LANE), lambda i: (0, 0))],
        out_specs=[spec] * 3,
        out_shape=[shp] * 3,
        compiler_params=pltpu.CompilerParams(dimension_semantics=("parallel",)),
        name="rope_tables",
    )(invf)


def _qkv_kernel(per_row, x_ref, mod_ref, ng_ref, w_ref, b_ref, c_ref, sm_ref, sp_ref, q_o, k_o, v_o):
    d = x_ref.shape[1]
    h = _prenorm(x_ref[...], ng_ref[...], _mod(mod_ref, per_row, 0, d), _mod(mod_ref, per_row, 1, d))
    qkv = _dot(_bf(h), w_ref[...]) + b_ref[...]
    c, sm, sp = c_ref[...], sm_ref[...], sp_ref[...]
    nq, nk = q_o.shape[1], k_o.shape[1]

    def rope_into(o_ref, off, width):
        for j in range(width // LANE):
            zc = qkv[:, off + j * LANE:off + (j + 1) * LANE]
            o_ref[:, j * LANE:(j + 1) * LANE] = (zc * c + pltpu.roll(zc, LANE - ROPE_DIM // 2, axis=1) * sm
                                                 + pltpu.roll(zc, ROPE_DIM // 2, axis=1) * sp)

    rope_into(q_o, 0, nq)
    rope_into(k_o, nq, nk)
    v_o[...] = qkv[:, nq + nk:]


def _qkv(x, mod, per_row, ng, w, b, tables, tm, seq_len):
    t, d = x.shape
    nkv = KV_HEADS * HEAD_DIM
    row = lambda i: (i, 0)
    tiles_per_seq = 1 if per_row else seq_len // tm
    mod_spec = (pl.BlockSpec((tm, N_ADA * d), row) if per_row
                else pl.BlockSpec((1, N_ADA, d), lambda i: (i // tiles_per_seq, 0, 0)))
    tab_spec = pl.BlockSpec((tm, LANE), lambda i: (i % tiles_per_seq, 0))
    return pl.pallas_call(
        functools.partial(_qkv_kernel, per_row),
        grid=(t // tm,),
        in_specs=[pl.BlockSpec((tm, d), row), mod_spec, _const_spec(ng.shape), _const_spec(w.shape),
                  _const_spec(b.shape)] + [tab_spec] * 3,
        out_specs=[pl.BlockSpec((tm, d), row), pl.BlockSpec((tm, nkv), row), pl.BlockSpec((tm, nkv), row)],
        out_shape=[jax.ShapeDtypeStruct((t, d), F32), jax.ShapeDtypeStruct((t, nkv), F32),
                   jax.ShapeDtypeStruct((t, nkv), F32)],
        compiler_params=pltpu.CompilerParams(dimension_semantics=("parallel",), vmem_limit_bytes=VMEM_LIMIT),
        name="qkv_step" if per_row else "qkv_seq",
    )(x, mod, ng, w, b, *tables)


def _sink_softmax(s, sink):
    m = jnp.maximum(jnp.max(s, axis=-1, keepdims=True), sink)
    p = jnp.exp(s - m)
    return p / (jnp.sum(p, axis=-1, keepdims=True) + jnp.exp(sink - m))


def _swa_seq_kernel(q_ref, kc_ref, kp_ref, vc_ref, vp_ref, sink_ref, o_ref):
    n = pl.program_id(1)
    q = q_ref[...]
    kall = jnp.concatenate([kp_ref[...], kc_ref[...]], axis=0)
    vall = jnp.concatenate([vp_ref[...], vc_ref[...]], axis=0)
    blk = ATT_BLOCK
    i = lax.broadcasted_iota(jnp.int32, (blk, 2 * blk), 0)
    j = lax.broadcasted_iota(jnp.int32, (blk, 2 * blk), 1)
    mask = (j >= i + (blk - WINDOW)) & (j <= i + blk) & ((n > 0) | (j >= blk))
    heads = range(KV_HEADS * GROUP)
    hsl = [slice(h * HEAD_DIM, (h + 1) * HEAD_DIM) for h in heads]
    kh = [_bf(kall[:, hsl[kvh]]) for kvh in range(KV_HEADS)]
    vh = [_bf(vall[:, hsl[kvh]]) for kvh in range(KV_HEADS)]
    qh = [_bf(q[:, hsl[h]]) for h in heads]
    s = [_dot_nt(qh[h], kh[h // GROUP]) * (HEAD_DIM ** -0.5) for h in heads]
    s = [jnp.where(mask, s[h], -jnp.inf) for h in heads]
    sink = [sink_ref[0:1, h:h + 1] for h in heads]
    m = [jnp.maximum(jnp.max(s[h], axis=-1, keepdims=True), sink[h]) for h in heads]
    p = [jnp.exp(s[h] - m[h]) for h in heads]
    inv = [1.0 / (jnp.sum(p[h], axis=-1, keepdims=True) + jnp.exp(sink[h] - m[h])) for h in heads]
    o = [_dot(_bf(p[h] * inv[h]), vh[h // GROUP]) for h in heads]
    for h in heads:
        o_ref[:, hsl[h]] = o[h]


def _swa_seq(q, k, v, sink, batch, seq_len):
    t, d = q.shape
    nkv = k.shape[1]
    nb = seq_len // ATT_BLOCK
    cur = lambda b, n: (b * nb + n, 0)
    prv = lambda b, n: (b * nb + jnp.maximum(n - 1, 0), 0)
    return pl.pallas_call(
        _swa_seq_kernel,
        grid=(batch, nb),
        in_specs=[pl.BlockSpec((ATT_BLOCK, d), cur),
                  pl.BlockSpec((ATT_BLOCK, nkv), cur), pl.BlockSpec((ATT_BLOCK, nkv), prv),
                  pl.BlockSpec((ATT_BLOCK, nkv), cur), pl.BlockSpec((ATT_BLOCK, nkv), prv),
                  pl.BlockSpec(sink.shape, lambda b, n: (0, 0))],
        out_specs=pl.BlockSpec((ATT_BLOCK, d), cur),
        out_shape=jax.ShapeDtypeStruct((t, d), F32),
        compiler_params=pltpu.CompilerParams(dimension_semantics=("parallel", "parallel")),
        name="swa_seq",
    )(q, k, k, v, v, sink)


def _swa_step_kernel(q_ref, kn_ref, vn_ref, kc_ref, vc_ref, sink_ref, o_ref, ko_ref, vo_ref):
    w = kc_ref.shape[1]
    kc = kc_ref[...]
    vc = vc_ref[...]
    kn = kn_ref[...]
    vn = vn_ref[...]
    q = q_ref[...]
    scale = HEAD_DIM ** -0.5
    for kvh in range(KV_HEADS):
        ksl = slice(kvh * HEAD_DIM, (kvh + 1) * HEAD_DIM)
        kh, vh = kc[:, :, ksl], vc[:, :, ksl]
        knh, vnh = kn[:, :, ksl], vn[:, :, ksl]
        qh = q[:, kvh * GROUP:(kvh + 1) * GROUP, :]
        s_old = jnp.einsum("bgd,bkd->bgk", _bf(qh), _bf(kh), preferred_element_type=F32) * scale
        s_new = jnp.sum(_bf(qh).astype(F32) * _bf(knh).astype(F32), axis=-1, keepdims=True) * scale
        sink = sink_ref[kvh]
        m = jnp.maximum(jnp.maximum(jnp.max(s_old, axis=-1, keepdims=True), s_new), sink)
        p_old = jnp.exp(s_old - m)
        p_new = jnp.exp(s_new - m)
        den = jnp.sum(p_old, axis=-1, keepdims=True) + p_new + jnp.exp(sink - m)
        o = jnp.einsum("bgk,bkd->bgd", _bf(p_old / den), _bf(vh), preferred_element_type=F32)
        o = o + _bf(p_new / den).astype(F32) * _bf(vnh).astype(F32)
        o_ref[:, kvh * GROUP:(kvh + 1) * GROUP, :] = o
    rowid = lax.broadcasted_iota(jnp.int32, kc.shape, 1)
    ko_ref[...] = jnp.where(rowid == w - 1, kn, pltpu.roll(kc, w - 1, axis=1))
    vo_ref[...] = jnp.where(rowid == w - 1, vn, pltpu.roll(vc, w - 1, axis=1))


def _swa_step(q, kn, vn, cache_k, cache_v, sink):
    b, nh, hd = q.shape
    w, nkv = cache_k.shape[1], cache_k.shape[2]
    nb = 8
    i3 = lambda i: (i, 0, 0)
    cspec = pl.BlockSpec((nb, w, nkv), i3)
    nspec = pl.BlockSpec((nb, 1, nkv), i3)
    qspec = pl.BlockSpec((nb, nh, hd), i3)
    return pl.pallas_call(
        _swa_step_kernel,
        grid=(b // nb,),
        in_specs=[qspec, nspec, nspec, cspec, cspec, pl.BlockSpec(sink.shape, lambda i: (0, 0, 0))],
        out_specs=[qspec, cspec, cspec],
        out_shape=[jax.ShapeDtypeStruct(q.shape, F32), jax.ShapeDtypeStruct(cache_k.shape, F32),
                   jax.ShapeDtypeStruct(cache_v.shape, F32)],
        compiler_params=pltpu.CompilerParams(dimension_semantics=("parallel",)),
        name="swa_step",
    )(q, kn, vn, cache_k, cache_v, sink)


def kernel(x_prompt, x_sample, c_prompt, c_sample, state_wkv, state_shift, cache_k, cache_v, norm1_g, norm2_g, ada_w, ada_b, mlp_up, mlp_down, final_g, rw_mix, rw_wr, rw_wk, rw_wv, rw_wo, rw_w0, rw_w1, rw_w2, rw_a0, rw_a1, rw_a2, rw_g1, rw_g2, rw_kk, rw_ka, rw_rk, rw_lnx_g, rw_lnx_b, at_wqkv, at_bqkv, at_wo, at_sink):
    bp, seq, d = x_prompt.shape
    bs = x_sample.shape[0]
    assert x_sample.shape[1] == 1 and d % GW == 0 and seq % (8 * CHUNK) == 0
    nh = d // HEAD_DIM
    tp = bp * seq
    row = lambda vec: vec.reshape(1, -1)

    n_c = bp + bs
    pad = (-n_c) % 8
    c_all = jnp.concatenate([c_prompt, c_sample, jnp.zeros((pad, d), F32)], axis=0)
    ada = _ada(c_all, ada_w, ada_b)
    mod_p = [ada[i, :bp].reshape(bp, N_ADA, d) for i in range(2)]
    mod_s = [ada[i, bp:bp + bs] for i in range(2)]

    xp = x_prompt.reshape(tp, d)
    xs = x_sample.reshape(bs, d)

    head_id = jnp.arange(GW) // HEAD_DIM
    ones_bd = (head_id[:, None] == head_id[None, :]).astype(BF16)

    rwp = dict(mix=rw_mix[0], wr=_bf(rw_wr[0]), wk=_bf(rw_wk[0]), wv=_bf(rw_wv[0]),
               w1=_bf(rw_w1[0]), w2=_bf(rw_w2[0]), w0=row(rw_w0[0]),
               a1=_bf(rw_a1[0]), a2=_bf(rw_a2[0]), a0=row(rw_a0[0]),
               g1=_bf(rw_g1[0]), g2=_bf(rw_g2[0]), kk=row(rw_kk[0]), ka=row(rw_ka[0]), ones=ones_bd)
    n1 = row(norm1_g[0])
    post_rw = (row(rw_lnx_g[0]), row(rw_lnx_b[0]), row(rw_rk[0]), ones_bd)
    wo0, up0, down0 = _bf(rw_wo[0]), _bf(mlp_up[0]), _bf(mlp_down[0])
    n2 = row(norm2_g[0])

    r, k, v, lw, kk, a, g, hlast = _rwkv_proj(xp, None, mod_p[0], n1, rwp, 256, seq)
    tiles = seq // 256
    shift_p = hlast.reshape(bp, tiles, 8, d)[:, -1, -1]
    y, st = _wkv_seq(r, k, v, lw, kk, a, bp, seq)
    wkv_p = st.reshape(bp, d // GW, HEAD_DIM, HPG, HEAD_DIM).transpose(0, 1, 3, 2, 4).reshape(bp, nh, HEAD_DIM, HEAD_DIM)
    (xp,) = _post(xp, y, (r, k, v, g), mod_p[0], False, post_rw, wo0, n2, up0, down0, None, 256, seq)

    rs, ks, vs, lws, kks, as_, gs, shift_s = _rwkv_proj(xs, state_shift[0], mod_s[0], n1, rwp, bs, 1)
    vec = lambda z: z.reshape(bs * nh, 1, HEAD_DIM)
    st_s, ys = _wkv_step(state_wkv[0].reshape(bs * nh, HEAD_DIM, HEAD_DIM), vec(rs), vec(jnp.exp(lws)),
                         vec(ks), vec(vs), vec(-kks), vec(kks * as_))
    wkv_s = st_s.reshape(bs, nh, HEAD_DIM, HEAD_DIM)
    (xs,) = _post(xs, ys.reshape(bs, d), (rs, ks, vs, gs), mod_s[0], True, post_rw, wo0, n2, up0, down0, None, bs, 1)

    wqkv, bqkv, wo1 = _bf(at_wqkv[0]), row(at_bqkv[0]), _bf(at_wo[0])
    n1, n2 = row(norm1_g[1]), row(norm2_g[1])
    up1, down1 = _bf(mlp_up[1]), _bf(mlp_down[1])
    fg = row(final_g)
    nkv = KV_HEADS * HEAD_DIM

    q, kq, vq = _qkv(xp, mod_p[1], False, n1, wqkv, bqkv, _rope_tables(seq, 0, 1), 256, seq)
    o = _swa_seq(q, kq, vq, row(at_sink[0]), bp, seq)
    keep = min(WINDOW, seq)
    k_p = kq.reshape(bp, seq, nkv)[:, seq - keep:].reshape(bp, keep, KV_HEADS, HEAD_DIM)
    v_p = vq.reshape(bp, seq, nkv)[:, seq - keep:].reshape(bp, keep, KV_HEADS, HEAD_DIM)
    xp, y_p = _post(xp, o, None, mod_p[1], False, None, wo1, n2, up1, down1, fg, 256, seq)

    qs, kn, vn = _qkv(xs, mod_s[1], True, n1, wqkv, bqkv, _rope_tables(bs, PAST_LEN, 0), bs, 1)
    w_buf = cache_k.shape[2]
    os_, k_s, v_s = _swa_step(qs.reshape(bs, nh, HEAD_DIM), kn.reshape(bs, 1, nkv), vn.reshape(bs, 1, nkv),
                              cache_k[0].reshape(bs, w_buf, nkv), cache_v[0].reshape(bs, w_buf, nkv),
                              at_sink[0].reshape(KV_HEADS, GROUP, 1))
    xs, y_s = _post(xs, os_.reshape(bs, d), None, mod_s[1], True, None, wo1, n2, up1, down1, fg, bs, 1)

    cshape = (1, bs, w_buf, KV_HEADS, HEAD_DIM)
    return (y_p.reshape(bp, seq, d), y_s.reshape(bs, 1, d),
            wkv_p[None], wkv_s[None], shift_p[None], shift_s[None],
            k_p[None], k_s.reshape(cshape), v_p[None], v_s.reshape(cshape))
```

```python
import functools
import math

import jax
import jax.numpy as jnp
from jax import lax
from jax.experimental import pallas as pl
from jax.experimental.pallas import tpu as pltpu

F32 = jnp.float32
BF16 = jnp.bfloat16

HEAD_DIM = 64
KV_HEADS = 4
GROUP = 4
WINDOW = 128
ATT_BLOCK = 128
ROPE_DIM = HEAD_DIM // 4
ROPE_THETA = 500000.0
PAST_LEN = 8192
NORM_EPS = 1e-6
LNX_EPS = 64e-5
N_ADA = 6

LANE = 128
V7X_VMEM_BYTES = 64 * 1024 * 1024
VMEM_LIMIT = V7X_VMEM_BYTES * 7 // 8

CHUNK = 64
HPG = 4
GW = HPG * HEAD_DIM


def _bf(x):
    return x.astype(BF16)


def _dot(a, b):
    return jnp.dot(a, b, preferred_element_type=F32)


def _dot_nt(a, b):
    return lax.dot_general(a, b, (((1,), (1,)), ((), ())), preferred_element_type=F32)


def _rms(x, g):
    ms = jnp.mean(x * x, axis=-1, keepdims=True)
    return x * lax.rsqrt(ms + NORM_EPS) * g


def _prenorm(x, g, shift, scale):
    return _rms(x, g) * (1.0 + scale) + shift


def _mod(mod_ref, per_row, idx, d):
    if per_row:
        return mod_ref[:, idx * d:(idx + 1) * d]
    return mod_ref[0, idx:idx + 1, :]


def _pieces(x, n):
    out = []
    rem = x
    for i in range(n):
        p = rem.astype(BF16)
        out.append(p)
        if i + 1 < n:
            rem = rem - p.astype(F32)
    return out


def _mm(xp, yp, dn):
    n = max(len(xp), len(yp))
    acc = None
    for i, xi in enumerate(xp):
        for j, yj in enumerate(yp):
            if i + j < n:
                t = lax.dot_general(xi, yj, dn, preferred_element_type=F32)
                acc = t if acc is None else acc + t
    return acc


_NN = (((1,), (0,)), ((), ()))
_NT = (((1,), (1,)), ((), ()))


def _head_sum(val, ones, n):
    gw = ones.shape[0]
    cols = []
    for j in range(val.shape[1] // gw):
        cols.append(_mm(_pieces(val[:, j * gw:(j + 1) * gw], n), [ones], _NN))
    return jnp.concatenate(cols, axis=1)


def _ada_kernel(c_ref, w_ref, b_ref, o_ref):
    c = c_ref[...]
    s = c * jax.nn.sigmoid(c)
    o_ref[0] = _dot(_bf(s), _bf(w_ref[0])) + b_ref[0]


def _ada(c_all, ada_w, ada_b):
    depth, d, n = ada_w.shape
    m = c_all.shape[0]
    tn = 512
    return pl.pallas_call(
        _ada_kernel,
        grid=(depth, n // tn),
        in_specs=[
            pl.BlockSpec((m, d), lambda i, j: (0, 0)),
            pl.BlockSpec((1, d, tn), lambda i, j: (i, 0, j)),
            pl.BlockSpec((1, 1, tn), lambda i, j: (i, 0, j)),
        ],
        out_specs=pl.BlockSpec((1, m, tn), lambda i, j: (i, 0, j)),
        out_shape=jax.ShapeDtypeStruct((depth, m, n), F32),
        compiler_params=pltpu.CompilerParams(dimension_semantics=("parallel", "parallel")),
        name="ada",
    )(c_all, ada_w, ada_b.reshape(depth, 1, n))


def _rwkv_proj_kernel(per_row, tiles_per_seq,
                      x_ref, xp_ref, mod_ref, ng_ref, mix_ref, wr_ref, wk_ref, wv_ref,
                      w1_ref, w2_ref, w0_ref, a1_ref, a2_ref, a0_ref, g1_ref, g2_ref,
                      kk_ref, ka_ref, ones_ref,
                      r_o, k_o, v_o, lw_o, kk_o, a_o, g_o, h_o):
    d = x_ref.shape[1]
    tm = x_ref.shape[0]
    ng = ng_ref[...]
    shift = _mod(mod_ref, per_row, 0, d)
    scale = _mod(mod_ref, per_row, 1, d)
    h = _prenorm(x_ref[...], ng, shift, scale)
    if per_row:
        prev = xp_ref[...]
        h_o[...] = h
    else:
        hp = _prenorm(xp_ref[...], ng, shift, scale)
        first = (pl.program_id(0) % tiles_per_seq) == 0
        prow = jnp.where(first, 0.0, hp[7:8, :])
        rowid = lax.broadcasted_iota(jnp.int32, h.shape, 0)
        prev = jnp.where(rowid == 0, prow, pltpu.roll(h, 1, axis=0))
        h_o[0] = h[tm - 8:tm, :]
    xx = prev - h

    def mixed(j):
        return _bf(h + xx * mix_ref[j:j + 1, :])

    r = _dot(mixed(0), wr_ref[...])
    wl = w0_ref[...] + _dot(_bf(jnp.tanh(_dot(mixed(1), w1_ref[...]))), w2_ref[...])
    k = _dot(mixed(2), wk_ref[...])
    v = _dot(mixed(3), wv_ref[...])
    a = jax.nn.sigmoid(a0_ref[...] + _dot(_bf(_dot(mixed(4), a1_ref[...])), a2_ref[...]))
    g = _dot(_bf(jax.nn.sigmoid(_dot(mixed(5), g1_ref[...]))), g2_ref[...])
    kk = k * kk_ref[...]
    ss = _head_sum(kk * kk, ones_ref[...], 1)
    kk = kk * lax.rsqrt(jnp.maximum(ss, 1e-24))
    r_o[...] = r
    k_o[...] = k * (1.0 + (a - 1.0) * ka_ref[...])
    v_o[...] = v
    lw_o[...] = jax.nn.sigmoid(wl) * (-math.exp(-0.5))
    kk_o[...] = kk
    a_o[...] = a
    g_o[...] = g


def _const_spec(shape):
    nd = len(shape)
    return pl.BlockSpec(shape, lambda *_: (0,) * nd, pipeline_mode=pl.Buffered(1))


def _rwkv_proj(x, prev_rows, mod, ng, p, tm, seq_len):
    t, d = x.shape
    per_row = prev_rows is not None
    nt = t // tm
    tiles_per_seq = 1 if per_row else seq_len // tm
    row = lambda i: (i, 0)
    if per_row:
        xp, xp_spec = prev_rows, pl.BlockSpec((tm, d), row)
        mod_spec = pl.BlockSpec((tm, N_ADA * d), row)
        h_shape, h_spec = jax.ShapeDtypeStruct((t, d), F32), pl.BlockSpec((tm, d), row)
    else:
        xp, xp_spec = x, pl.BlockSpec((8, d), lambda i: (jnp.maximum(i * (tm // 8) - 1, 0), 0))
        mod_spec = pl.BlockSpec((1, N_ADA, d), lambda i: (i // tiles_per_seq, 0, 0))
        h_shape, h_spec = jax.ShapeDtypeStruct((nt, 8, d), F32), pl.BlockSpec((1, 8, d), lambda i: (i, 0, 0))
    consts = [ng, p["mix"], p["wr"], p["wk"], p["wv"], p["w1"], p["w2"], p["w0"], p["a1"], p["a2"], p["a0"],
              p["g1"], p["g2"], p["kk"], p["ka"], p["ones"]]
    act = jax.ShapeDtypeStruct((t, d), F32)
    return pl.pallas_call(
        functools.partial(_rwkv_proj_kernel, per_row, tiles_per_seq),
        grid=(nt,),
        in_specs=[pl.BlockSpec((tm, d), row), xp_spec, mod_spec] + [_const_spec(c.shape) for c in consts],
        out_specs=[pl.BlockSpec((tm, d), row)] * 7 + [h_spec],
        out_shape=[act] * 7 + [h_shape],
        compiler_params=pltpu.CompilerParams(dimension_semantics=("parallel",), vmem_limit_bytes=VMEM_LIMIT),
        name="rwkv_proj_step" if per_row else "rwkv_proj_seq",
    )(x, xp, mod, *consts)


def _head_of_lane(shape, axis):
    return lax.broadcasted_iota(jnp.int32, shape, axis) // HEAD_DIM


def _bd(p, low_half):
    zero = jnp.zeros((), p.dtype)
    zeros = jnp.zeros((p.shape[0], LANE), p.dtype)
    per_col = LANE // HEAD_DIM
    ncol = GW // LANE
    blocks = []
    for h in range(HPG):
        c = h // per_col
        col = p[:, c * LANE:(c + 1) * LANE]
        col = jnp.where(low_half, col, zero) if h % per_col == 0 else jnp.where(low_half, zero, col)
        blocks.append(jnp.concatenate([col if j == c else zeros for j in range(ncol)], axis=1))
    return jnp.concatenate(blocks, axis=0)


def _wkv_prep_chunks(rs, ks, vs, lws, kks, as_, masks):
    tri, diag, strict, incl, eye, blk8, levels, lane_head = masks
    n = len(rs)

    def each(f, *args):
        return [f(*[a[i] for a in args]) for i in range(n)]

    def pp(xs, ys):
        rhs = each(lambda y: _bd(_bf(y), diag), ys)
        return each(lambda x, b: _dot(_bf(x), b), xs, rhs)

    def pabt(xs, ys):
        rhs = each(lambda y: _bd(_bf(y), diag), ys)
        return each(lambda x, b: _dot_nt(_bf(x), b), xs, rhs)

    def pick_diag(gram):
        out = gram[0:HEAD_DIM, :]
        for hh in range(1, HPG):
            out = jnp.where(lane_head == hh, gram[hh * HEAD_DIM:(hh + 1) * HEAD_DIM, :], out)
        return out

    def patb(xs, ys):
        lhs = each(lambda x: _bf(x.T), xs)
        return each(lambda xt, y: pick_diag(_dot(xt, _bf(y))), lhs, ys)

    def add(xs, ys):
        return each(lambda x, y: x + y, xs, ys)

    def masked(m, xs):
        return each(lambda x: jnp.where(m, x, 0.0), xs)

    c = each(lambda lw: _mm([tri], _pieces(lw, 2), _NN), lws)
    c_last = each(lambda ci: ci[CHUNK - 1:CHUNK, :], c)
    e_neg = each(lambda ci: jnp.exp(-ci), c)
    e_end = each(lambda ci, cl: jnp.exp(cl - ci), c, c_last)
    bv = each(lambda kk, a: kk * a, kks, as_)
    a_t = each(lambda kk, ci, lw: _bf(-kk * jnp.exp(ci - lw)), kks, c, lws)
    r_t = each(lambda r, ci: r * jnp.exp(ci), rs, c)
    b_t = each(lambda b, e: _bf(b * e), bv, e_neg)
    k_t = each(lambda k, e: _bf(k * e), ks, e_neg)
    b_e = each(lambda b, e: _bf(b * e), bv, e_end)
    k_e = each(lambda k, e: _bf(k * e), ks, e_end)

    ar = each(lambda x, y: jnp.concatenate([x, _bf(y)], axis=0), a_t, r_t)
    xb = pabt(ar, b_t)
    xk = pabt(ar, k_t)
    a_ab = each(lambda x: jnp.where(strict, x[:CHUNK], 0.0), xb)
    a_rb = each(lambda x: jnp.where(incl, x[CHUNK:], 0.0), xb)
    a_ak = each(lambda x: jnp.where(strict, x[:CHUNK], 0.0), xk)
    a_rk = each(lambda x: jnp.where(incl, x[CHUNK:], 0.0), xk)

    n0 = masked(blk8, a_ab)
    n2 = pp(n0, n0)
    n4 = pp(n2, n2)
    plus_eye = lambda xs: each(lambda x: eye + x, xs)
    tinv = pp(pp(plus_eye(n0), plus_eye(n2)), plus_eye(n4))
    b = 8
    for lvl in levels:
        lower = lambda x: jnp.concatenate([x[r0:r0 + b] for r0 in range(b, CHUNK, 2 * b)], axis=0)
        z = pp(pp(each(lower, tinv), masked(lvl, a_ab)), tinv)

        def merged(t, zi):
            rows = []
            for j, r0 in enumerate(range(0, CHUNK, 2 * b)):
                rows += [t[r0:r0 + b], t[r0 + b:r0 + 2 * b] + zi[j * b:(j + 1) * b]]
            return jnp.concatenate(rows, axis=0)

        tinv = each(merged, tinv, z)
        b *= 2

    w1 = pp(tinv, a_t)
    w2 = pp(tinv, pp(a_ak, vs))
    qc = add(r_t, pp(a_rb, w1))
    y0 = add(pp(a_rb, w2), pp(a_rk, vs))
    mc = patb(w1, b_e)
    stack = lambda x, y: jnp.concatenate([x, y], axis=0)
    cc = patb(each(stack, w2, vs), each(stack, b_e, k_e))
    return qc, y0, mc, cc, c_last


def _wkv_masks():
    t = lax.broadcasted_iota(jnp.int32, (CHUNK, GW), 0)
    s = lax.broadcasted_iota(jnp.int32, (CHUNK, GW), 1) % HEAD_DIM
    tri = (lax.broadcasted_iota(jnp.int32, (CHUNK, CHUNK), 1)
           <= lax.broadcasted_iota(jnp.int32, (CHUNK, CHUNK), 0)).astype(BF16)
    diag = lax.broadcasted_iota(jnp.int32, (CHUNK, LANE), 1) < HEAD_DIM
    strict = s < t
    incl = s <= t
    eye = (s == t).astype(F32)
    blk8 = strict & ((s // 8) == (t // 8))
    levels = []
    b = 8
    while b < CHUNK:
        levels.append(((s // (2 * b)) == (t // (2 * b))) & ((s // b) != (t // b)) & strict)
        b *= 2
    lane_head = _head_of_lane((HEAD_DIM, GW), 1)
    return tri, diag, strict, incl, eye, blk8, levels, lane_head


def _rwkv_epilogue(y, r, k, v, g, lg, lb, rk, ones):
    inv_n = 1.0 / HEAD_DIM
    mu = _head_sum(y, ones, 1) * inv_n
    dy = y - mu
    var = _head_sum(dy * dy, ones, 1) * inv_n
    yn = dy * lax.rsqrt(var + LNX_EPS) * lg + lb
    bonus = _head_sum(r * k * rk, ones, 1) * v
    return (yn + bonus) * g


def _wkv_seq_kernel(nsub, ng, r_ref, k_ref, v_ref, lw_ref, kk_ref, a_ref, y_o, s_o, s_scr):
    step = pl.program_id(1)

    @pl.when(step == 0)
    def _():
        s_scr[...] = jnp.zeros_like(s_scr)

    masks = _wkv_masks()
    diag = masks[1]
    sls = [slice(u * CHUNK, (u + 1) * CHUNK) for u in range(nsub)]
    gls = [slice(g * GW, (g + 1) * GW) for g in range(ng)]
    items = [(u, g) for u in range(nsub) for g in range(ng)]
    chunks = lambda ref: [ref[sls[u], gls[g]] for u, g in items]
    qc, y0, mc, cc, cl = _wkv_prep_chunks(chunks(r_ref), chunks(k_ref), chunks(v_ref), chunks(lw_ref),
                                          chunks(kk_ref), chunks(a_ref), masks)

    states = [s_scr[:, gl] for gl in gls]
    for u in range(nsub):
        sb = [_bf(s) for s in states]
        upd = [_dot(sb[g], _bd(_bf(mc[u * ng + g]), diag)) for g in range(ng)]
        yy = [_dot_nt(_bf(qc[u * ng + g]), _bd(sb[g], diag)) for g in range(ng)]
        for g in range(ng):
            i = u * ng + g
            states[g] = states[g] * jnp.exp(cl[i]) + upd[g] + cc[i]
            y_o[sls[u], gls[g]] = yy[g] + y0[i]
    for g, gl in enumerate(gls):
        s_scr[:, gl] = states[g]

    @pl.when(step == pl.num_programs(1) - 1)
    def _():
        for g in range(ng):
            s_o[0, g] = states[g]


def _wkv_seq(r, k, v, lw, kk, a, batch, seq_len):
    t, d = r.shape
    nsub = 4
    rows = nsub * CHUNK
    ng = d // GW
    steps = seq_len // rows
    blk = pl.BlockSpec((rows, d), lambda i, j: (i * steps + j, 0))
    return pl.pallas_call(
        functools.partial(_wkv_seq_kernel, nsub, ng),
        grid=(batch, steps),
        in_specs=[blk] * 6,
        out_specs=[blk, pl.BlockSpec((1, ng, HEAD_DIM, GW), lambda i, j: (i, 0, 0, 0))],
        out_shape=[jax.ShapeDtypeStruct((t, d), F32), jax.ShapeDtypeStruct((batch, ng, HEAD_DIM, GW), F32)],
        scratch_shapes=[pltpu.VMEM((HEAD_DIM, d), F32)],
        compiler_params=pltpu.CompilerParams(dimension_semantics=("parallel", "arbitrary"),
                                             vmem_limit_bytes=VMEM_LIMIT),
        name="wkv_seq",
    )(r, k, v, lw, kk, a)


def _wkv_step_kernel(s_ref, r_ref, w_ref, k_ref, v_ref, a_ref, b_ref, so_ref, y_ref):
    s = s_ref[0]
    sa = jnp.sum(s * a_ref[0][None], axis=1)
    s = s * w_ref[0][None] + sa[:, None, :] * b_ref[0][None] + v_ref[0][:, None, :] * k_ref[0][None]
    so_ref[0] = s
    y_ref[0] = jnp.sum(s * r_ref[0][None], axis=1)


def _wkv_step(state, r, decay, k, v, a, b):
    nh, n, _, nb = state.shape
    sspec = pl.BlockSpec((1, n, n, nb), lambda i: (i, 0, 0, 0))
    vspec = pl.BlockSpec((1, n, nb), lambda i: (i, 0, 0))
    return pl.pallas_call(
        _wkv_step_kernel,
        grid=(nh,),
        in_specs=[sspec] + [vspec] * 6,
        out_specs=[sspec, vspec],
        out_shape=[jax.ShapeDtypeStruct(state.shape, F32), jax.ShapeDtypeStruct((nh, n, nb), F32)],
        compiler_params=pltpu.CompilerParams(dimension_semantics=("parallel",)),
        name="wkv_step",
    )(state, r, decay, k, v, a, b)


def _post_kernel(rwkv, final, per_row, *refs):
    refs = list(refs)
    x_ref = refs.pop(0)
    z_ref = refs.pop(0)
    if rwkv:
        r_ref, k_ref, v_ref, g_ref = refs[:4]
        refs = refs[4:]
    mod_ref = refs.pop(0)
    if rwkv:
        lg_ref, lb_ref, rk_ref, ones_ref = refs[:4]
        refs = refs[4:]
    wo_ref, n2_ref, up_ref, down_ref = refs[:4]
    refs = refs[4:]
    if final:
        fg_ref = refs.pop(0)
    x_o = refs.pop(0)
    d = x_ref.shape[1]

    z = z_ref[...]
    if rwkv:
        z = _rwkv_epilogue(z, r_ref[...], k_ref[...], v_ref[...], g_ref[...],
                           lg_ref[...], lb_ref[...], rk_ref[...], ones_ref[...])
    out = _dot(_bf(z), wo_ref[...])
    x1 = x_ref[...] + _mod(mod_ref, per_row, 2, d) * out
    h2 = _bf(_prenorm(x1, n2_ref[...], _mod(mod_ref, per_row, 3, d), _mod(mod_ref, per_row, 4, d)))
    dff = up_ref.shape[1]
    fc = 1024
    acc = None
    for c in range(dff // fc):
        hid = jnp.maximum(_dot(h2, up_ref[:, c * fc:(c + 1) * fc]), 0.0)
        part = _dot(_bf(hid * hid), down_ref[c * fc:(c + 1) * fc, :])
        acc = part if acc is None else acc + part
    x2 = x1 + _mod(mod_ref, per_row, 5, d) * acc
    x_o[...] = x2
    if final:
        refs[0][...] = _rms(x2, fg_ref[...])


def _post(x, z, extra, mod, per_row, rw, wo, n2g, up, down, final_g, tm, seq_len):
    t, d = x.shape
    rwkv = rw is not None
    final = final_g is not None
    row = lambda i: (i, 0)
    tile = pl.BlockSpec((tm, d), row)
    tiles_per_seq = 1 if per_row else seq_len // tm
    mod_spec = (pl.BlockSpec((tm, N_ADA * d), row) if per_row
                else pl.BlockSpec((1, N_ADA, d), lambda i: (i // tiles_per_seq, 0, 0)))
    args = [x, z]
    specs = [tile, tile]
    if rwkv:
        args += list(extra)
        specs += [tile] * 4
    args.append(mod)
    specs.append(mod_spec)
    consts = (list(rw) if rwkv else []) + [wo, n2g, up, down] + ([final_g] if final else [])
    args += consts
    specs += [_const_spec(c.shape) for c in consts]
    act = jax.ShapeDtypeStruct((t, d), F32)
    outs = pl.pallas_call(
        functools.partial(_post_kernel, rwkv, final, per_row),
        grid=(t // tm,),
        in_specs=specs,
        out_specs=[tile] * (2 if final else 1),
        out_shape=[act] * (2 if final else 1),
        compiler_params=pltpu.CompilerParams(dimension_semantics=("parallel",), vmem_limit_bytes=VMEM_LIMIT),
        name=("post_rwkv" if rwkv else "post") + ("_final" if final else "") + ("_step" if per_row else "_seq"),
    )(*args)
    return outs


def _rope_table_kernel(base, step, invf_ref, c_o, sm_o, sp_o):
    shape = c_o.shape
    row = lax.broadcasted_iota(jnp.int32, shape, 0) + pl.program_id(0) * shape[0]
    pos = (base + step * row).astype(F32)
    ang = pos * invf_ref[...]
    lane = lax.broadcasted_iota(jnp.int32, shape, 1) % HEAD_DIM
    cos = jnp.cos(ang)
    sin = jnp.sin(ang)
    half = ROPE_DIM // 2
    c_o[...] = jnp.where(lane < ROPE_DIM, cos, 1.0)
    sm_o[...] = jnp.where(lane < half, -sin, 0.0)
    sp_o[...] = jnp.where((lane >= half) & (lane < ROPE_DIM), sin, 0.0)


def _rope_tables(n, base, step):
    half = ROPE_DIM // 2
    inv_freq = ROPE_THETA ** (-jnp.arange(half, dtype=F32) / half)
    lane = jnp.arange(LANE) % HEAD_DIM
    invf = jnp.where(lane < ROPE_DIM, inv_freq[lane % half], 0.0).reshape(1, LANE).astype(F32)
    tr = min(n, 512)
    spec = pl.BlockSpec((tr, LANE), lambda i: (i, 0))
    shp = jax.ShapeDtypeStruct((n, LANE), F32)
    return pl.pallas_call(
        functools.partial(_rope_table_kernel, base, step),
        grid=(n // tr,),
        in_specs=[pl.BlockSpec((1, LANE), lambda i: (0, 0))],
        out_specs=[spec] * 3,
        out_shape=[shp] * 3,
        compiler_params=pltpu.CompilerParams(dimension_semantics=("parallel",)),
        name="rope_tables",
    )(invf)


def _qkv_kernel(per_row, x_ref, mod_ref, ng_ref, w_ref, b_ref, c_ref, sm_ref, sp_ref, q_o, k_o, v_o):
    d = x_ref.shape[1]
    h = _prenorm(x_ref[...], ng_ref[...], _mod(mod_ref, per_row, 0, d), _mod(mod_ref, per_row, 1, d))
    qkv = _dot(_bf(h), w_ref[...]) + b_ref[...]
    c, sm, sp = c_ref[...], sm_ref[...], sp_ref[...]
    nq, nk = q_o.shape[1], k_o.shape[1]

    def rope_into(o_ref, off, width):
        for j in range(width // LANE):
            zc = qkv[:, off + j * LANE:off + (j + 1) * LANE]
            o_ref[:, j * LANE:(j + 1) * LANE] = (zc * c + pltpu.roll(zc, LANE - ROPE_DIM // 2, axis=1) * sm
                                                 + pltpu.roll(zc, ROPE_DIM // 2, axis=1) * sp)

    rope_into(q_o, 0, nq)
    rope_into(k_o, nq, nk)
    v_o[...] = qkv[:, nq + nk:]


def _qkv(x, mod, per_row, ng, w, b, tables, tm, seq_len):
    t, d = x.shape
    nkv = KV_HEADS * HEAD_DIM
    row = lambda i: (i, 0)
    tiles_per_seq = 1 if per_row else seq_len // tm
    mod_spec = (pl.BlockSpec((tm, N_ADA * d), row) if per_row
                else pl.BlockSpec((1, N_ADA, d), lambda i: (i // tiles_per_seq, 0, 0)))
    tab_spec = pl.BlockSpec((tm, LANE), lambda i: (i % tiles_per_seq, 0))
    return pl.pallas_call(
        functools.partial(_qkv_kernel, per_row),
        grid=(t // tm,),
        in_specs=[pl.BlockSpec((tm, d), row), mod_spec, _const_spec(ng.shape), _const_spec(w.shape),
                  _const_spec(b.shape)] + [tab_spec] * 3,
        out_specs=[pl.BlockSpec((tm, d), row), pl.BlockSpec((tm, nkv), row), pl.BlockSpec((tm, nkv), row)],
        out_shape=[jax.ShapeDtypeStruct((t, d), F32), jax.ShapeDtypeStruct((t, nkv), F32),
                   jax.ShapeDtypeStruct((t, nkv), F32)],
        compiler_params=pltpu.CompilerParams(dimension_semantics=("parallel",), vmem_limit_bytes=VMEM_LIMIT),
        name="qkv_step" if per_row else "qkv_seq",
    )(x, mod, ng, w, b, *tables)


def _sink_softmax(s, sink):
    m = jnp.maximum(jnp.max(s, axis=-1, keepdims=True), sink)
    p = jnp.exp(s - m)
    return p / (jnp.sum(p, axis=-1, keepdims=True) + jnp.exp(sink - m))


def _swa_seq_kernel(q_ref, kc_ref, kp_ref, vc_ref, vp_ref, sink_ref, o_ref):
    n = pl.program_id(1)
    q = q_ref[...]
    kall = jnp.concatenate([kp_ref[...], kc_ref[...]], axis=0)
    vall = jnp.concatenate([vp_ref[...], vc_ref[...]], axis=0)
    blk = ATT_BLOCK
    i = lax.broadcasted_iota(jnp.int32, (blk, 2 * blk), 0)
    j = lax.broadcasted_iota(jnp.int32, (blk, 2 * blk), 1)
    mask = (j >= i + (blk - WINDOW)) & (j <= i + blk) & ((n > 0) | (j >= blk))
    heads = range(KV_HEADS * GROUP)
    hsl = [slice(h * HEAD_DIM, (h + 1) * HEAD_DIM) for h in heads]
    kh = [_bf(kall[:, hsl[kvh]]) for kvh in range(KV_HEADS)]
    vh = [_bf(vall[:, hsl[kvh]]) for kvh in range(KV_HEADS)]
    qh = [_bf(q[:, hsl[h]]) for h in heads]
    s = [_dot_nt(qh[h], kh[h // GROUP]) * (HEAD_DIM ** -0.5) for h in heads]
    s = [jnp.where(mask, s[h], -jnp.inf) for h in heads]
    sink = [sink_ref[0:1, h:h + 1] for h in heads]
    m = [jnp.maximum(jnp.max(s[h], axis=-1, keepdims=True), sink[h]) for h in heads]
    p = [_bf(jnp.exp(s[h] - m[h])) for h in heads]
    ones = jnp.ones((2 * blk, LANE), BF16)
    den = [_dot(p[h], ones)[:, :HEAD_DIM] + jnp.exp(sink[h] - m[h]) for h in heads]
    o = [_dot(p[h], vh[h // GROUP]) / den[h] for h in heads]
    for h in heads:
        o_ref[:, hsl[h]] = o[h]


def _swa_seq(q, k, v, sink, batch, seq_len):
    t, d = q.shape
    nkv = k.shape[1]
    nb = seq_len // ATT_BLOCK
    cur = lambda b, n: (b * nb + n, 0)
    prv = lambda b, n: (b * nb + jnp.maximum(n - 1, 0), 0)
    return pl.pallas_call(
        _swa_seq_kernel,
        grid=(batch, nb),
        in_specs=[pl.BlockSpec((ATT_BLOCK, d), cur),
                  pl.BlockSpec((ATT_BLOCK, nkv), cur), pl.BlockSpec((ATT_BLOCK, nkv), prv),
                  pl.BlockSpec((ATT_BLOCK, nkv), cur), pl.BlockSpec((ATT_BLOCK, nkv), prv),
                  pl.BlockSpec(sink.shape, lambda b, n: (0, 0))],
        out_specs=pl.BlockSpec((ATT_BLOCK, d), cur),
        out_shape=jax.ShapeDtypeStruct((t, d), F32),
        compiler_params=pltpu.CompilerParams(dimension_semantics=("parallel", "parallel")),
        name="swa_seq",
    )(q, k, k, v, v, sink)


def _swa_step_kernel(q_ref, kn_ref, vn_ref, knt_ref, vnt_ref, kc_ref, vc_ref, sink_ref, o_ref, ko_ref, vo_ref):
    nb, _, w = kc_ref.shape
    kc = kc_ref[...]
    vc = vc_ref[...]
    kn = kn_ref[...]
    vn = vn_ref[...]
    q = q_ref[...]
    scale = HEAD_DIM ** -0.5
    kvs = range(KV_HEADS)
    ksl = [slice(h * HEAD_DIM, (h + 1) * HEAD_DIM) for h in kvs]
    gsl = [slice(h * GROUP, (h + 1) * GROUP) for h in kvs]
    qh = [_bf(q[:, gsl[h], :]) for h in kvs]
    s_old = [jnp.einsum("bgd,bdw->bgw", qh[h], _bf(kc[:, ksl[h], :]), preferred_element_type=F32) * scale
             for h in kvs]
    s_new = [jnp.sum(qh[h].astype(F32) * _bf(kn[:, :, ksl[h]]).astype(F32), axis=-1, keepdims=True) * scale
             for h in kvs]
    sink = [sink_ref[h] for h in kvs]
    m = [jnp.maximum(jnp.maximum(jnp.max(s_old[h], axis=-1, keepdims=True), s_new[h]), sink[h]) for h in kvs]
    p_old = [jnp.exp(s_old[h] - m[h]) for h in kvs]
    p_new = [jnp.exp(s_new[h] - m[h]) for h in kvs]
    den = [jnp.sum(p_old[h], axis=-1, keepdims=True) + p_new[h] + jnp.exp(sink[h] - m[h]) for h in kvs]
    o = [jnp.einsum("bgw,bdw->bgd", _bf(p_old[h] / den[h]), _bf(vc[:, ksl[h], :]), preferred_element_type=F32)
         for h in kvs]
    for h in kvs:
        o_ref[:, gsl[h], :] = o[h] + _bf(p_new[h] / den[h]).astype(F32) * _bf(vn[:, :, ksl[h]]).astype(F32)
    last = lax.broadcasted_iota(jnp.int32, kc.shape, 2) == w - 1

    def column(t_ref):
        t = t_ref[0]
        return jnp.stack([jnp.broadcast_to(t[:, b:b + 1], t.shape[:1] + (w,)) for b in range(nb)])

    ko_ref[...] = jnp.where(last, column(knt_ref), pltpu.roll(kc, w - 1, axis=2))
    vo_ref[...] = jnp.where(last, column(vnt_ref), pltpu.roll(vc, w - 1, axis=2))


def _swa_step(q, kn, vn, cache_k, cache_v, sink):
    b, nh, hd = q.shape
    nkv, w = cache_k.shape[1], cache_k.shape[2]
    nb = 8
    i3 = lambda i: (i, 0, 0)
    cspec = pl.BlockSpec((nb, nkv, w), i3)
    nspec = pl.BlockSpec((nb, 1, nkv), i3)
    tspec = pl.BlockSpec((1, nkv, nb), i3)
    qspec = pl.BlockSpec((nb, nh, hd), i3)
    cols = lambda z: z.reshape(b // nb, nb, nkv).transpose(0, 2, 1)
    return pl.pallas_call(
        _swa_step_kernel,
        grid=(b // nb,),
        in_specs=[qspec, nspec, nspec, tspec, tspec, cspec, cspec, pl.BlockSpec(sink.shape, lambda i: (0, 0, 0))],
        out_specs=[qspec, cspec, cspec],
        out_shape=[jax.ShapeDtypeStruct(q.shape, F32), jax.ShapeDtypeStruct(cache_k.shape, F32),
                   jax.ShapeDtypeStruct(cache_v.shape, F32)],
        compiler_params=pltpu.CompilerParams(dimension_semantics=("parallel",)),
        name="swa_step",
    )(q, kn.reshape(b, 1, nkv), vn.reshape(b, 1, nkv), cols(kn), cols(vn), cache_k, cache_v, sink)


def kernel(x_prompt, x_sample, c_prompt, c_sample, state_wkv, state_shift, cache_k, cache_v, norm1_g, norm2_g, ada_w, ada_b, mlp_up, mlp_down, final_g, rw_mix, rw_wr, rw_wk, rw_wv, rw_wo, rw_w0, rw_w1, rw_w2, rw_a0, rw_a1, rw_a2, rw_g1, rw_g2, rw_kk, rw_ka, rw_rk, rw_lnx_g, rw_lnx_b, at_wqkv, at_bqkv, at_wo, at_sink):
    bp, seq, d = x_prompt.shape
    bs = x_sample.shape[0]
    assert x_sample.shape[1] == 1 and d % GW == 0 and seq % (8 * CHUNK) == 0
    nh = d // HEAD_DIM
    tp = bp * seq
    row = lambda vec: vec.reshape(1, -1)

    n_c = bp + bs
    pad = (-n_c) % 8
    c_all = jnp.concatenate([c_prompt, c_sample, jnp.zeros((pad, d), F32)], axis=0)
    ada = _ada(c_all, ada_w, ada_b)
    mod_p = [ada[i, :bp].reshape(bp, N_ADA, d) for i in range(2)]
    mod_s = [ada[i, bp:bp + bs] for i in range(2)]

    xp = x_prompt.reshape(tp, d)
    xs = x_sample.reshape(bs, d)

    head_id = jnp.arange(GW) // HEAD_DIM
    ones_bd = (head_id[:, None] == head_id[None, :]).astype(BF16)

    rwp = dict(mix=rw_mix[0], wr=_bf(rw_wr[0]), wk=_bf(rw_wk[0]), wv=_bf(rw_wv[0]),
               w1=_bf(rw_w1[0]), w2=_bf(rw_w2[0]), w0=row(rw_w0[0]),
               a1=_bf(rw_a1[0]), a2=_bf(rw_a2[0]), a0=row(rw_a0[0]),
               g1=_bf(rw_g1[0]), g2=_bf(rw_g2[0]), kk=row(rw_kk[0]), ka=row(rw_ka[0]), ones=ones_bd)
    n1 = row(norm1_g[0])
    post_rw = (row(rw_lnx_g[0]), row(rw_lnx_b[0]), row(rw_rk[0]), ones_bd)
    wo0, up0, down0 = _bf(rw_wo[0]), _bf(mlp_up[0]), _bf(mlp_down[0])
    n2 = row(norm2_g[0])

    r, k, v, lw, kk, a, g, hlast = _rwkv_proj(xp, None, mod_p[0], n1, rwp, 256, seq)
    tiles = seq // 256
    shift_p = hlast.reshape(bp, tiles, 8, d)[:, -1, -1]
    y, st = _wkv_seq(r, k, v, lw, kk, a, bp, seq)
    wkv_p = st.reshape(bp, d // GW, HEAD_DIM, HPG, HEAD_DIM).transpose(0, 1, 3, 2, 4).reshape(bp, nh, HEAD_DIM, HEAD_DIM)
    (xp,) = _post(xp, y, (r, k, v, g), mod_p[0], False, post_rw, wo0, n2, up0, down0, None, 256, seq)

    rs, ks, vs, lws, kks, as_, gs, shift_s = _rwkv_proj(xs, state_shift[0], mod_s[0], n1, rwp, bs, 1)
    vecs = jnp.stack([rs, jnp.exp(lws), ks, vs, -kks, kks * as_]).transpose(0, 2, 1).reshape(6, nh, HEAD_DIM, bs)
    st_s, ys = _wkv_step(state_wkv[0].transpose(1, 2, 3, 0), *vecs)
    wkv_s = st_s.transpose(3, 0, 1, 2)
    (xs,) = _post(xs, ys.reshape(d, bs).T, (rs, ks, vs, gs), mod_s[0], True, post_rw, wo0, n2, up0, down0, None, bs, 1)

    wqkv, bqkv, wo1 = _bf(at_wqkv[0]), row(at_bqkv[0]), _bf(at_wo[0])
    n1, n2 = row(norm1_g[1]), row(norm2_g[1])
    up1, down1 = _bf(mlp_up[1]), _bf(mlp_down[1])
    fg = row(final_g)
    nkv = KV_HEADS * HEAD_DIM

    q, kq, vq = _qkv(xp, mod_p[1], False, n1, wqkv, bqkv, _rope_tables(seq, 0, 1), 256, seq)
    o = _swa_seq(q, kq, vq, row(at_sink[0]), bp, seq)
    keep = min(WINDOW, seq)
    k_p = kq.reshape(bp, seq, nkv)[:, seq - keep:].reshape(bp, keep, KV_HEADS, HEAD_DIM)
    v_p = vq.reshape(bp, seq, nkv)[:, seq - keep:].reshape(bp, keep, KV_HEADS, HEAD_DIM)
    xp, y_p = _post(xp, o, None, mod_p[1], False, None, wo1, n2, up1, down1, fg, 256, seq)

    qs, kn, vn = _qkv(xs, mod_s[1], True, n1, wqkv, bqkv, _rope_tables(bs, PAST_LEN, 0), bs, 1)
    w_buf = cache_k.shape[2]
    to_minor = lambda c: c.transpose(0, 2, 3, 1).reshape(bs, nkv, w_buf)
    from_minor = lambda c: c.reshape(bs, KV_HEADS, HEAD_DIM, w_buf).transpose(0, 3, 1, 2)
    os_, k_s, v_s = _swa_step(qs.reshape(bs, nh, HEAD_DIM), kn, vn, to_minor(cache_k[0]), to_minor(cache_v[0]),
                              at_sink[0].reshape(KV_HEADS, GROUP, 1))
    xs, y_s = _post(xs, os_.reshape(bs, d), None, mod_s[1], True, None, wo1, n2, up1, down1, fg, bs, 1)

    return (y_p.reshape(bp, seq, d), y_s.reshape(bs, 1, d),
            wkv_p[None], wkv_s[None], shift_p[None], shift_s[None],
            k_p[None], from_minor(k_s)[None], v_p[None], from_minor(v_s)[None])
```

```python
import functools
import math

import jax
import jax.numpy as jnp
from jax import lax
from jax.experimental import pallas as pl
from jax.experimental.pallas import tpu as pltpu

F32 = jnp.float32
BF16 = jnp.bfloat16

HEAD_DIM = 64
KV_HEADS = 4
GROUP = 4
WINDOW = 128
ATT_BLOCK = 128
ROPE_DIM = HEAD_DIM // 4
ROPE_THETA = 500000.0
PAST_LEN = 8192
NORM_EPS = 1e-6
LNX_EPS = 64e-5
N_ADA = 6

LANE = 128
V7X_VMEM_BYTES = 64 * 1024 * 1024
VMEM_LIMIT = V7X_VMEM_BYTES * 7 // 8

CHUNK = 64
HPG = 4
GW = HPG * HEAD_DIM


def _bf(x):
    return x.astype(BF16)


def _dot(a, b):
    return jnp.dot(a, b, preferred_element_type=F32)


def _dot_nt(a, b):
    return lax.dot_general(a, b, (((1,), (1,)), ((), ())), preferred_element_type=F32)


def _rms(x, g):
    ms = jnp.mean(x * x, axis=-1, keepdims=True)
    return x * lax.rsqrt(ms + NORM_EPS) * g


def _prenorm(x, g, shift, scale):
    return _rms(x, g) * (1.0 + scale) + shift


def _mod(mod_ref, per_row, idx, d):
    if per_row:
        return mod_ref[:, idx * d:(idx + 1) * d]
    return mod_ref[0, idx:idx + 1, :]


def _pieces(x, n):
    out = []
    rem = x
    for i in range(n):
        p = rem.astype(BF16)
        out.append(p)
        if i + 1 < n:
            rem = rem - p.astype(F32)
    return out


def _mm(xp, yp, dn):
    n = max(len(xp), len(yp))
    acc = None
    for i, xi in enumerate(xp):
        for j, yj in enumerate(yp):
            if i + j < n:
                t = lax.dot_general(xi, yj, dn, preferred_element_type=F32)
                acc = t if acc is None else acc + t
    return acc


_NN = (((1,), (0,)), ((), ()))
_NT = (((1,), (1,)), ((), ()))


def _head_sum(val, ones, n):
    gw = ones.shape[0]
    cols = []
    for j in range(val.shape[1] // gw):
        cols.append(_mm(_pieces(val[:, j * gw:(j + 1) * gw], n), [ones], _NN))
    return jnp.concatenate(cols, axis=1)


def _ada_kernel(c_ref, w_ref, b_ref, o_ref):
    c = c_ref[...]
    s = c * jax.nn.sigmoid(c)
    o_ref[0] = _dot(_bf(s), _bf(w_ref[0])) + b_ref[0]


def _ada(c_all, ada_w, ada_b):
    depth, d, n = ada_w.shape
    m = c_all.shape[0]
    tn = 512
    return pl.pallas_call(
        _ada_kernel,
        grid=(depth, n // tn),
        in_specs=[
            pl.BlockSpec((m, d), lambda i, j: (0, 0)),
            pl.BlockSpec((1, d, tn), lambda i, j: (i, 0, j)),
            pl.BlockSpec((1, 1, tn), lambda i, j: (i, 0, j)),
        ],
        out_specs=pl.BlockSpec((1, m, tn), lambda i, j: (i, 0, j)),
        out_shape=jax.ShapeDtypeStruct((depth, m, n), F32),
        compiler_params=pltpu.CompilerParams(dimension_semantics=("parallel", "parallel")),
        name="ada",
    )(c_all, ada_w, ada_b.reshape(depth, 1, n))


def _rwkv_proj_kernel(per_row, tiles_per_seq,
                      x_ref, xp_ref, mod_ref, ng_ref, mix_ref, wr_ref, wk_ref, wv_ref,
                      w1_ref, w2_ref, w0_ref, a1_ref, a2_ref, a0_ref, g1_ref, g2_ref,
                      kk_ref, ka_ref, ones_ref,
                      r_o, k_o, v_o, lw_o, kk_o, a_o, g_o, h_o, h_scr, xx_scr):
    d = x_ref.shape[1]
    tm = x_ref.shape[0]
    ng = ng_ref[...]
    shift = _mod(mod_ref, per_row, 0, d)
    scale = _mod(mod_ref, per_row, 1, d)
    h = _prenorm(x_ref[...], ng, shift, scale)
    if per_row:
        prev = xp_ref[...]
        h_o[...] = h
    else:
        hp = _prenorm(xp_ref[...], ng, shift, scale)
        first = (pl.program_id(0) % tiles_per_seq) == 0
        prow = jnp.where(first, 0.0, hp[7:8, :])
        rowid = lax.broadcasted_iota(jnp.int32, h.shape, 0)
        prev = jnp.where(rowid == 0, prow, pltpu.roll(h, 1, axis=0))
        h_o[0] = h[tm - 8:tm, :]
    h_scr[...] = h
    xx_scr[...] = prev - h

    sub = min(tm, 256)
    subs = [slice(i, i + sub) for i in range(0, tm, sub)]

    def mixed(j):
        return [_bf(h_scr[sl, :] + xx_scr[sl, :] * mix_ref[j:j + 1, :]) for sl in subs]

    def each(f, *cols):
        return [f(*vals) for vals in zip(*cols)]

    for sl, r in zip(subs, each(lambda m: _dot(m, wr_ref[...]), mixed(0))):
        r_o[sl, :] = r
    t1 = each(lambda m: _bf(jnp.tanh(_dot(m, w1_ref[...]))), mixed(1))
    for sl, wl in zip(subs, each(lambda t: w0_ref[...] + _dot(t, w2_ref[...]), t1)):
        lw_o[sl, :] = jax.nn.sigmoid(wl) * (-math.exp(-0.5))
    t4 = each(lambda m: _bf(_dot(m, a1_ref[...])), mixed(4))
    a = each(lambda t: jax.nn.sigmoid(a0_ref[...] + _dot(t, a2_ref[...])), t4)
    for sl, ai in zip(subs, a):
        a_o[sl, :] = ai
    k = each(lambda m: _dot(m, wk_ref[...]), mixed(2))
    kk = each(lambda ki: ki * kk_ref[...], k)
    ss = each(lambda kki: _head_sum(kki * kki, ones_ref[...], 1), kk)
    for sl, ki, ai, kki, ssi in zip(subs, k, a, kk, ss):
        k_o[sl, :] = ki * (1.0 + (ai - 1.0) * ka_ref[...])
        kk_o[sl, :] = kki * lax.rsqrt(jnp.maximum(ssi, 1e-24))
    for sl, v in zip(subs, each(lambda m: _dot(m, wv_ref[...]), mixed(3))):
        v_o[sl, :] = v
    t5 = each(lambda m: _bf(jax.nn.sigmoid(_dot(m, g1_ref[...]))), mixed(5))
    for sl, g in zip(subs, each(lambda t: _dot(t, g2_ref[...]), t5)):
        g_o[sl, :] = g


def _const_spec(shape):
    nd = len(shape)
    return pl.BlockSpec(shape, lambda *_: (0,) * nd, pipeline_mode=pl.Buffered(1))


def _rwkv_proj(x, prev_rows, mod, ng, p, tm, seq_len):
    t, d = x.shape
    per_row = prev_rows is not None
    nt = t // tm
    tiles_per_seq = 1 if per_row else seq_len // tm
    row = lambda i: (i, 0)
    if per_row:
        xp, xp_spec = prev_rows, pl.BlockSpec((tm, d), row)
        mod_spec = pl.BlockSpec((tm, N_ADA * d), row)
        h_shape, h_spec = jax.ShapeDtypeStruct((t, d), F32), pl.BlockSpec((tm, d), row)
    else:
        xp, xp_spec = x, pl.BlockSpec((8, d), lambda i: (jnp.maximum(i * (tm // 8) - 1, 0), 0))
        mod_spec = pl.BlockSpec((1, N_ADA, d), lambda i: (i // tiles_per_seq, 0, 0))
        h_shape, h_spec = jax.ShapeDtypeStruct((nt, 8, d), F32), pl.BlockSpec((1, 8, d), lambda i: (i, 0, 0))
    consts = [ng, p["mix"], p["wr"], p["wk"], p["wv"], p["w1"], p["w2"], p["w0"], p["a1"], p["a2"], p["a0"],
              p["g1"], p["g2"], p["kk"], p["ka"], p["ones"]]
    act = jax.ShapeDtypeStruct((t, d), F32)
    return pl.pallas_call(
        functools.partial(_rwkv_proj_kernel, per_row, tiles_per_seq),
        grid=(nt,),
        in_specs=[pl.BlockSpec((tm, d), row), xp_spec, mod_spec] + [_const_spec(c.shape) for c in consts],
        out_specs=[pl.BlockSpec((tm, d), row)] * 7 + [h_spec],
        out_shape=[act] * 7 + [h_shape],
        scratch_shapes=[pltpu.VMEM((tm, d), F32)] * 2,
        compiler_params=pltpu.CompilerParams(dimension_semantics=("parallel",), vmem_limit_bytes=VMEM_LIMIT),
        name="rwkv_proj_step" if per_row else "rwkv_proj_seq",
    )(x, xp, mod, *consts)


def _head_of_lane(shape, axis):
    return lax.broadcasted_iota(jnp.int32, shape, axis) // HEAD_DIM


def _bd(p, low_half):
    zero = jnp.zeros((), p.dtype)
    zeros = jnp.zeros((p.shape[0], LANE), p.dtype)
    per_col = LANE // HEAD_DIM
    ncol = GW // LANE
    blocks = []
    for h in range(HPG):
        c = h // per_col
        col = p[:, c * LANE:(c + 1) * LANE]
        col = jnp.where(low_half, col, zero) if h % per_col == 0 else jnp.where(low_half, zero, col)
        blocks.append(jnp.concatenate([col if j == c else zeros for j in range(ncol)], axis=1))
    return jnp.concatenate(blocks, axis=0)


def _wkv_prep_chunks(rs, ks, vs, lws, kks, as_, masks):
    tri, diag, strict, incl, eye, blk8, levels, lane_head = masks
    n = len(rs)

    def each(f, *args):
        return [f(*[a[i] for a in args]) for i in range(n)]

    def pp(xs, ys):
        rhs = each(lambda y: _bd(_bf(y), diag), ys)
        return each(lambda x, b: _dot(_bf(x), b), xs, rhs)

    def pabt(xs, ys):
        rhs = each(lambda y: _bd(_bf(y), diag), ys)
        return each(lambda x, b: _dot_nt(_bf(x), b), xs, rhs)

    def pick_diag(gram):
        out = gram[0:HEAD_DIM, :]
        for hh in range(1, HPG):
            out = jnp.where(lane_head == hh, gram[hh * HEAD_DIM:(hh + 1) * HEAD_DIM, :], out)
        return out

    def patb(xs, ys):
        lhs = each(lambda x: _bf(x.T), xs)
        return each(lambda xt, y: pick_diag(_dot(xt, _bf(y))), lhs, ys)

    def add(xs, ys):
        return each(lambda x, y: x + y, xs, ys)

    def masked(m, xs):
        return each(lambda x: jnp.where(m, x, 0.0), xs)

    c = each(lambda lw: _mm([tri], _pieces(lw, 2), _NN), lws)
    c_last = each(lambda ci: ci[CHUNK - 1:CHUNK, :], c)
    e_neg = each(lambda ci: jnp.exp(-ci), c)
    e_end = each(lambda ci, cl: jnp.exp(cl - ci), c, c_last)
    bv = each(lambda kk, a: kk * a, kks, as_)
    a_t = each(lambda kk, ci, lw: _bf(-kk * jnp.exp(ci - lw)), kks, c, lws)
    r_t = each(lambda r, ci: r * jnp.exp(ci), rs, c)
    b_t = each(lambda b, e: _bf(b * e), bv, e_neg)
    k_t = each(lambda k, e: _bf(k * e), ks, e_neg)
    b_e = each(lambda b, e: _bf(b * e), bv, e_end)
    k_e = each(lambda k, e: _bf(k * e), ks, e_end)

    ar = each(lambda x, y: jnp.concatenate([x, _bf(y)], axis=0), a_t, r_t)
    xb = pabt(ar, b_t)
    xk = pabt(ar, k_t)
    a_ab = each(lambda x: jnp.where(strict, x[:CHUNK], 0.0), xb)
    a_rb = each(lambda x: jnp.where(incl, x[CHUNK:], 0.0), xb)
    a_ak = each(lambda x: jnp.where(strict, x[:CHUNK], 0.0), xk)
    a_rk = each(lambda x: jnp.where(incl, x[CHUNK:], 0.0), xk)

    n0 = masked(blk8, a_ab)
    n2 = pp(n0, n0)
    n4 = pp(n2, n2)
    plus_eye = lambda xs: each(lambda x: eye + x, xs)
    tinv = pp(pp(plus_eye(n0), plus_eye(n2)), plus_eye(n4))
    b = 8
    for lvl in levels:
        lower = lambda x: jnp.concatenate([x[r0:r0 + b] for r0 in range(b, CHUNK, 2 * b)], axis=0)
        z = pp(pp(each(lower, tinv), masked(lvl, a_ab)), tinv)

        def merged(t, zi):
            rows = []
            for j, r0 in enumerate(range(0, CHUNK, 2 * b)):
                rows += [t[r0:r0 + b], t[r0 + b:r0 + 2 * b] + zi[j * b:(j + 1) * b]]
            return jnp.concatenate(rows, axis=0)

        tinv = each(merged, tinv, z)
        b *= 2

    w1 = pp(tinv, a_t)
    w2 = pp(tinv, pp(a_ak, vs))
    qc = add(r_t, pp(a_rb, w1))
    y0 = add(pp(a_rb, w2), pp(a_rk, vs))
    mc = patb(w1, b_e)
    stack = lambda x, y: jnp.concatenate([x, y], axis=0)
    cc = patb(each(stack, w2, vs), each(stack, b_e, k_e))
    return qc, y0, mc, cc, c_last


def _wkv_masks():
    t = lax.broadcasted_iota(jnp.int32, (CHUNK, GW), 0)
    s = lax.broadcasted_iota(jnp.int32, (CHUNK, GW), 1) % HEAD_DIM
    tri = (lax.broadcasted_iota(jnp.int32, (CHUNK, CHUNK), 1)
           <= lax.broadcasted_iota(jnp.int32, (CHUNK, CHUNK), 0)).astype(BF16)
    diag = lax.broadcasted_iota(jnp.int32, (CHUNK, LANE), 1) < HEAD_DIM
    strict = s < t
    incl = s <= t
    eye = (s == t).astype(F32)
    blk8 = strict & ((s // 8) == (t // 8))
    levels = []
    b = 8
    while b < CHUNK:
        levels.append(((s // (2 * b)) == (t // (2 * b))) & ((s // b) != (t // b)) & strict)
        b *= 2
    lane_head = _head_of_lane((HEAD_DIM, GW), 1)
    return tri, diag, strict, incl, eye, blk8, levels, lane_head


def _rwkv_epilogue(y, r, k, v, g, lg, lb, rk, ones):
    inv_n = 1.0 / HEAD_DIM
    mu = _head_sum(y, ones, 1) * inv_n
    dy = y - mu
    var = _head_sum(dy * dy, ones, 1) * inv_n
    yn = dy * lax.rsqrt(var + LNX_EPS) * lg + lb
    bonus = _head_sum(r * k * rk, ones, 1) * v
    return (yn + bonus) * g


def _wkv_seq_kernel(nsub, ng, r_ref, k_ref, v_ref, lw_ref, kk_ref, a_ref, y_o, s_o, s_scr):
    step = pl.program_id(1)

    @pl.when(step == 0)
    def _():
        s_scr[...] = jnp.zeros_like(s_scr)

    masks = _wkv_masks()
    diag = masks[1]
    sls = [slice(u * CHUNK, (u + 1) * CHUNK) for u in range(nsub)]
    gls = [slice(g * GW, (g + 1) * GW) for g in range(ng)]
    items = [(u, g) for u in range(nsub) for g in range(ng)]
    chunks = lambda ref: [ref[sls[u], gls[g]] for u, g in items]
    qc, y0, mc, cc, cl = _wkv_prep_chunks(chunks(r_ref), chunks(k_ref), chunks(v_ref), chunks(lw_ref),
                                          chunks(kk_ref), chunks(a_ref), masks)

    states = [s_scr[:, gl] for gl in gls]
    for u in range(nsub):
        sb = [_bf(s) for s in states]
        upd = [_dot(sb[g], _bd(_bf(mc[u * ng + g]), diag)) for g in range(ng)]
        yy = [_dot_nt(_bf(qc[u * ng + g]), _bd(sb[g], diag)) for g in range(ng)]
        for g in range(ng):
            i = u * ng + g
            states[g] = states[g] * jnp.exp(cl[i]) + upd[g] + cc[i]
            y_o[sls[u], gls[g]] = yy[g] + y0[i]
    for g, gl in enumerate(gls):
        s_scr[:, gl] = states[g]

    @pl.when(step == pl.num_programs(1) - 1)
    def _():
        for g in range(ng):
            s_o[0, g] = states[g]


def _wkv_seq(r, k, v, lw, kk, a, batch, seq_len):
    t, d = r.shape
    nsub = 4
    rows = nsub * CHUNK
    ng = d // GW
    steps = seq_len // rows
    blk = pl.BlockSpec((rows, d), lambda i, j: (i * steps + j, 0))
    return pl.pallas_call(
        functools.partial(_wkv_seq_kernel, nsub, ng),
        grid=(batch, steps),
        in_specs=[blk] * 6,
        out_specs=[blk, pl.BlockSpec((1, ng, HEAD_DIM, GW), lambda i, j: (i, 0, 0, 0))],
        out_shape=[jax.ShapeDtypeStruct((t, d), F32), jax.ShapeDtypeStruct((batch, ng, HEAD_DIM, GW), F32)],
        scratch_shapes=[pltpu.VMEM((HEAD_DIM, d), F32)],
        compiler_params=pltpu.CompilerParams(dimension_semantics=("parallel", "arbitrary"),
                                             vmem_limit_bytes=VMEM_LIMIT),
        name="wkv_seq",
    )(r, k, v, lw, kk, a)


def _wkv_step_kernel(s_ref, r_ref, w_ref, k_ref, v_ref, a_ref, b_ref, so_ref, y_ref):
    s = s_ref[0]
    sa = jnp.sum(s * a_ref[0][None], axis=1)
    s = s * w_ref[0][None] + sa[:, None, :] * b_ref[0][None] + v_ref[0][:, None, :] * k_ref[0][None]
    so_ref[0] = s
    y_ref[0] = jnp.sum(s * r_ref[0][None], axis=1)


def _wkv_step(state, r, decay, k, v, a, b):
    nh, n, _, nb = state.shape
    sspec = pl.BlockSpec((1, n, n, nb), lambda i: (i, 0, 0, 0))
    vspec = pl.BlockSpec((1, n, nb), lambda i: (i, 0, 0))
    return pl.pallas_call(
        _wkv_step_kernel,
        grid=(nh,),
        in_specs=[sspec] + [vspec] * 6,
        out_specs=[sspec, vspec],
        out_shape=[jax.ShapeDtypeStruct(state.shape, F32), jax.ShapeDtypeStruct((nh, n, nb), F32)],
        compiler_params=pltpu.CompilerParams(dimension_semantics=("parallel",)),
        name="wkv_step",
    )(state, r, decay, k, v, a, b)


def _post_kernel(rwkv, final, per_row, *refs):
    refs = list(refs)
    x_ref = refs.pop(0)
    z_ref = refs.pop(0)
    if rwkv:
        r_ref, k_ref, v_ref, g_ref = refs[:4]
        refs = refs[4:]
    mod_ref = refs.pop(0)
    if rwkv:
        lg_ref, lb_ref, rk_ref, ones_ref = refs[:4]
        refs = refs[4:]
    wo_ref, n2_ref, up_ref, down_ref = refs[:4]
    refs = refs[4:]
    if final:
        fg_ref = refs.pop(0)
    x_o = refs.pop(0)
    d = x_ref.shape[1]

    z = z_ref[...]
    if rwkv:
        z = _rwkv_epilogue(z, r_ref[...], k_ref[...], v_ref[...], g_ref[...],
                           lg_ref[...], lb_ref[...], rk_ref[...], ones_ref[...])
    out = _dot(_bf(z), wo_ref[...])
    x1 = x_ref[...] + _mod(mod_ref, per_row, 2, d) * out
    h2 = _bf(_prenorm(x1, n2_ref[...], _mod(mod_ref, per_row, 3, d), _mod(mod_ref, per_row, 4, d)))
    dff = up_ref.shape[1]
    fc = 1024
    acc = None
    for c in range(dff // fc):
        hid = jnp.maximum(_dot(h2, up_ref[:, c * fc:(c + 1) * fc]), 0.0)
        part = _dot(_bf(hid * hid), down_ref[c * fc:(c + 1) * fc, :])
        acc = part if acc is None else acc + part
    x2 = x1 + _mod(mod_ref, per_row, 5, d) * acc
    x_o[...] = x2
    if final:
        refs[0][...] = _rms(x2, fg_ref[...])


def _post(x, z, extra, mod, per_row, rw, wo, n2g, up, down, final_g, tm, seq_len):
    t, d = x.shape
    rwkv = rw is not None
    final = final_g is not None
    row = lambda i: (i, 0)
    tile = pl.BlockSpec((tm, d), row)
    tiles_per_seq = 1 if per_row else seq_len // tm
    mod_spec = (pl.BlockSpec((tm, N_ADA * d), row) if per_row
                else pl.BlockSpec((1, N_ADA, d), lambda i: (i // tiles_per_seq, 0, 0)))
    args = [x, z]
    specs = [tile, tile]
    if rwkv:
        args += list(extra)
        specs += [tile] * 4
    args.append(mod)
    specs.append(mod_spec)
    consts = (list(rw) if rwkv else []) + [wo, n2g, up, down] + ([final_g] if final else [])
    args += consts
    specs += [_const_spec(c.shape) for c in consts]
    act = jax.ShapeDtypeStruct((t, d), F32)
    outs = pl.pallas_call(
        functools.partial(_post_kernel, rwkv, final, per_row),
        grid=(t // tm,),
        in_specs=specs,
        out_specs=[tile] * (2 if final else 1),
        out_shape=[act] * (2 if final else 1),
        compiler_params=pltpu.CompilerParams(dimension_semantics=("parallel",), vmem_limit_bytes=VMEM_LIMIT),
        name=("post_rwkv" if rwkv else "post") + ("_final" if final else "") + ("_step" if per_row else "_seq"),
    )(*args)
    return outs


def _rope_table_kernel(base, step, invf_ref, c_o, sm_o, sp_o):
    shape = c_o.shape
    row = lax.broadcasted_iota(jnp.int32, shape, 0) + pl.program_id(0) * shape[0]
    pos = (base + step * row).astype(F32)
    ang = pos * invf_ref[...]
    lane = lax.broadcasted_iota(jnp.int32, shape, 1) % HEAD_DIM
    cos = jnp.cos(ang)
    sin = jnp.sin(ang)
    half = ROPE_DIM // 2
    c_o[...] = jnp.where(lane < ROPE_DIM, cos, 1.0)
    sm_o[...] = jnp.where(lane < half, -sin, 0.0)
    sp_o[...] = jnp.where((lane >= half) & (lane < ROPE_DIM), sin, 0.0)


def _rope_tables(n, base, step):
    half = ROPE_DIM // 2
    inv_freq = ROPE_THETA ** (-jnp.arange(half, dtype=F32) / half)
    lane = jnp.arange(LANE) % HEAD_DIM
    invf = jnp.where(lane < ROPE_DIM, inv_freq[lane % half], 0.0).reshape(1, LANE).astype(F32)
    tr = min(n, 512)
    spec = pl.BlockSpec((tr, LANE), lambda i: (i, 0))
    shp = jax.ShapeDtypeStruct((n, LANE), F32)
    return pl.pallas_call(
        functools.partial(_rope_table_kernel, base, step),
        grid=(n // tr,),
        in_specs=[pl.BlockSpec((1, LANE), lambda i: (0, 0))],
        out_specs=[spec] * 3,
        out_shape=[shp] * 3,
        compiler_params=pltpu.CompilerParams(dimension_semantics=("parallel",)),
        name="rope_tables",
    )(invf)


def _qkv_kernel(per_row, x_ref, mod_ref, ng_ref, w_ref, b_ref, c_ref, sm_ref, sp_ref, q_o, k_o, v_o):
    tm, d = x_ref.shape
    nq, nk = q_o.shape[1], k_o.shape[1]
    sub = min(tm, 256)
    subs = [slice(i, i + sub) for i in range(0, tm, sub)]
    shift, scale = _mod(mod_ref, per_row, 0, d), _mod(mod_ref, per_row, 1, d)
    rows = lambda m, sl: m[sl, :] if per_row else m
    h = [_bf(_prenorm(x_ref[sl, :], ng_ref[...], rows(shift, sl), rows(scale, sl))) for sl in subs]
    qkv = [_dot(hh, w_ref[...]) + b_ref[...] for hh in h]
    for sl, z in zip(subs, qkv):
        c, sm, sp = c_ref[sl, :], sm_ref[sl, :], sp_ref[sl, :]

        def rope_into(o_ref, off, width):
            for j in range(width // LANE):
                zc = z[:, off + j * LANE:off + (j + 1) * LANE]
                o_ref[sl, j * LANE:(j + 1) * LANE] = (zc * c + pltpu.roll(zc, LANE - ROPE_DIM // 2, axis=1) * sm
                                                      + pltpu.roll(zc, ROPE_DIM // 2, axis=1) * sp)

        rope_into(q_o, 0, nq)
        rope_into(k_o, nq, nk)
        v_o[sl, :] = z[:, nq + nk:]


def _qkv(x, mod, per_row, ng, w, b, tables, tm, seq_len):
    t, d = x.shape
    nkv = KV_HEADS * HEAD_DIM
    row = lambda i: (i, 0)
    tiles_per_seq = 1 if per_row else seq_len // tm
    mod_spec = (pl.BlockSpec((tm, N_ADA * d), row) if per_row
                else pl.BlockSpec((1, N_ADA, d), lambda i: (i // tiles_per_seq, 0, 0)))
    tab_spec = pl.BlockSpec((tm, LANE), lambda i: (i % tiles_per_seq, 0))
    return pl.pallas_call(
        functools.partial(_qkv_kernel, per_row),
        grid=(t // tm,),
        in_specs=[pl.BlockSpec((tm, d), row), mod_spec, _const_spec(ng.shape), _const_spec(w.shape),
                  _const_spec(b.shape)] + [tab_spec] * 3,
        out_specs=[pl.BlockSpec((tm, d), row), pl.BlockSpec((tm, nkv), row), pl.BlockSpec((tm, nkv), row)],
        out_shape=[jax.ShapeDtypeStruct((t, d), F32), jax.ShapeDtypeStruct((t, nkv), F32),
                   jax.ShapeDtypeStruct((t, nkv), F32)],
        compiler_params=pltpu.CompilerParams(dimension_semantics=("parallel",), vmem_limit_bytes=VMEM_LIMIT),
        name="qkv_step" if per_row else "qkv_seq",
    )(x, mod, ng, w, b, *tables)


def _sink_softmax(s, sink):
    m = jnp.maximum(jnp.max(s, axis=-1, keepdims=True), sink)
    p = jnp.exp(s - m)
    return p / (jnp.sum(p, axis=-1, keepdims=True) + jnp.exp(sink - m))


def _swa_seq_kernel(q_ref, kc_ref, kp_ref, vc_ref, vp_ref, sink_ref, o_ref):
    n = pl.program_id(1)
    q = q_ref[...]
    kall = jnp.concatenate([kp_ref[...], kc_ref[...]], axis=0)
    vall = jnp.concatenate([vp_ref[...], vc_ref[...]], axis=0)
    blk = ATT_BLOCK
    j = lax.broadcasted_iota(jnp.int32, (2 * blk, GROUP * blk), 0)
    i = lax.broadcasted_iota(jnp.int32, (2 * blk, GROUP * blk), 1) % blk
    mask = (j >= i + (blk - WINDOW)) & (j <= i + blk) & ((n > 0) | (j >= blk))
    kvs = range(KV_HEADS)
    hsl = [slice(h * HEAD_DIM, (h + 1) * HEAD_DIM) for h in range(KV_HEADS * GROUP)]
    eye_d = (lax.broadcasted_iota(jnp.int32, (HEAD_DIM, HEAD_DIM), 0)
             == lax.broadcasted_iota(jnp.int32, (HEAD_DIM, HEAD_DIM), 1)).astype(BF16)
    eye_q = (lax.broadcasted_iota(jnp.int32, (blk, blk), 0)
             == lax.broadcasted_iota(jnp.int32, (blk, blk), 1)).astype(BF16)
    ones = jnp.ones((16, 2 * blk), BF16)
    scale = HEAD_DIM ** -0.5
    kh = [_bf(kall[:, hsl[c]]) for c in kvs]
    vte = [jnp.concatenate([_bf(_dot_nt(eye_d, _bf(vall[:, hsl[c]]))), ones], axis=0) for c in kvs]
    qg = [_bf(jnp.concatenate([q[:, hsl[c * GROUP + g]] for g in range(GROUP)], axis=0) * scale) for c in kvs]
    s = [jnp.where(mask, _dot_nt(kh[c], qg[c]), -jnp.inf) for c in kvs]
    sink = [jnp.concatenate([jnp.broadcast_to(sink_ref[0:1, h:h + 1], (1, blk))
                             for h in range(c * GROUP, (c + 1) * GROUP)], axis=1) for c in kvs]
    m = [jnp.maximum(jnp.max(s[c], axis=0, keepdims=True), sink[c]) for c in kvs]
    p = [_bf(jnp.exp(s[c] - m[c])) for c in kvs]
    oe = [_dot(vte[c], p[c]) for c in kvs]
    ot = [_bf(oe[c][:HEAD_DIM] / (oe[c][HEAD_DIM:HEAD_DIM + 1] + jnp.exp(sink[c] - m[c]))) for c in kvs]
    for c in kvs:
        for pair in range(GROUP // 2):
            both = jnp.concatenate([ot[c][:, (2 * pair + e) * blk:(2 * pair + e + 1) * blk] for e in range(2)],
                                   axis=0)
            col = (c * GROUP + 2 * pair) * HEAD_DIM
            o_ref[:, col:col + 2 * HEAD_DIM] = _dot_nt(eye_q, both).astype(o_ref.dtype)


def _swa_seq(q, k, v, sink, batch, seq_len):
    t, d = q.shape
    nkv = k.shape[1]
    nb = seq_len // ATT_BLOCK
    cur = lambda b, n: (b * nb + n, 0)
    prv = lambda b, n: (b * nb + jnp.maximum(n - 1, 0), 0)
    return pl.pallas_call(
        _swa_seq_kernel,
        grid=(batch, nb),
        in_specs=[pl.BlockSpec((ATT_BLOCK, d), cur),
                  pl.BlockSpec((ATT_BLOCK, nkv), cur), pl.BlockSpec((ATT_BLOCK, nkv), prv),
                  pl.BlockSpec((ATT_BLOCK, nkv), cur), pl.BlockSpec((ATT_BLOCK, nkv), prv),
                  pl.BlockSpec(sink.shape, lambda b, n: (0, 0))],
        out_specs=pl.BlockSpec((ATT_BLOCK, d), cur),
        out_shape=jax.ShapeDtypeStruct((t, d), BF16),
        compiler_params=pltpu.CompilerParams(dimension_semantics=("parallel", "parallel")),
        name="swa_seq",
    )(q, k, k, v, v, sink)


def _swa_step_kernel(q_ref, kn_ref, vn_ref, knt_ref, vnt_ref, kc_ref, vc_ref, sink_ref, o_ref, ko_ref, vo_ref):
    nb, _, w = kc_ref.shape
    kc = kc_ref[...]
    vc = vc_ref[...]
    kn = kn_ref[...]
    vn = vn_ref[...]
    q = q_ref[...]
    scale = HEAD_DIM ** -0.5
    kvs = range(KV_HEADS)
    ksl = [slice(h * HEAD_DIM, (h + 1) * HEAD_DIM) for h in kvs]
    gsl = [slice(h * GROUP, (h + 1) * GROUP) for h in kvs]
    qh = [_bf(q[:, gsl[h], :]) for h in kvs]
    s_old = [jnp.einsum("bgd,bdw->bgw", qh[h], _bf(kc[:, ksl[h], :]), preferred_element_type=F32) * scale
             for h in kvs]
    s_new = [jnp.sum(qh[h].astype(F32) * _bf(kn[:, :, ksl[h]]).astype(F32), axis=-1, keepdims=True) * scale
             for h in kvs]
    sink = [sink_ref[h] for h in kvs]
    m = [jnp.maximum(jnp.maximum(jnp.max(s_old[h], axis=-1, keepdims=True), s_new[h]), sink[h]) for h in kvs]
    p_old = [jnp.exp(s_old[h] - m[h]) for h in kvs]
    p_new = [jnp.exp(s_new[h] - m[h]) for h in kvs]
    den = [jnp.sum(p_old[h], axis=-1, keepdims=True) + p_new[h] + jnp.exp(sink[h] - m[h]) for h in kvs]
    o = [jnp.einsum("bgw,bdw->bgd", _bf(p_old[h] / den[h]), _bf(vc[:, ksl[h], :]), preferred_element_type=F32)
         for h in kvs]
    for h in kvs:
        o_ref[:, gsl[h], :] = o[h] + _bf(p_new[h] / den[h]).astype(F32) * _bf(vn[:, :, ksl[h]]).astype(F32)
    last = lax.broadcasted_iota(jnp.int32, kc.shape, 2) == w - 1

    def column(t_ref):
        t = t_ref[0]
        return jnp.stack([jnp.broadcast_to(t[:, b:b + 1], t.shape[:1] + (w,)) for b in range(nb)])

    ko_ref[...] = jnp.where(last, column(knt_ref), pltpu.roll(kc, w - 1, axis=2))
    vo_ref[...] = jnp.where(last, column(vnt_ref), pltpu.roll(vc, w - 1, axis=2))


def _swa_step(q, kn, vn, cache_k, cache_v, sink):
    b, nh, hd = q.shape
    nkv, w = cache_k.shape[1], cache_k.shape[2]
    nb = 8
    i3 = lambda i: (i, 0, 0)
    cspec = pl.BlockSpec((nb, nkv, w), i3)
    nspec = pl.BlockSpec((nb, 1, nkv), i3)
    tspec = pl.BlockSpec((1, nkv, nb), i3)
    qspec = pl.BlockSpec((nb, nh, hd), i3)
    cols = lambda z: z.reshape(b // nb, nb, nkv).transpose(0, 2, 1)
    return pl.pallas_call(
        _swa_step_kernel,
        grid=(b // nb,),
        in_specs=[qspec, nspec, nspec, tspec, tspec, cspec, cspec, pl.BlockSpec(sink.shape, lambda i: (0, 0, 0))],
        out_specs=[qspec, cspec, cspec],
        out_shape=[jax.ShapeDtypeStruct(q.shape, F32), jax.ShapeDtypeStruct(cache_k.shape, F32),
                   jax.ShapeDtypeStruct(cache_v.shape, F32)],
        compiler_params=pltpu.CompilerParams(dimension_semantics=("parallel",)),
        name="swa_step",
    )(q, kn.reshape(b, 1, nkv), vn.reshape(b, 1, nkv), cols(kn), cols(vn), cache_k, cache_v, sink)


def kernel(x_prompt, x_sample, c_prompt, c_sample, state_wkv, state_shift, cache_k, cache_v, norm1_g, norm2_g, ada_w, ada_b, mlp_up, mlp_down, final_g, rw_mix, rw_wr, rw_wk, rw_wv, rw_wo, rw_w0, rw_w1, rw_w2, rw_a0, rw_a1, rw_a2, rw_g1, rw_g2, rw_kk, rw_ka, rw_rk, rw_lnx_g, rw_lnx_b, at_wqkv, at_bqkv, at_wo, at_sink):
    bp, seq, d = x_prompt.shape
    bs = x_sample.shape[0]
    assert x_sample.shape[1] == 1 and d % GW == 0 and seq % (8 * CHUNK) == 0
    nh = d // HEAD_DIM
    tp = bp * seq
    row = lambda vec: vec.reshape(1, -1)

    n_c = bp + bs
    pad = (-n_c) % 8
    c_all = jnp.concatenate([c_prompt, c_sample, jnp.zeros((pad, d), F32)], axis=0)
    ada = _ada(c_all, ada_w, ada_b)
    mod_p = [ada[i, :bp].reshape(bp, N_ADA, d) for i in range(2)]
    mod_s = [ada[i, bp:bp + bs] for i in range(2)]

    xp = x_prompt.reshape(tp, d)
    xs = x_sample.reshape(bs, d)

    head_id = jnp.arange(GW) // HEAD_DIM
    ones_bd = (head_id[:, None] == head_id[None, :]).astype(BF16)

    rwp = dict(mix=rw_mix[0], wr=_bf(rw_wr[0]), wk=_bf(rw_wk[0]), wv=_bf(rw_wv[0]),
               w1=_bf(rw_w1[0]), w2=_bf(rw_w2[0]), w0=row(rw_w0[0]),
               a1=_bf(rw_a1[0]), a2=_bf(rw_a2[0]), a0=row(rw_a0[0]),
               g1=_bf(rw_g1[0]), g2=_bf(rw_g2[0]), kk=row(rw_kk[0]), ka=row(rw_ka[0]), ones=ones_bd)
    n1 = row(norm1_g[0])
    post_rw = (row(rw_lnx_g[0]), row(rw_lnx_b[0]), row(rw_rk[0]), ones_bd)
    wo0, up0, down0 = _bf(rw_wo[0]), _bf(mlp_up[0]), _bf(mlp_down[0])
    n2 = row(norm2_g[0])

    r, k, v, lw, kk, a, g, hlast = _rwkv_proj(xp, None, mod_p[0], n1, rwp, 512, seq)
    tiles = seq // 512
    shift_p = hlast.reshape(bp, tiles, 8, d)[:, -1, -1]
    y, st = _wkv_seq(r, k, v, lw, kk, a, bp, seq)
    wkv_p = st.reshape(bp, d // GW, HEAD_DIM, HPG, HEAD_DIM).transpose(0, 1, 3, 2, 4).reshape(bp, nh, HEAD_DIM, HEAD_DIM)
    (xp,) = _post(xp, y, (r, k, v, g), mod_p[0], False, post_rw, wo0, n2, up0, down0, None, 256, seq)

    rs, ks, vs, lws, kks, as_, gs, shift_s = _rwkv_proj(xs, state_shift[0], mod_s[0], n1, rwp, bs, 1)
    vecs = jnp.stack([rs, jnp.exp(lws), ks, vs, -kks, kks * as_]).transpose(0, 2, 1).reshape(6, nh, HEAD_DIM, bs)
    st_s, ys = _wkv_step(state_wkv[0].transpose(1, 2, 3, 0), *vecs)
    wkv_s = st_s.transpose(3, 0, 1, 2)
    (xs,) = _post(xs, ys.reshape(d, bs).T, (rs, ks, vs, gs), mod_s[0], True, post_rw, wo0, n2, up0, down0, None, bs, 1)

    wqkv, bqkv, wo1 = _bf(at_wqkv[0]), row(at_bqkv[0]), _bf(at_wo[0])
    n1, n2 = row(norm1_g[1]), row(norm2_g[1])
    up1, down1 = _bf(mlp_up[1]), _bf(mlp_down[1])
    fg = row(final_g)
    nkv = KV_HEADS * HEAD_DIM

    q, kq, vq = _qkv(xp, mod_p[1], False, n1, wqkv, bqkv, _rope_tables(seq, 0, 1), 512, seq)
    o = _swa_seq(q, kq, vq, row(at_sink[0]), bp, seq)
    keep = min(WINDOW, seq)
    k_p = kq.reshape(bp, seq, nkv)[:, seq - keep:].reshape(bp, keep, KV_HEADS, HEAD_DIM)
    v_p = vq.reshape(bp, seq, nkv)[:, seq - keep:].reshape(bp, keep, KV_HEADS, HEAD_DIM)
    xp, y_p = _post(xp, o, None, mod_p[1], False, None, wo1, n2, up1, down1, fg, 256, seq)

    qs, kn, vn = _qkv(xs, mod_s[1], True, n1, wqkv, bqkv, _rope_tables(bs, PAST_LEN, 0), bs, 1)
    w_buf = cache_k.shape[2]
    to_minor = lambda c: c.transpose(0, 2, 3, 1).reshape(bs, nkv, w_buf)
    from_minor = lambda c: c.reshape(bs, KV_HEADS, HEAD_DIM, w_buf).transpose(0, 3, 1, 2)
    os_, k_s, v_s = _swa_step(qs.reshape(bs, nh, HEAD_DIM), kn, vn, to_minor(cache_k[0]), to_minor(cache_v[0]),
                              at_sink[0].reshape(KV_HEADS, GROUP, 1))
    xs, y_s = _post(xs, os_.reshape(bs, d), None, mod_s[1], True, None, wo1, n2, up1, down1, fg, bs, 1)

    return (y_p.reshape(bp, seq, d), y_s.reshape(bs, 1, d),
            wkv_p[None], wkv_s[None], shift_p[None], shift_s[None],
            k_p[None], from_minor(k_s)[None], v_p[None], from_minor(v_s)[None])
```

```python
import functools
import math

import jax
import jax.numpy as jnp
from jax import lax
from jax.experimental import pallas as pl
from jax.experimental.pallas import tpu as pltpu

F32 = jnp.float32
BF16 = jnp.bfloat16

HEAD_DIM = 64
KV_HEADS = 4
GROUP = 4
WINDOW = 128
ATT_BLOCK = 128
ROPE_DIM = HEAD_DIM // 4
ROPE_THETA = 500000.0
PAST_LEN = 8192
NORM_EPS = 1e-6
LNX_EPS = 64e-5
N_ADA = 6

LANE = 128
V7X_VMEM_BYTES = 64 * 1024 * 1024
VMEM_LIMIT = V7X_VMEM_BYTES * 7 // 8

CHUNK = 64
HPG = 4
GW = HPG * HEAD_DIM


def _bf(x):
    return x.astype(BF16)


def _dot(a, b):
    return jnp.dot(a, b, preferred_element_type=F32)


def _dot_nt(a, b):
    return lax.dot_general(a, b, (((1,), (1,)), ((), ())), preferred_element_type=F32)


def _rms(x, g):
    ms = jnp.mean(x * x, axis=-1, keepdims=True)
    return x * lax.rsqrt(ms + NORM_EPS) * g


def _prenorm(x, g, shift, scale):
    return _rms(x, g) * (1.0 + scale) + shift


def _mod(mod_ref, per_row, idx, d):
    if per_row:
        return mod_ref[:, idx * d:(idx + 1) * d]
    return mod_ref[0, idx:idx + 1, :]


def _pieces(x, n):
    out = []
    rem = x
    for i in range(n):
        p = rem.astype(BF16)
        out.append(p)
        if i + 1 < n:
            rem = rem - p.astype(F32)
    return out


def _mm(xp, yp, dn):
    n = max(len(xp), len(yp))
    acc = None
    for i, xi in enumerate(xp):
        for j, yj in enumerate(yp):
            if i + j < n:
                t = lax.dot_general(xi, yj, dn, preferred_element_type=F32)
                acc = t if acc is None else acc + t
    return acc


_NN = (((1,), (0,)), ((), ()))
_NT = (((1,), (1,)), ((), ()))


def _head_sum(val, ones, n):
    gw = ones.shape[0]
    cols = []
    for j in range(val.shape[1] // gw):
        cols.append(_mm(_pieces(val[:, j * gw:(j + 1) * gw], n), [ones], _NN))
    return jnp.concatenate(cols, axis=1)


def _ada_kernel(c_ref, w_ref, b_ref, o_ref):
    c = c_ref[...]
    s = c * jax.nn.sigmoid(c)
    o_ref[0] = _dot(_bf(s), _bf(w_ref[0])) + b_ref[0]


def _ada(c_all, ada_w, ada_b):
    depth, d, n = ada_w.shape
    m = c_all.shape[0]
    tn = 1024
    return pl.pallas_call(
        _ada_kernel,
        grid=(depth, n // tn),
        in_specs=[
            pl.BlockSpec((m, d), lambda i, j: (0, 0)),
            pl.BlockSpec((1, d, tn), lambda i, j: (i, 0, j)),
            pl.BlockSpec((1, 1, tn), lambda i, j: (i, 0, j)),
        ],
        out_specs=pl.BlockSpec((1, m, tn), lambda i, j: (i, 0, j)),
        out_shape=jax.ShapeDtypeStruct((depth, m, n), F32),
        compiler_params=pltpu.CompilerParams(dimension_semantics=("parallel", "parallel")),
        name="ada",
    )(c_all, ada_w, ada_b.reshape(depth, 1, n))


def _rwkv_proj_kernel(per_row, tiles_per_seq,
                      x_ref, xp_ref, mod_ref, ng_ref, mix_ref, wr_ref, wk_ref, wv_ref,
                      w1_ref, w2_ref, w0_ref, a1_ref, a2_ref, a0_ref, g1_ref, g2_ref,
                      kk_ref, ka_ref, ones_ref,
                      r_o, k_o, v_o, lw_o, kk_o, a_o, g_o, h_o, h_scr, xx_scr):
    d = x_ref.shape[1]
    tm = x_ref.shape[0]
    ng = ng_ref[...]
    shift = _mod(mod_ref, per_row, 0, d)
    scale = _mod(mod_ref, per_row, 1, d)
    h = _prenorm(x_ref[...], ng, shift, scale)
    if per_row:
        prev = xp_ref[...]
        h_o[...] = h
    else:
        hp = _prenorm(xp_ref[...], ng, shift, scale)
        first = (pl.program_id(0) % tiles_per_seq) == 0
        prow = jnp.where(first, 0.0, hp[7:8, :])
        rowid = lax.broadcasted_iota(jnp.int32, h.shape, 0)
        prev = jnp.where(rowid == 0, prow, pltpu.roll(h, 1, axis=0))
        h_o[0] = h[tm - 8:tm, :]
    h_scr[...] = h
    xx_scr[...] = prev - h

    sub = min(tm, 256)
    subs = [slice(i, i + sub) for i in range(0, tm, sub)]

    def mixed(j):
        return [_bf(h_scr[sl, :] + xx_scr[sl, :] * mix_ref[j:j + 1, :]) for sl in subs]

    def each(f, *cols):
        return [f(*vals) for vals in zip(*cols)]

    for sl, r in zip(subs, each(lambda m: _dot(m, wr_ref[...]), mixed(0))):
        r_o[sl, :] = r
    t1 = each(lambda m: _bf(jnp.tanh(_dot(m, w1_ref[...]))), mixed(1))
    for sl, wl in zip(subs, each(lambda t: w0_ref[...] + _dot(t, w2_ref[...]), t1)):
        lw_o[sl, :] = jax.nn.sigmoid(wl) * (-math.exp(-0.5))
    t4 = each(lambda m: _bf(_dot(m, a1_ref[...])), mixed(4))
    a = each(lambda t: jax.nn.sigmoid(a0_ref[...] + _dot(t, a2_ref[...])), t4)
    for sl, ai in zip(subs, a):
        a_o[sl, :] = ai
    k = each(lambda m: _dot(m, wk_ref[...]), mixed(2))
    kk = each(lambda ki: ki * kk_ref[...], k)
    ss = each(lambda kki: _head_sum(kki * kki, ones_ref[...], 1), kk)
    for sl, ki, ai, kki, ssi in zip(subs, k, a, kk, ss):
        k_o[sl, :] = ki * (1.0 + (ai - 1.0) * ka_ref[...])
        kk_o[sl, :] = kki * lax.rsqrt(jnp.maximum(ssi, 1e-24))
    for sl, v in zip(subs, each(lambda m: _dot(m, wv_ref[...]), mixed(3))):
        v_o[sl, :] = v
    t5 = each(lambda m: _bf(jax.nn.sigmoid(_dot(m, g1_ref[...]))), mixed(5))
    for sl, g in zip(subs, each(lambda t: _dot(t, g2_ref[...]), t5)):
        g_o[sl, :] = g


def _const_spec(shape):
    nd = len(shape)
    return pl.BlockSpec(shape, lambda *_: (0,) * nd, pipeline_mode=pl.Buffered(1))


def _rwkv_proj(x, prev_rows, mod, ng, p, tm, seq_len):
    t, d = x.shape
    per_row = prev_rows is not None
    nt = t // tm
    tiles_per_seq = 1 if per_row else seq_len // tm
    row = lambda i: (i, 0)
    if per_row:
        xp, xp_spec = prev_rows, pl.BlockSpec((tm, d), row)
        mod_spec = pl.BlockSpec((tm, N_ADA * d), row)
        h_shape, h_spec = jax.ShapeDtypeStruct((t, d), F32), pl.BlockSpec((tm, d), row)
    else:
        xp, xp_spec = x, pl.BlockSpec((8, d), lambda i: (jnp.maximum(i * (tm // 8) - 1, 0), 0))
        mod_spec = pl.BlockSpec((1, N_ADA, d), lambda i: (i // tiles_per_seq, 0, 0))
        h_shape, h_spec = jax.ShapeDtypeStruct((nt, 8, d), F32), pl.BlockSpec((1, 8, d), lambda i: (i, 0, 0))
    consts = [ng, p["mix"], p["wr"], p["wk"], p["wv"], p["w1"], p["w2"], p["w0"], p["a1"], p["a2"], p["a0"],
              p["g1"], p["g2"], p["kk"], p["ka"], p["ones"]]
    act = jax.ShapeDtypeStruct((t, d), F32)
    return pl.pallas_call(
        functools.partial(_rwkv_proj_kernel, per_row, tiles_per_seq),
        grid=(nt,),
        in_specs=[pl.BlockSpec((tm, d), row), xp_spec, mod_spec] + [_const_spec(c.shape) for c in consts],
        out_specs=[pl.BlockSpec((tm, d), row)] * 7 + [h_spec],
        out_shape=[act] * 7 + [h_shape],
        scratch_shapes=[pltpu.VMEM((tm, d), F32)] * 2,
        compiler_params=pltpu.CompilerParams(dimension_semantics=("parallel",), vmem_limit_bytes=VMEM_LIMIT),
        name="rwkv_proj_step" if per_row else "rwkv_proj_seq",
    )(x, xp, mod, *consts)


def _head_of_lane(shape, axis):
    return lax.broadcasted_iota(jnp.int32, shape, axis) // HEAD_DIM


def _bd(p, low_half):
    zero = jnp.zeros((), p.dtype)
    zeros = jnp.zeros((p.shape[0], LANE), p.dtype)
    per_col = LANE // HEAD_DIM
    ncol = GW // LANE
    blocks = []
    for h in range(HPG):
        c = h // per_col
        col = p[:, c * LANE:(c + 1) * LANE]
        col = jnp.where(low_half, col, zero) if h % per_col == 0 else jnp.where(low_half, zero, col)
        blocks.append(jnp.concatenate([col if j == c else zeros for j in range(ncol)], axis=1))
    return jnp.concatenate(blocks, axis=0)


def _wkv_prep_chunks(rs, ks, vs, lws, kks, as_, masks):
    tri, diag, strict, incl, eye, blk8, levels, lane_head = masks
    n = len(rs)

    def each(f, *args):
        return [f(*[a[i] for a in args]) for i in range(n)]

    def pp(xs, ys):
        rhs = each(lambda y: _bd(_bf(y), diag), ys)
        return each(lambda x, b: _dot(_bf(x), b), xs, rhs)

    def pabt(xs, ys):
        rhs = each(lambda y: _bd(_bf(y), diag), ys)
        return each(lambda x, b: _dot_nt(_bf(x), b), xs, rhs)

    def pick_diag(gram):
        out = gram[0:HEAD_DIM, :]
        for hh in range(1, HPG):
            out = jnp.where(lane_head == hh, gram[hh * HEAD_DIM:(hh + 1) * HEAD_DIM, :], out)
        return out

    def patb(xs, ys):
        lhs = each(lambda x: _bf(x.T), xs)
        return each(lambda xt, y: pick_diag(_dot(xt, _bf(y))), lhs, ys)

    def add(xs, ys):
        return each(lambda x, y: x + y, xs, ys)

    def masked(m, xs):
        return each(lambda x: jnp.where(m, x, 0.0), xs)

    c = each(lambda lw: _mm([tri], _pieces(lw, 2), _NN), lws)
    c_last = each(lambda ci: ci[CHUNK - 1:CHUNK, :], c)
    e_neg = each(lambda ci: jnp.exp(-ci), c)
    e_end = each(lambda ci, cl: jnp.exp(cl - ci), c, c_last)
    bv = each(lambda kk, a: kk * a, kks, as_)
    a_t = each(lambda kk, ci, lw: _bf(-kk * jnp.exp(ci - lw)), kks, c, lws)
    r_t = each(lambda r, ci: r * jnp.exp(ci), rs, c)
    b_t = each(lambda b, e: _bf(b * e), bv, e_neg)
    k_t = each(lambda k, e: _bf(k * e), ks, e_neg)
    b_e = each(lambda b, e: _bf(b * e), bv, e_end)
    k_e = each(lambda k, e: _bf(k * e), ks, e_end)

    ar = each(lambda x, y: jnp.concatenate([x, _bf(y)], axis=0), a_t, r_t)
    xb = pabt(ar, b_t)
    xk = pabt(ar, k_t)
    a_ab = each(lambda x: jnp.where(strict, x[:CHUNK], 0.0), xb)
    a_rb = each(lambda x: jnp.where(incl, x[CHUNK:], 0.0), xb)
    a_ak = each(lambda x: jnp.where(strict, x[:CHUNK], 0.0), xk)
    a_rk = each(lambda x: jnp.where(incl, x[CHUNK:], 0.0), xk)

    n0 = masked(blk8, a_ab)
    n2 = pp(n0, n0)
    n4 = pp(n2, n2)
    plus_eye = lambda xs: each(lambda x: eye + x, xs)
    tinv = pp(pp(plus_eye(n0), plus_eye(n2)), plus_eye(n4))
    b = 8
    for lvl in levels:
        lower = lambda x: jnp.concatenate([x[r0:r0 + b] for r0 in range(b, CHUNK, 2 * b)], axis=0)
        z = pp(pp(each(lower, tinv), masked(lvl, a_ab)), tinv)

        def merged(t, zi):
            rows = []
            for j, r0 in enumerate(range(0, CHUNK, 2 * b)):
                rows += [t[r0:r0 + b], t[r0 + b:r0 + 2 * b] + zi[j * b:(j + 1) * b]]
            return jnp.concatenate(rows, axis=0)

        tinv = each(merged, tinv, z)
        b *= 2

    w1 = pp(tinv, a_t)
    w2 = pp(tinv, pp(a_ak, vs))
    qc = add(r_t, pp(a_rb, w1))
    y0 = add(pp(a_rb, w2), pp(a_rk, vs))
    mc = patb(w1, b_e)
    stack = lambda x, y: jnp.concatenate([x, y], axis=0)
    cc = patb(each(stack, w2, vs), each(stack, b_e, k_e))
    return qc, y0, mc, cc, c_last


def _wkv_masks():
    t = lax.broadcasted_iota(jnp.int32, (CHUNK, GW), 0)
    s = lax.broadcasted_iota(jnp.int32, (CHUNK, GW), 1) % HEAD_DIM
    tri = (lax.broadcasted_iota(jnp.int32, (CHUNK, CHUNK), 1)
           <= lax.broadcasted_iota(jnp.int32, (CHUNK, CHUNK), 0)).astype(BF16)
    diag = lax.broadcasted_iota(jnp.int32, (CHUNK, LANE), 1) < HEAD_DIM
    strict = s < t
    incl = s <= t
    eye = (s == t).astype(F32)
    blk8 = strict & ((s // 8) == (t // 8))
    levels = []
    b = 8
    while b < CHUNK:
        levels.append(((s // (2 * b)) == (t // (2 * b))) & ((s // b) != (t // b)) & strict)
        b *= 2
    lane_head = _head_of_lane((HEAD_DIM, GW), 1)
    return tri, diag, strict, incl, eye, blk8, levels, lane_head


def _rwkv_epilogue(y, r, k, v, g, lg, lb, rk, ones):
    inv_n = 1.0 / HEAD_DIM
    mu = _head_sum(y, ones, 1) * inv_n
    dy = y - mu
    var = _head_sum(dy * dy, ones, 1) * inv_n
    yn = dy * lax.rsqrt(var + LNX_EPS) * lg + lb
    bonus = _head_sum(r * k * rk, ones, 1) * v
    return (yn + bonus) * g


def _wkv_seq_kernel(nsub, ng, r_ref, k_ref, v_ref, lw_ref, kk_ref, a_ref, y_o, s_o, s_scr):
    step = pl.program_id(1)

    @pl.when(step == 0)
    def _():
        s_scr[...] = jnp.zeros_like(s_scr)

    masks = _wkv_masks()
    diag = masks[1]
    sls = [slice(u * CHUNK, (u + 1) * CHUNK) for u in range(nsub)]
    gls = [slice(g * GW, (g + 1) * GW) for g in range(ng)]
    items = [(u, g) for u in range(nsub) for g in range(ng)]
    chunks = lambda ref: [ref[sls[u], gls[g]] for u, g in items]
    qc, y0, mc, cc, cl = _wkv_prep_chunks(chunks(r_ref), chunks(k_ref), chunks(v_ref), chunks(lw_ref),
                                          chunks(kk_ref), chunks(a_ref), masks)

    states = [s_scr[:, gl] for gl in gls]
    for u in range(nsub):
        sb = [_bf(s) for s in states]
        upd = [_dot(sb[g], _bd(_bf(mc[u * ng + g]), diag)) for g in range(ng)]
        yy = [_dot_nt(_bf(qc[u * ng + g]), _bd(sb[g], diag)) for g in range(ng)]
        for g in range(ng):
            i = u * ng + g
            states[g] = states[g] * jnp.exp(cl[i]) + upd[g] + cc[i]
            y_o[sls[u], gls[g]] = yy[g] + y0[i]
    for g, gl in enumerate(gls):
        s_scr[:, gl] = states[g]

    @pl.when(step == pl.num_programs(1) - 1)
    def _():
        for g in range(ng):
            s_o[0, g] = states[g]


def _wkv_seq(r, k, v, lw, kk, a, batch, seq_len):
    t, d = r.shape
    nsub = 4
    rows = nsub * CHUNK
    ng = d // GW
    steps = seq_len // rows
    blk = pl.BlockSpec((rows, d), lambda i, j: (i * steps + j, 0))
    return pl.pallas_call(
        functools.partial(_wkv_seq_kernel, nsub, ng),
        grid=(batch, steps),
        in_specs=[blk] * 6,
        out_specs=[blk, pl.BlockSpec((1, ng, HEAD_DIM, GW), lambda i, j: (i, 0, 0, 0))],
        out_shape=[jax.ShapeDtypeStruct((t, d), F32), jax.ShapeDtypeStruct((batch, ng, HEAD_DIM, GW), F32)],
        scratch_shapes=[pltpu.VMEM((HEAD_DIM, d), F32)],
        compiler_params=pltpu.CompilerParams(dimension_semantics=("parallel", "arbitrary"),
                                             vmem_limit_bytes=VMEM_LIMIT),
        name="wkv_seq",
    )(r, k, v, lw, kk, a)


def _wkv_step_kernel(s_ref, r_ref, w_ref, k_ref, v_ref, a_ref, b_ref, so_ref, y_ref):
    s = s_ref[0]
    sa = jnp.sum(s * a_ref[0][None], axis=1)
    s = s * w_ref[0][None] + sa[:, None, :] * b_ref[0][None] + v_ref[0][:, None, :] * k_ref[0][None]
    so_ref[0] = s
    y_ref[0] = jnp.sum(s * r_ref[0][None], axis=1)


def _wkv_step(state, r, decay, k, v, a, b):
    nh, n, _, nb = state.shape
    sspec = pl.BlockSpec((1, n, n, nb), lambda i: (i, 0, 0, 0))
    vspec = pl.BlockSpec((1, n, nb), lambda i: (i, 0, 0))
    return pl.pallas_call(
        _wkv_step_kernel,
        grid=(nh,),
        in_specs=[sspec] + [vspec] * 6,
        out_specs=[sspec, vspec],
        out_shape=[jax.ShapeDtypeStruct(state.shape, F32), jax.ShapeDtypeStruct((nh, n, nb), F32)],
        compiler_params=pltpu.CompilerParams(dimension_semantics=("parallel",)),
        name="wkv_step",
    )(state, r, decay, k, v, a, b)


def _post_kernel(rwkv, final, per_row, *refs):
    refs = list(refs)
    x_ref = refs.pop(0)
    z_ref = refs.pop(0)
    if rwkv:
        r_ref, k_ref, v_ref, g_ref = refs[:4]
        refs = refs[4:]
    mod_ref = refs.pop(0)
    if rwkv:
        lg_ref, lb_ref, rk_ref, ones_ref = refs[:4]
        refs = refs[4:]
    wo_ref, n2_ref, up_ref, down_ref = refs[:4]
    refs = refs[4:]
    if final:
        fg_ref = refs.pop(0)
    x_o = refs.pop(0)
    d = x_ref.shape[1]

    z = z_ref[...]
    if rwkv:
        z = _rwkv_epilogue(z, r_ref[...], k_ref[...], v_ref[...], g_ref[...],
                           lg_ref[...], lb_ref[...], rk_ref[...], ones_ref[...])
    out = _dot(_bf(z), wo_ref[...])
    x1 = x_ref[...] + _mod(mod_ref, per_row, 2, d) * out
    h2 = _bf(_prenorm(x1, n2_ref[...], _mod(mod_ref, per_row, 3, d), _mod(mod_ref, per_row, 4, d)))
    dff = up_ref.shape[1]
    fc = 1024
    acc = None
    for c in range(dff // fc):
        hid = jnp.maximum(_dot(h2, up_ref[:, c * fc:(c + 1) * fc]), 0.0)
        part = _dot(_bf(hid * hid), down_ref[c * fc:(c + 1) * fc, :])
        acc = part if acc is None else acc + part
    x2 = x1 + _mod(mod_ref, per_row, 5, d) * acc
    x_o[...] = x2
    if final:
        refs[0][...] = _rms(x2, fg_ref[...])


def _post(x, z, extra, mod, per_row, rw, wo, n2g, up, down, final_g, tm, seq_len):
    t, d = x.shape
    rwkv = rw is not None
    final = final_g is not None
    row = lambda i: (i, 0)
    tile = pl.BlockSpec((tm, d), row)
    tiles_per_seq = 1 if per_row else seq_len // tm
    mod_spec = (pl.BlockSpec((tm, N_ADA * d), row) if per_row
                else pl.BlockSpec((1, N_ADA, d), lambda i: (i // tiles_per_seq, 0, 0)))
    args = [x, z]
    specs = [tile, tile]
    if rwkv:
        args += list(extra)
        specs += [tile] * 4
    args.append(mod)
    specs.append(mod_spec)
    consts = (list(rw) if rwkv else []) + [wo, n2g, up, down] + ([final_g] if final else [])
    args += consts
    specs += [_const_spec(c.shape) for c in consts]
    act = jax.ShapeDtypeStruct((t, d), F32)
    outs = pl.pallas_call(
        functools.partial(_post_kernel, rwkv, final, per_row),
        grid=(t // tm,),
        in_specs=specs,
        out_specs=[tile] * (2 if final else 1),
        out_shape=[act] * (2 if final else 1),
        compiler_params=pltpu.CompilerParams(dimension_semantics=("parallel",), vmem_limit_bytes=VMEM_LIMIT),
        name=("post_rwkv" if rwkv else "post") + ("_final" if final else "") + ("_step" if per_row else "_seq"),
    )(*args)
    return outs


def _rope_table_kernel(base, step, invf_ref, c_o, sm_o, sp_o):
    shape = c_o.shape
    row = lax.broadcasted_iota(jnp.int32, shape, 0) + pl.program_id(0) * shape[0]
    pos = (base + step * row).astype(F32)
    ang = pos * invf_ref[...]
    lane = lax.broadcasted_iota(jnp.int32, shape, 1) % HEAD_DIM
    cos = jnp.cos(ang)
    sin = jnp.sin(ang)
    half = ROPE_DIM // 2
    c_o[...] = jnp.where(lane < ROPE_DIM, cos, 1.0)
    sm_o[...] = jnp.where(lane < half, -sin, 0.0)
    sp_o[...] = jnp.where((lane >= half) & (lane < ROPE_DIM), sin, 0.0)


def _rope_tables(n, base, step):
    half = ROPE_DIM // 2
    inv_freq = ROPE_THETA ** (-jnp.arange(half, dtype=F32) / half)
    lane = jnp.arange(LANE) % HEAD_DIM
    invf = jnp.where(lane < ROPE_DIM, inv_freq[lane % half], 0.0).reshape(1, LANE).astype(F32)
    tr = min(n, 512)
    spec = pl.BlockSpec((tr, LANE), lambda i: (i, 0))
    shp = jax.ShapeDtypeStruct((n, LANE), F32)
    return pl.pallas_call(
        functools.partial(_rope_table_kernel, base, step),
        grid=(n // tr,),
        in_specs=[pl.BlockSpec((1, LANE), lambda i: (0, 0))],
        out_specs=[spec] * 3,
        out_shape=[shp] * 3,
        compiler_params=pltpu.CompilerParams(dimension_semantics=("parallel",)),
        name="rope_tables",
    )(invf)


def _qkv_kernel(per_row, x_ref, mod_ref, ng_ref, w_ref, b_ref, c_ref, sm_ref, sp_ref, q_o, k_o, v_o):
    tm, d = x_ref.shape
    nq, nk = q_o.shape[1], k_o.shape[1]
    sub = min(tm, 256)
    subs = [slice(i, i + sub) for i in range(0, tm, sub)]
    shift, scale = _mod(mod_ref, per_row, 0, d), _mod(mod_ref, per_row, 1, d)
    rows = lambda m, sl: m[sl, :] if per_row else m
    h = [_bf(_prenorm(x_ref[sl, :], ng_ref[...], rows(shift, sl), rows(scale, sl))) for sl in subs]
    qkv = [_dot(hh, w_ref[...]) + b_ref[...] for hh in h]
    for sl, z in zip(subs, qkv):
        c, sm, sp = c_ref[sl, :], sm_ref[sl, :], sp_ref[sl, :]

        def rope_into(o_ref, off, width):
            for j in range(width // LANE):
                zc = z[:, off + j * LANE:off + (j + 1) * LANE]
                o_ref[sl, j * LANE:(j + 1) * LANE] = (zc * c + pltpu.roll(zc, LANE - ROPE_DIM // 2, axis=1) * sm
                                                      + pltpu.roll(zc, ROPE_DIM // 2, axis=1) * sp)

        rope_into(q_o, 0, nq)
        rope_into(k_o, nq, nk)
        v_o[sl, :] = z[:, nq + nk:]


def _qkv(x, mod, per_row, ng, w, b, tables, tm, seq_len):
    t, d = x.shape
    nkv = KV_HEADS * HEAD_DIM
    row = lambda i: (i, 0)
    tiles_per_seq = 1 if per_row else seq_len // tm
    mod_spec = (pl.BlockSpec((tm, N_ADA * d), row) if per_row
                else pl.BlockSpec((1, N_ADA, d), lambda i: (i // tiles_per_seq, 0, 0)))
    tab_spec = pl.BlockSpec((tm, LANE), lambda i: (i % tiles_per_seq, 0))
    return pl.pallas_call(
        functools.partial(_qkv_kernel, per_row),
        grid=(t // tm,),
        in_specs=[pl.BlockSpec((tm, d), row), mod_spec, _const_spec(ng.shape), _const_spec(w.shape),
                  _const_spec(b.shape)] + [tab_spec] * 3,
        out_specs=[pl.BlockSpec((tm, d), row), pl.BlockSpec((tm, nkv), row), pl.BlockSpec((tm, nkv), row)],
        out_shape=[jax.ShapeDtypeStruct((t, d), F32), jax.ShapeDtypeStruct((t, nkv), F32),
                   jax.ShapeDtypeStruct((t, nkv), F32)],
        compiler_params=pltpu.CompilerParams(dimension_semantics=("parallel",), vmem_limit_bytes=VMEM_LIMIT),
        name="qkv_step" if per_row else "qkv_seq",
    )(x, mod, ng, w, b, *tables)


def _sink_softmax(s, sink):
    m = jnp.maximum(jnp.max(s, axis=-1, keepdims=True), sink)
    p = jnp.exp(s - m)
    return p / (jnp.sum(p, axis=-1, keepdims=True) + jnp.exp(sink - m))


def _swa_seq_kernel(q_ref, kc_ref, kp_ref, vc_ref, vp_ref, sink_ref, o_ref):
    n = pl.program_id(1)
    q = q_ref[...]
    kall = jnp.concatenate([kp_ref[...], kc_ref[...]], axis=0)
    vall = jnp.concatenate([vp_ref[...], vc_ref[...]], axis=0)
    blk = ATT_BLOCK
    j = lax.broadcasted_iota(jnp.int32, (2 * blk, GROUP * blk), 0)
    i = lax.broadcasted_iota(jnp.int32, (2 * blk, GROUP * blk), 1) % blk
    mask = (j >= i + (blk - WINDOW)) & (j <= i + blk) & ((n > 0) | (j >= blk))
    kvs = range(KV_HEADS)
    hsl = [slice(h * HEAD_DIM, (h + 1) * HEAD_DIM) for h in range(KV_HEADS * GROUP)]
    eye_d = (lax.broadcasted_iota(jnp.int32, (HEAD_DIM, HEAD_DIM), 0)
             == lax.broadcasted_iota(jnp.int32, (HEAD_DIM, HEAD_DIM), 1)).astype(BF16)
    eye_q = (lax.broadcasted_iota(jnp.int32, (blk, blk), 0)
             == lax.broadcasted_iota(jnp.int32, (blk, blk), 1)).astype(BF16)
    ones = jnp.ones((16, 2 * blk), BF16)
    scale = HEAD_DIM ** -0.5
    kh = [_bf(kall[:, hsl[c]]) for c in kvs]
    vte = [jnp.concatenate([_bf(_dot_nt(eye_d, _bf(vall[:, hsl[c]]))), ones], axis=0) for c in kvs]
    qg = [_bf(jnp.concatenate([q[:, hsl[c * GROUP + g]] for g in range(GROUP)], axis=0) * scale) for c in kvs]
    s = [jnp.where(mask, _dot_nt(kh[c], qg[c]), -jnp.inf) for c in kvs]
    sink = [jnp.concatenate([jnp.broadcast_to(sink_ref[0:1, h:h + 1], (1, blk))
                             for h in range(c * GROUP, (c + 1) * GROUP)], axis=1) for c in kvs]
    m = [jnp.maximum(jnp.max(s[c], axis=0, keepdims=True), sink[c]) for c in kvs]
    p = [_bf(jnp.exp(s[c] - m[c])) for c in kvs]
    oe = [_dot(vte[c], p[c]) for c in kvs]
    ot = [_bf(oe[c][:HEAD_DIM] / (oe[c][HEAD_DIM:HEAD_DIM + 1] + jnp.exp(sink[c] - m[c]))) for c in kvs]
    for c in kvs:
        for pair in range(GROUP // 2):
            both = jnp.concatenate([ot[c][:, (2 * pair + e) * blk:(2 * pair + e + 1) * blk] for e in range(2)],
                                   axis=0)
            col = (c * GROUP + 2 * pair) * HEAD_DIM
            o_ref[:, col:col + 2 * HEAD_DIM] = _dot_nt(eye_q, both).astype(o_ref.dtype)


def _swa_seq(q, k, v, sink, batch, seq_len):
    t, d = q.shape
    nkv = k.shape[1]
    nb = seq_len // ATT_BLOCK
    cur = lambda b, n: (b * nb + n, 0)
    prv = lambda b, n: (b * nb + jnp.maximum(n - 1, 0), 0)
    return pl.pallas_call(
        _swa_seq_kernel,
        grid=(batch, nb),
        in_specs=[pl.BlockSpec((ATT_BLOCK, d), cur),
                  pl.BlockSpec((ATT_BLOCK, nkv), cur), pl.BlockSpec((ATT_BLOCK, nkv), prv),
                  pl.BlockSpec((ATT_BLOCK, nkv), cur), pl.BlockSpec((ATT_BLOCK, nkv), prv),
                  pl.BlockSpec(sink.shape, lambda b, n: (0, 0))],
        out_specs=pl.BlockSpec((ATT_BLOCK, d), cur),
        out_shape=jax.ShapeDtypeStruct((t, d), BF16),
        compiler_params=pltpu.CompilerParams(dimension_semantics=("parallel", "parallel")),
        name="swa_seq",
    )(q, k, k, v, v, sink)


def _swa_step_kernel(q_ref, kn_ref, vn_ref, knt_ref, vnt_ref, kc_ref, vc_ref, sink_ref, o_ref, ko_ref, vo_ref):
    nb, _, w = kc_ref.shape
    kc = kc_ref[...]
    vc = vc_ref[...]
    kn = kn_ref[...]
    vn = vn_ref[...]
    q = q_ref[...]
    scale = HEAD_DIM ** -0.5
    kvs = range(KV_HEADS)
    ksl = [slice(h * HEAD_DIM, (h + 1) * HEAD_DIM) for h in kvs]
    gsl = [slice(h * GROUP, (h + 1) * GROUP) for h in kvs]
    qh = [_bf(q[:, gsl[h], :]) for h in kvs]
    s_old = [jnp.einsum("bgd,bdw->bgw", qh[h], _bf(kc[:, ksl[h], :]), preferred_element_type=F32) * scale
             for h in kvs]
    s_new = [jnp.sum(qh[h].astype(F32) * _bf(kn[:, :, ksl[h]]).astype(F32), axis=-1, keepdims=True) * scale
             for h in kvs]
    sink = [sink_ref[h] for h in kvs]
    m = [jnp.maximum(jnp.maximum(jnp.max(s_old[h], axis=-1, keepdims=True), s_new[h]), sink[h]) for h in kvs]
    p_old = [jnp.exp(s_old[h] - m[h]) for h in kvs]
    p_new = [jnp.exp(s_new[h] - m[h]) for h in kvs]
    den = [jnp.sum(p_old[h], axis=-1, keepdims=True) + p_new[h] + jnp.exp(sink[h] - m[h]) for h in kvs]
    o = [jnp.einsum("bgw,bdw->bgd", _bf(p_old[h] / den[h]), _bf(vc[:, ksl[h], :]), preferred_element_type=F32)
         for h in kvs]
    for h in kvs:
        o_ref[:, gsl[h], :] = o[h] + _bf(p_new[h] / den[h]).astype(F32) * _bf(vn[:, :, ksl[h]]).astype(F32)
    last = lax.broadcasted_iota(jnp.int32, kc.shape, 2) == w - 1

    def column(t_ref):
        t = t_ref[0]
        return jnp.stack([jnp.broadcast_to(t[:, b:b + 1], t.shape[:1] + (w,)) for b in range(nb)])

    ko_ref[...] = jnp.where(last, column(knt_ref), pltpu.roll(kc, w - 1, axis=2))
    vo_ref[...] = jnp.where(last, column(vnt_ref), pltpu.roll(vc, w - 1, axis=2))


def _swa_step(q, kn, vn, cache_k, cache_v, sink):
    b, nh, hd = q.shape
    nkv, w = cache_k.shape[1], cache_k.shape[2]
    nb = min(b, 16)
    i3 = lambda i: (i, 0, 0)
    cspec = pl.BlockSpec((nb, nkv, w), i3)
    nspec = pl.BlockSpec((nb, 1, nkv), i3)
    tspec = pl.BlockSpec((1, nkv, nb), i3)
    qspec = pl.BlockSpec((nb, nh, hd), i3)
    cols = lambda z: z.reshape(b // nb, nb, nkv).transpose(0, 2, 1)
    return pl.pallas_call(
        _swa_step_kernel,
        grid=(b // nb,),
        in_specs=[qspec, nspec, nspec, tspec, tspec, cspec, cspec, pl.BlockSpec(sink.shape, lambda i: (0, 0, 0))],
        out_specs=[qspec, cspec, cspec],
        out_shape=[jax.ShapeDtypeStruct(q.shape, F32), jax.ShapeDtypeStruct(cache_k.shape, F32),
                   jax.ShapeDtypeStruct(cache_v.shape, F32)],
        compiler_params=pltpu.CompilerParams(dimension_semantics=("parallel",)),
        name="swa_step",
    )(q, kn.reshape(b, 1, nkv), vn.reshape(b, 1, nkv), cols(kn), cols(vn), cache_k, cache_v, sink)


def kernel(x_prompt, x_sample, c_prompt, c_sample, state_wkv, state_shift, cache_k, cache_v, norm1_g, norm2_g, ada_w, ada_b, mlp_up, mlp_down, final_g, rw_mix, rw_wr, rw_wk, rw_wv, rw_wo, rw_w0, rw_w1, rw_w2, rw_a0, rw_a1, rw_a2, rw_g1, rw_g2, rw_kk, rw_ka, rw_rk, rw_lnx_g, rw_lnx_b, at_wqkv, at_bqkv, at_wo, at_sink):
    bp, seq, d = x_prompt.shape
    bs = x_sample.shape[0]
    assert x_sample.shape[1] == 1 and d % GW == 0 and seq % (8 * CHUNK) == 0
    nh = d // HEAD_DIM
    tp = bp * seq
    row = lambda vec: vec.reshape(1, -1)

    n_c = bp + bs
    pad = (-n_c) % 8
    c_all = jnp.concatenate([c_prompt, c_sample, jnp.zeros((pad, d), F32)], axis=0)
    ada = _ada(c_all, ada_w, ada_b)
    mod_p = [ada[i, :bp].reshape(bp, N_ADA, d) for i in range(2)]
    mod_s = [ada[i, bp:bp + bs] for i in range(2)]

    xp = x_prompt.reshape(tp, d)
    xs = x_sample.reshape(bs, d)

    head_id = jnp.arange(GW) // HEAD_DIM
    ones_bd = (head_id[:, None] == head_id[None, :]).astype(BF16)

    rwp = dict(mix=rw_mix[0], wr=_bf(rw_wr[0]), wk=_bf(rw_wk[0]), wv=_bf(rw_wv[0]),
               w1=_bf(rw_w1[0]), w2=_bf(rw_w2[0]), w0=row(rw_w0[0]),
               a1=_bf(rw_a1[0]), a2=_bf(rw_a2[0]), a0=row(rw_a0[0]),
               g1=_bf(rw_g1[0]), g2=_bf(rw_g2[0]), kk=row(rw_kk[0]), ka=row(rw_ka[0]), ones=ones_bd)
    n1 = row(norm1_g[0])
    post_rw = (row(rw_lnx_g[0]), row(rw_lnx_b[0]), row(rw_rk[0]), ones_bd)
    wo0, up0, down0 = _bf(rw_wo[0]), _bf(mlp_up[0]), _bf(mlp_down[0])
    n2 = row(norm2_g[0])

    r, k, v, lw, kk, a, g, hlast = _rwkv_proj(xp, None, mod_p[0], n1, rwp, 512, seq)
    tiles = seq // 512
    shift_p = hlast.reshape(bp, tiles, 8, d)[:, -1, -1]
    y, st = _wkv_seq(r, k, v, lw, kk, a, bp, seq)
    wkv_p = st.reshape(bp, d // GW, HEAD_DIM, HPG, HEAD_DIM).transpose(0, 1, 3, 2, 4).reshape(bp, nh, HEAD_DIM, HEAD_DIM)
    (xp,) = _post(xp, y, (r, k, v, g), mod_p[0], False, post_rw, wo0, n2, up0, down0, None, 256, seq)

    rs, ks, vs, lws, kks, as_, gs, shift_s = _rwkv_proj(xs, state_shift[0], mod_s[0], n1, rwp, bs, 1)
    vecs = jnp.stack([rs, jnp.exp(lws), ks, vs, -kks, kks * as_]).transpose(0, 2, 1).reshape(6, nh, HEAD_DIM, bs)
    st_s, ys = _wkv_step(state_wkv[0].transpose(1, 2, 3, 0), *vecs)
    wkv_s = st_s.transpose(3, 0, 1, 2)
    (xs,) = _post(xs, ys.reshape(d, bs).T, (rs, ks, vs, gs), mod_s[0], True, post_rw, wo0, n2, up0, down0, None, bs, 1)

    wqkv, bqkv, wo1 = _bf(at_wqkv[0]), row(at_bqkv[0]), _bf(at_wo[0])
    n1, n2 = row(norm1_g[1]), row(norm2_g[1])
    up1, down1 = _bf(mlp_up[1]), _bf(mlp_down[1])
    fg = row(final_g)
    nkv = KV_HEADS * HEAD_DIM

    q, kq, vq = _qkv(xp, mod_p[1], False, n1, wqkv, bqkv, _rope_tables(seq, 0, 1), 512, seq)
    o = _swa_seq(q, kq, vq, row(at_sink[0]), bp, seq)
    keep = min(WINDOW, seq)
    k_p = kq.reshape(bp, seq, nkv)[:, seq - keep:].reshape(bp, keep, KV_HEADS, HEAD_DIM)
    v_p = vq.reshape(bp, seq, nkv)[:, seq - keep:].reshape(bp, keep, KV_HEADS, HEAD_DIM)
    xp, y_p = _post(xp, o, None, mod_p[1], False, None, wo1, n2, up1, down1, fg, 512, seq)

    qs, kn, vn = _qkv(xs, mod_s[1], True, n1, wqkv, bqkv, _rope_tables(bs, PAST_LEN, 0), bs, 1)
    w_buf = cache_k.shape[2]
    to_minor = lambda c: c.transpose(0, 2, 3, 1).reshape(bs, nkv, w_buf)
    from_minor = lambda c: c.reshape(bs, KV_HEADS, HEAD_DIM, w_buf).transpose(0, 3, 1, 2)
    os_, k_s, v_s = _swa_step(qs.reshape(bs, nh, HEAD_DIM), kn, vn, to_minor(cache_k[0]), to_minor(cache_v[0]),
                              at_sink[0].reshape(KV_HEADS, GROUP, 1))
    xs, y_s = _post(xs, os_.reshape(bs, d), None, mod_s[1], True, None, wo1, n2, up1, down1, fg, bs, 1)

    return (y_p.reshape(bp, seq, d), y_s.reshape(bs, 1, d),
            wkv_p[None], wkv_s[None], shift_p[None], shift_s[None],
            k_p[None], from_minor(k_s)[None], v_p[None], from_minor(v_s)[None])
```

```python
import functools
import math

import jax
import jax.numpy as jnp
from jax import lax
from jax.experimental import pallas as pl
from jax.experimental.pallas import tpu as pltpu

F32 = jnp.float32
BF16 = jnp.bfloat16

HEAD_DIM = 64
KV_HEADS = 4
GROUP = 4
WINDOW = 128
ATT_BLOCK = 128
ROPE_DIM = HEAD_DIM // 4
ROPE_THETA = 500000.0
PAST_LEN = 8192
NORM_EPS = 1e-6
LNX_EPS = 64e-5
N_ADA = 6

LANE = 128
V7X_VMEM_BYTES = 64 * 1024 * 1024
VMEM_LIMIT = V7X_VMEM_BYTES * 7 // 8

CHUNK = 64
HPG = 2
GW = HPG * HEAD_DIM


def _bf(x):
    return x.astype(BF16)


def _dot(a, b):
    return jnp.dot(a, b, preferred_element_type=F32)


def _dot_nt(a, b):
    return lax.dot_general(a, b, (((1,), (1,)), ((), ())), preferred_element_type=F32)


def _rms(x, g):
    ms = jnp.mean(x * x, axis=-1, keepdims=True)
    return x * lax.rsqrt(ms + NORM_EPS) * g


def _prenorm(x, g, shift, scale):
    return _rms(x, g) * (1.0 + scale) + shift


def _mod(mod_ref, per_row, idx, d):
    if per_row:
        return mod_ref[:, idx * d:(idx + 1) * d]
    return mod_ref[0, idx:idx + 1, :]


def _pieces(x, n):
    out = []
    rem = x
    for i in range(n):
        p = rem.astype(BF16)
        out.append(p)
        if i + 1 < n:
            rem = rem - p.astype(F32)
    return out


def _mm(xp, yp, dn):
    n = max(len(xp), len(yp))
    acc = None
    for i, xi in enumerate(xp):
        for j, yj in enumerate(yp):
            if i + j < n:
                t = lax.dot_general(xi, yj, dn, preferred_element_type=F32)
                acc = t if acc is None else acc + t
    return acc


_NN = (((1,), (0,)), ((), ()))
_NT = (((1,), (1,)), ((), ()))


def _head_sum(val, ones, n):
    gw = ones.shape[0]
    cols = []
    for j in range(val.shape[1] // gw):
        cols.append(_mm(_pieces(val[:, j * gw:(j + 1) * gw], n), [ones], _NN))
    return jnp.concatenate(cols, axis=1)


def _ada_kernel(c_ref, w_ref, b_ref, o_ref):
    c = c_ref[...]
    s = c * jax.nn.sigmoid(c)
    o_ref[0] = _dot(_bf(s), _bf(w_ref[0])) + b_ref[0]


def _ada(c_all, ada_w, ada_b):
    depth, d, n = ada_w.shape
    m = c_all.shape[0]
    tn = 1024
    return pl.pallas_call(
        _ada_kernel,
        grid=(depth, n // tn),
        in_specs=[
            pl.BlockSpec((m, d), lambda i, j: (0, 0)),
            pl.BlockSpec((1, d, tn), lambda i, j: (i, 0, j)),
            pl.BlockSpec((1, 1, tn), lambda i, j: (i, 0, j)),
        ],
        out_specs=pl.BlockSpec((1, m, tn), lambda i, j: (i, 0, j)),
        out_shape=jax.ShapeDtypeStruct((depth, m, n), F32),
        compiler_params=pltpu.CompilerParams(dimension_semantics=("parallel", "parallel")),
        name="ada",
    )(c_all, ada_w, ada_b.reshape(depth, 1, n))


def _rwkv_proj_kernel(per_row, tiles_per_seq,
                      x_ref, xp_ref, mod_ref, ng_ref, mix_ref, wr_ref, wk_ref, wv_ref,
                      w1_ref, w2_ref, w0_ref, a1_ref, a2_ref, a0_ref, g1_ref, g2_ref,
                      kk_ref, ka_ref, ones_ref,
                      r_o, k_o, v_o, lw_o, kk_o, a_o, g_o, h_o, h_scr, xx_scr):
    d = x_ref.shape[1]
    tm = x_ref.shape[0]
    ng = ng_ref[...]
    shift = _mod(mod_ref, per_row, 0, d)
    scale = _mod(mod_ref, per_row, 1, d)
    h = _prenorm(x_ref[...], ng, shift, scale)
    if per_row:
        prev = xp_ref[...]
        h_o[...] = h
    else:
        hp = _prenorm(xp_ref[...], ng, shift, scale)
        first = (pl.program_id(0) % tiles_per_seq) == 0
        prow = jnp.where(first, 0.0, hp[7:8, :])
        rowid = lax.broadcasted_iota(jnp.int32, h.shape, 0)
        prev = jnp.where(rowid == 0, prow, pltpu.roll(h, 1, axis=0))
        h_o[0] = h[tm - 8:tm, :]
    h_scr[...] = h
    xx_scr[...] = prev - h

    sub = min(tm, 256)
    subs = [slice(i, i + sub) for i in range(0, tm, sub)]

    def mixed(j):
        return [_bf(h_scr[sl, :] + xx_scr[sl, :] * mix_ref[j:j + 1, :]) for sl in subs]

    def each(f, *cols):
        return [f(*vals) for vals in zip(*cols)]

    for sl, r in zip(subs, each(lambda m: _dot(m, wr_ref[...]), mixed(0))):
        r_o[sl, :] = r
    t1 = each(lambda m: _bf(jnp.tanh(_dot(m, w1_ref[...]))), mixed(1))
    for sl, wl in zip(subs, each(lambda t: w0_ref[...] + _dot(t, w2_ref[...]), t1)):
        lw_o[sl, :] = jax.nn.sigmoid(wl) * (-math.exp(-0.5))
    t4 = each(lambda m: _bf(_dot(m, a1_ref[...])), mixed(4))
    a = each(lambda t: jax.nn.sigmoid(a0_ref[...] + _dot(t, a2_ref[...])), t4)
    for sl, ai in zip(subs, a):
        a_o[sl, :] = ai
    k = each(lambda m: _dot(m, wk_ref[...]), mixed(2))
    kk = each(lambda ki: ki * kk_ref[...], k)
    ss = each(lambda kki: _head_sum(kki * kki, ones_ref[...], 1), kk)
    for sl, ki, ai, kki, ssi in zip(subs, k, a, kk, ss):
        k_o[sl, :] = ki * (1.0 + (ai - 1.0) * ka_ref[...])
        kk_o[sl, :] = kki * lax.rsqrt(jnp.maximum(ssi, 1e-24))
    for sl, v in zip(subs, each(lambda m: _dot(m, wv_ref[...]), mixed(3))):
        v_o[sl, :] = v
    t5 = each(lambda m: _bf(jax.nn.sigmoid(_dot(m, g1_ref[...]))), mixed(5))
    for sl, g in zip(subs, each(lambda t: _dot(t, g2_ref[...]), t5)):
        g_o[sl, :] = g


def _const_spec(shape):
    nd = len(shape)
    return pl.BlockSpec(shape, lambda *_: (0,) * nd, pipeline_mode=pl.Buffered(1))


def _rwkv_proj(x, prev_rows, mod, ng, p, tm, seq_len):
    t, d = x.shape
    per_row = prev_rows is not None
    nt = t // tm
    tiles_per_seq = 1 if per_row else seq_len // tm
    row = lambda i: (i, 0)
    if per_row:
        xp, xp_spec = prev_rows, pl.BlockSpec((tm, d), row)
        mod_spec = pl.BlockSpec((tm, N_ADA * d), row)
        h_shape, h_spec = jax.ShapeDtypeStruct((t, d), F32), pl.BlockSpec((tm, d), row)
    else:
        xp, xp_spec = x, pl.BlockSpec((8, d), lambda i: (jnp.maximum(i * (tm // 8) - 1, 0), 0))
        mod_spec = pl.BlockSpec((1, N_ADA, d), lambda i: (i // tiles_per_seq, 0, 0))
        h_shape, h_spec = jax.ShapeDtypeStruct((nt, 8, d), F32), pl.BlockSpec((1, 8, d), lambda i: (i, 0, 0))
    consts = [ng, p["mix"], p["wr"], p["wk"], p["wv"], p["w1"], p["w2"], p["w0"], p["a1"], p["a2"], p["a0"],
              p["g1"], p["g2"], p["kk"], p["ka"], p["ones"]]
    act = jax.ShapeDtypeStruct((t, d), F32)
    return pl.pallas_call(
        functools.partial(_rwkv_proj_kernel, per_row, tiles_per_seq),
        grid=(nt,),
        in_specs=[pl.BlockSpec((tm, d), row), xp_spec, mod_spec] + [_const_spec(c.shape) for c in consts],
        out_specs=[pl.BlockSpec((tm, d), row)] * 7 + [h_spec],
        out_shape=[act] * 7 + [h_shape],
        scratch_shapes=[pltpu.VMEM((tm, d), F32)] * 2,
        compiler_params=pltpu.CompilerParams(dimension_semantics=("parallel",), vmem_limit_bytes=VMEM_LIMIT),
        name="rwkv_proj_step" if per_row else "rwkv_proj_seq",
    )(x, xp, mod, *consts)


def _head_of_lane(shape, axis):
    return lax.broadcasted_iota(jnp.int32, shape, axis) // HEAD_DIM


def _bd(p, low_half):
    zero = jnp.zeros((), p.dtype)
    zeros = jnp.zeros((p.shape[0], LANE), p.dtype)
    per_col = LANE // HEAD_DIM
    ncol = GW // LANE
    blocks = []
    for h in range(HPG):
        c = h // per_col
        col = p[:, c * LANE:(c + 1) * LANE]
        col = jnp.where(low_half, col, zero) if h % per_col == 0 else jnp.where(low_half, zero, col)
        blocks.append(jnp.concatenate([col if j == c else zeros for j in range(ncol)], axis=1))
    return jnp.concatenate(blocks, axis=0)


def _wkv_prep_chunks(rs, ks, vs, lws, kks, as_, masks):
    tri, diag, strict, incl, eye, blk8, levels, lane_head = masks
    n = len(rs)

    def each(f, *args):
        return [f(*[a[i] for a in args]) for i in range(n)]

    def pp(xs, ys):
        rhs = each(lambda y: _bd(_bf(y), diag), ys)
        return each(lambda x, b: _dot(_bf(x), b), xs, rhs)

    def pabt(xs, ys):
        rhs = each(lambda y: _bd(_bf(y), diag), ys)
        return each(lambda x, b: _dot_nt(_bf(x), b), xs, rhs)

    def pick_diag(gram):
        out = gram[0:HEAD_DIM, :]
        for hh in range(1, HPG):
            out = jnp.where(lane_head == hh, gram[hh * HEAD_DIM:(hh + 1) * HEAD_DIM, :], out)
        return out

    def patb(xs, ys):
        lhs = each(lambda x: _bf(x.T), xs)
        return each(lambda xt, y: pick_diag(_dot(xt, _bf(y))), lhs, ys)

    def add(xs, ys):
        return each(lambda x, y: x + y, xs, ys)

    def masked(m, xs):
        return each(lambda x: jnp.where(m, x, 0.0), xs)

    c2 = each(lambda lw: _dot(tri, jnp.concatenate(_pieces(lw, 2), axis=1)), lws)
    c = each(lambda x: x[:, :GW] + x[:, GW:], c2)
    c_last = each(lambda ci: ci[CHUNK - 1:CHUNK, :], c)
    e_neg = each(lambda ci: jnp.exp(-ci), c)
    e_end = each(lambda ci, cl: jnp.exp(cl - ci), c, c_last)
    bv = each(lambda kk, a: kk * a, kks, as_)
    a_t = each(lambda kk, ci, lw: _bf(-kk * jnp.exp(ci - lw)), kks, c, lws)
    r_t = each(lambda r, ci: r * jnp.exp(ci), rs, c)
    b_t = each(lambda b, e: _bf(b * e), bv, e_neg)
    k_t = each(lambda k, e: _bf(k * e), ks, e_neg)
    b_e = each(lambda b, e: _bf(b * e), bv, e_end)
    k_e = each(lambda k, e: _bf(k * e), ks, e_end)

    bd = lambda y: _bd(_bf(y), diag)
    ar = each(lambda x, y: jnp.concatenate([x, _bf(y)], axis=0), a_t, r_t)
    xbk = each(lambda x, b, k: _dot_nt(x, jnp.concatenate([bd(b), bd(k)], axis=0)), ar, b_t, k_t)
    a_ab = each(lambda x: jnp.where(strict, x[:CHUNK, :GW], 0.0), xbk)
    a_rb = each(lambda x: jnp.where(incl, x[CHUNK:, :GW], 0.0), xbk)
    a_ak = each(lambda x: jnp.where(strict, x[:CHUNK, GW:], 0.0), xbk)
    a_rk = each(lambda x: jnp.where(incl, x[CHUNK:, GW:], 0.0), xbk)
    av = pp(a_ak, vs)

    n0 = masked(blk8, a_ab)
    n2 = pp(n0, n0)
    both = pp(each(lambda x, y: jnp.concatenate([x, eye + y], axis=0), n2, n0), n2)
    n4 = each(lambda x: x[:CHUNK], both)
    t01 = each(lambda x, y: eye + x + y[CHUNK:], n0, both)
    tinv = add(t01, pp(t01, n4))
    b = 8
    for lvl in levels:
        lower = lambda x: jnp.concatenate([x[r0:r0 + b] for r0 in range(b, CHUNK, 2 * b)], axis=0)
        z = pp(pp(each(lower, tinv), masked(lvl, a_ab)), tinv)

        def merged(t, zi):
            rows = []
            for j, r0 in enumerate(range(0, CHUNK, 2 * b)):
                rows += [t[r0:r0 + b], t[r0 + b:r0 + 2 * b] + zi[j * b:(j + 1) * b]]
            return jnp.concatenate(rows, axis=0)

        tinv = each(merged, tinv, z)
        b *= 2

    w12 = each(lambda t, a, x: _dot(_bf(t), jnp.concatenate([bd(a), bd(x)], axis=1)), tinv, a_t, av)
    w1 = each(lambda w: w[:, :GW], w12)
    w2 = each(lambda w: w[:, GW:], w12)
    zeros = jnp.zeros((GW, GW), BF16)

    def qy_rhs(wa, wb, v):
        return jnp.concatenate([jnp.concatenate([bd(wa), bd(wb)], axis=1),
                                jnp.concatenate([zeros, bd(v)], axis=1)], axis=0)

    qy = each(lambda x1, x2, wa, wb, v: _dot(jnp.concatenate([_bf(x1), _bf(x2)], axis=1), qy_rhs(wa, wb, v)),
              a_rb, a_rk, w1, w2, vs)
    qc = each(lambda r, x: r + x[:, :GW], r_t, qy)
    y0 = each(lambda x: x[:, GW:], qy)
    stack = lambda x, y: jnp.concatenate([x, y], axis=0)
    lhs = each(lambda wa, wb, v: _bf(jnp.concatenate([stack(wa, jnp.zeros_like(wa)).T, stack(wb, v).T], axis=0)),
               w1, w2, vs)
    gram = each(lambda x, b, k: _dot(x, stack(b, k)), lhs, b_e, k_e)
    mc = each(lambda gm: pick_diag(gm[:GW]), gram)
    cc = each(lambda gm: pick_diag(gm[GW:]), gram)
    return qc, y0, mc, cc, c_last


def _wkv_masks():
    t = lax.broadcasted_iota(jnp.int32, (CHUNK, GW), 0)
    s = lax.broadcasted_iota(jnp.int32, (CHUNK, GW), 1) % HEAD_DIM
    tri = (lax.broadcasted_iota(jnp.int32, (CHUNK, CHUNK), 1)
           <= lax.broadcasted_iota(jnp.int32, (CHUNK, CHUNK), 0)).astype(BF16)
    diag = lax.broadcasted_iota(jnp.int32, (CHUNK, LANE), 1) < HEAD_DIM
    strict = s < t
    incl = s <= t
    eye = (s == t).astype(F32)
    blk8 = strict & ((s // 8) == (t // 8))
    levels = []
    b = 8
    while b < CHUNK:
        levels.append(((s // (2 * b)) == (t // (2 * b))) & ((s // b) != (t // b)) & strict)
        b *= 2
    lane_head = _head_of_lane((HEAD_DIM, GW), 1)
    return tri, diag, strict, incl, eye, blk8, levels, lane_head


def _rwkv_epilogue(y, r, k, v, g, lg, lb, rk, ones):
    inv_n = 1.0 / HEAD_DIM
    mu = _head_sum(y, ones, 1) * inv_n
    dy = y - mu
    var = _head_sum(dy * dy, ones, 1) * inv_n
    yn = dy * lax.rsqrt(var + LNX_EPS) * lg + lb
    bonus = _head_sum(r * k * rk, ones, 1) * v
    return (yn + bonus) * g


def _wkv_seq_kernel(nsub, ng, r_ref, k_ref, v_ref, lw_ref, kk_ref, a_ref, y_o, s_o, s_scr):
    step = pl.program_id(1)

    @pl.when(step == 0)
    def _():
        s_scr[...] = jnp.zeros_like(s_scr)

    masks = _wkv_masks()
    diag = masks[1]
    sls = [slice(u * CHUNK, (u + 1) * CHUNK) for u in range(nsub)]
    gls = [slice(g * GW, (g + 1) * GW) for g in range(ng)]
    items = [(u, g) for u in range(nsub) for g in range(ng)]
    chunks = lambda ref: [ref[sls[u], gls[g]] for u, g in items]
    qc, y0, mc, cc, cl = _wkv_prep_chunks(chunks(r_ref), chunks(k_ref), chunks(v_ref), chunks(lw_ref),
                                          chunks(kk_ref), chunks(a_ref), masks)

    states = [s_scr[:, gl] for gl in gls]
    for u in range(nsub):
        sb = [_bf(s) for s in states]
        upd = [_dot(sb[g], _bd(_bf(mc[u * ng + g]), diag)) for g in range(ng)]
        yy = [_dot_nt(_bf(qc[u * ng + g]), _bd(sb[g], diag)) for g in range(ng)]
        for g in range(ng):
            i = u * ng + g
            states[g] = states[g] * jnp.exp(cl[i]) + upd[g] + cc[i]
            y_o[sls[u], gls[g]] = yy[g] + y0[i]
    for g, gl in enumerate(gls):
        s_scr[:, gl] = states[g]

    @pl.when(step == pl.num_programs(1) - 1)
    def _():
        for g in range(ng):
            s_o[0, g] = states[g]


def _wkv_seq(r, k, v, lw, kk, a, batch, seq_len):
    t, d = r.shape
    nsub = 4
    rows = nsub * CHUNK
    ng = d // GW
    steps = seq_len // rows
    blk = pl.BlockSpec((rows, d), lambda i, j: (i * steps + j, 0))
    return pl.pallas_call(
        functools.partial(_wkv_seq_kernel, nsub, ng),
        grid=(batch, steps),
        in_specs=[blk] * 6,
        out_specs=[blk, pl.BlockSpec((1, ng, HEAD_DIM, GW), lambda i, j: (i, 0, 0, 0))],
        out_shape=[jax.ShapeDtypeStruct((t, d), F32), jax.ShapeDtypeStruct((batch, ng, HEAD_DIM, GW), F32)],
        scratch_shapes=[pltpu.VMEM((HEAD_DIM, d), F32)],
        compiler_params=pltpu.CompilerParams(dimension_semantics=("parallel", "arbitrary"),
                                             vmem_limit_bytes=VMEM_LIMIT),
        name="wkv_seq",
    )(r, k, v, lw, kk, a)


def _wkv_step_kernel(s_ref, r_ref, w_ref, k_ref, v_ref, a_ref, b_ref, so_ref, y_ref):
    s = s_ref[0]
    sa = jnp.sum(s * a_ref[0][None], axis=1)
    s = s * w_ref[0][None] + sa[:, None, :] * b_ref[0][None] + v_ref[0][:, None, :] * k_ref[0][None]
    so_ref[0] = s
    y_ref[0] = jnp.sum(s * r_ref[0][None], axis=1)


def _wkv_step(state, r, decay, k, v, a, b):
    nh, n, _, nb = state.shape
    sspec = pl.BlockSpec((1, n, n, nb), lambda i: (i, 0, 0, 0))
    vspec = pl.BlockSpec((1, n, nb), lambda i: (i, 0, 0))
    return pl.pallas_call(
        _wkv_step_kernel,
        grid=(nh,),
        in_specs=[sspec] + [vspec] * 6,
        out_specs=[sspec, vspec],
        out_shape=[jax.ShapeDtypeStruct(state.shape, F32), jax.ShapeDtypeStruct((nh, n, nb), F32)],
        compiler_params=pltpu.CompilerParams(dimension_semantics=("parallel",)),
        name="wkv_step",
    )(state, r, decay, k, v, a, b)


def _post_kernel(rwkv, final, per_row, *refs):
    refs = list(refs)
    x_ref = refs.pop(0)
    z_ref = refs.pop(0)
    if rwkv:
        r_ref, k_ref, v_ref, g_ref = refs[:4]
        refs = refs[4:]
    mod_ref = refs.pop(0)
    if rwkv:
        lg_ref, lb_ref, rk_ref, ones_ref = refs[:4]
        refs = refs[4:]
    wo_ref, n2_ref, up_ref, down_ref = refs[:4]
    refs = refs[4:]
    if final:
        fg_ref = refs.pop(0)
    x_o = refs.pop(0)
    d = x_ref.shape[1]

    z = z_ref[...]
    if rwkv:
        z = _rwkv_epilogue(z, r_ref[...], k_ref[...], v_ref[...], g_ref[...],
                           lg_ref[...], lb_ref[...], rk_ref[...], ones_ref[...])
    out = _dot(_bf(z), wo_ref[...])
    x1 = x_ref[...] + _mod(mod_ref, per_row, 2, d) * out
    h2 = _bf(_prenorm(x1, n2_ref[...], _mod(mod_ref, per_row, 3, d), _mod(mod_ref, per_row, 4, d)))
    dff = up_ref.shape[1]
    fc = 1024
    acc = None
    for c in range(dff // fc):
        hid = jnp.maximum(_dot(h2, up_ref[:, c * fc:(c + 1) * fc]), 0.0)
        part = _dot(_bf(hid * hid), down_ref[c * fc:(c + 1) * fc, :])
        acc = part if acc is None else acc + part
    x2 = x1 + _mod(mod_ref, per_row, 5, d) * acc
    x_o[...] = x2
    if final:
        refs[0][...] = _rms(x2, fg_ref[...])


def _post(x, z, extra, mod, per_row, rw, wo, n2g, up, down, final_g, tm, seq_len):
    t, d = x.shape
    rwkv = rw is not None
    final = final_g is not None
    row = lambda i: (i, 0)
    tile = pl.BlockSpec((tm, d), row)
    tiles_per_seq = 1 if per_row else seq_len // tm
    mod_spec = (pl.BlockSpec((tm, N_ADA * d), row) if per_row
                else pl.BlockSpec((1, N_ADA, d), lambda i: (i // tiles_per_seq, 0, 0)))
    args = [x, z]
    specs = [tile, tile]
    if rwkv:
        args += list(extra)
        specs += [tile] * 4
    args.append(mod)
    specs.append(mod_spec)
    consts = (list(rw) if rwkv else []) + [wo, n2g, up, down] + ([final_g] if final else [])
    args += consts
    specs += [_const_spec(c.shape) for c in consts]
    act = jax.ShapeDtypeStruct((t, d), F32)
    outs = pl.pallas_call(
        functools.partial(_post_kernel, rwkv, final, per_row),
        grid=(t // tm,),
        in_specs=specs,
        out_specs=[tile] * (2 if final else 1),
        out_shape=[act] * (2 if final else 1),
        compiler_params=pltpu.CompilerParams(dimension_semantics=("parallel",), vmem_limit_bytes=VMEM_LIMIT),
        name=("post_rwkv" if rwkv else "post") + ("_final" if final else "") + ("_step" if per_row else "_seq"),
    )(*args)
    return outs


def _rope_table_kernel(base, step, invf_ref, c_o, sm_o, sp_o):
    shape = c_o.shape
    row = lax.broadcasted_iota(jnp.int32, shape, 0) + pl.program_id(0) * shape[0]
    pos = (base + step * row).astype(F32)
    ang = pos * invf_ref[...]
    lane = lax.broadcasted_iota(jnp.int32, shape, 1) % HEAD_DIM
    cos = jnp.cos(ang)
    sin = jnp.sin(ang)
    half = ROPE_DIM // 2
    c_o[...] = jnp.where(lane < ROPE_DIM, cos, 1.0)
    sm_o[...] = jnp.where(lane < half, -sin, 0.0)
    sp_o[...] = jnp.where((lane >= half) & (lane < ROPE_DIM), sin, 0.0)


def _rope_tables(n, base, step):
    half = ROPE_DIM // 2
    inv_freq = ROPE_THETA ** (-jnp.arange(half, dtype=F32) / half)
    lane = jnp.arange(LANE) % HEAD_DIM
    invf = jnp.where(lane < ROPE_DIM, inv_freq[lane % half], 0.0).reshape(1, LANE).astype(F32)
    tr = min(n, 512)
    spec = pl.BlockSpec((tr, LANE), lambda i: (i, 0))
    shp = jax.ShapeDtypeStruct((n, LANE), F32)
    return pl.pallas_call(
        functools.partial(_rope_table_kernel, base, step),
        grid=(n // tr,),
        in_specs=[pl.BlockSpec((1, LANE), lambda i: (0, 0))],
        out_specs=[spec] * 3,
        out_shape=[shp] * 3,
        compiler_params=pltpu.CompilerParams(dimension_semantics=("parallel",)),
        name="rope_tables",
    )(invf)


def _qkv_kernel(per_row, x_ref, mod_ref, ng_ref, w_ref, b_ref, c_ref, sm_ref, sp_ref, q_o, k_o, v_o):
    tm, d = x_ref.shape
    nq, nk = q_o.shape[1], k_o.shape[1]
    sub = min(tm, 256)
    subs = [slice(i, i + sub) for i in range(0, tm, sub)]
    shift, scale = _mod(mod_ref, per_row, 0, d), _mod(mod_ref, per_row, 1, d)
    rows = lambda m, sl: m[sl, :] if per_row else m
    h = [_bf(_prenorm(x_ref[sl, :], ng_ref[...], rows(shift, sl), rows(scale, sl))) for sl in subs]
    qkv = [_dot(hh, w_ref[...]) + b_ref[...] for hh in h]
    for sl, z in zip(subs, qkv):
        c, sm, sp = c_ref[sl, :], sm_ref[sl, :], sp_ref[sl, :]

        def rope_into(o_ref, off, width):
            for j in range(width // LANE):
                zc = z[:, off + j * LANE:off + (j + 1) * LANE]
                o_ref[sl, j * LANE:(j + 1) * LANE] = (zc * c + pltpu.roll(zc, LANE - ROPE_DIM // 2, axis=1) * sm
                                                      + pltpu.roll(zc, ROPE_DIM // 2, axis=1) * sp)

        rope_into(q_o, 0, nq)
        rope_into(k_o, nq, nk)
        v_o[sl, :] = z[:, nq + nk:]


def _qkv(x, mod, per_row, ng, w, b, tables, tm, seq_len):
    t, d = x.shape
    nkv = KV_HEADS * HEAD_DIM
    row = lambda i: (i, 0)
    tiles_per_seq = 1 if per_row else seq_len // tm
    mod_spec = (pl.BlockSpec((tm, N_ADA * d), row) if per_row
                else pl.BlockSpec((1, N_ADA, d), lambda i: (i // tiles_per_seq, 0, 0)))
    tab_spec = pl.BlockSpec((tm, LANE), lambda i: (i % tiles_per_seq, 0))
    return pl.pallas_call(
        functools.partial(_qkv_kernel, per_row),
        grid=(t // tm,),
        in_specs=[pl.BlockSpec((tm, d), row), mod_spec, _const_spec(ng.shape), _const_spec(w.shape),
                  _const_spec(b.shape)] + [tab_spec] * 3,
        out_specs=[pl.BlockSpec((tm, d), row), pl.BlockSpec((tm, nkv), row), pl.BlockSpec((tm, nkv), row)],
        out_shape=[jax.ShapeDtypeStruct((t, d), F32), jax.ShapeDtypeStruct((t, nkv), F32),
                   jax.ShapeDtypeStruct((t, nkv), F32)],
        compiler_params=pltpu.CompilerParams(dimension_semantics=("parallel",), vmem_limit_bytes=VMEM_LIMIT),
        name="qkv_step" if per_row else "qkv_seq",
    )(x, mod, ng, w, b, *tables)


def _sink_softmax(s, sink):
    m = jnp.maximum(jnp.max(s, axis=-1, keepdims=True), sink)
    p = jnp.exp(s - m)
    return p / (jnp.sum(p, axis=-1, keepdims=True) + jnp.exp(sink - m))


def _swa_seq_kernel(q_ref, kc_ref, kp_ref, vc_ref, vp_ref, sink_ref, o_ref):
    n = pl.program_id(1)
    q = q_ref[...]
    kall = jnp.concatenate([kp_ref[...], kc_ref[...]], axis=0)
    vall = jnp.concatenate([vp_ref[...], vc_ref[...]], axis=0)
    blk = ATT_BLOCK
    j = lax.broadcasted_iota(jnp.int32, (2 * blk, GROUP * blk), 0)
    i = lax.broadcasted_iota(jnp.int32, (2 * blk, GROUP * blk), 1) % blk
    mask = (j >= i + (blk - WINDOW)) & (j <= i + blk) & ((n > 0) | (j >= blk))
    kvs = range(KV_HEADS)
    hsl = [slice(h * HEAD_DIM, (h + 1) * HEAD_DIM) for h in range(KV_HEADS * GROUP)]
    eye_d = (lax.broadcasted_iota(jnp.int32, (HEAD_DIM, HEAD_DIM), 0)
             == lax.broadcasted_iota(jnp.int32, (HEAD_DIM, HEAD_DIM), 1)).astype(BF16)
    eye_q = (lax.broadcasted_iota(jnp.int32, (blk, blk), 0)
             == lax.broadcasted_iota(jnp.int32, (blk, blk), 1)).astype(BF16)
    ones = jnp.ones((16, 2 * blk), BF16)
    scale = HEAD_DIM ** -0.5
    kh = [_bf(kall[:, hsl[c]]) for c in kvs]
    vte = [jnp.concatenate([_bf(_dot_nt(eye_d, _bf(vall[:, hsl[c]]))), ones], axis=0) for c in kvs]
    qg = [_bf(jnp.concatenate([q[:, hsl[c * GROUP + g]] for g in range(GROUP)], axis=0) * scale) for c in kvs]
    s = [jnp.where(mask, _dot_nt(kh[c], qg[c]), -jnp.inf) for c in kvs]
    sink = [jnp.concatenate([jnp.broadcast_to(sink_ref[0:1, h:h + 1], (1, blk))
                             for h in range(c * GROUP, (c + 1) * GROUP)], axis=1) for c in kvs]
    m = [jnp.maximum(jnp.max(s[c], axis=0, keepdims=True), sink[c]) for c in kvs]
    p = [_bf(jnp.exp(s[c] - m[c])) for c in kvs]
    oe = [_dot(vte[c], p[c]) for c in kvs]
    ot = [_bf(oe[c][:HEAD_DIM] / (oe[c][HEAD_DIM:HEAD_DIM + 1] + jnp.exp(sink[c] - m[c]))) for c in kvs]
    for c in kvs:
        for pair in range(GROUP // 2):
            both = jnp.concatenate([ot[c][:, (2 * pair + e) * blk:(2 * pair + e + 1) * blk] for e in range(2)],
                                   axis=0)
            col = (c * GROUP + 2 * pair) * HEAD_DIM
            o_ref[:, col:col + 2 * HEAD_DIM] = _dot_nt(eye_q, both).astype(o_ref.dtype)


def _swa_seq(q, k, v, sink, batch, seq_len):
    t, d = q.shape
    nkv = k.shape[1]
    nb = seq_len // ATT_BLOCK
    cur = lambda b, n: (b * nb + n, 0)
    prv = lambda b, n: (b * nb + jnp.maximum(n - 1, 0), 0)
    return pl.pallas_call(
        _swa_seq_kernel,
        grid=(batch, nb),
        in_specs=[pl.BlockSpec((ATT_BLOCK, d), cur),
                  pl.BlockSpec((ATT_BLOCK, nkv), cur), pl.BlockSpec((ATT_BLOCK, nkv), prv),
                  pl.BlockSpec((ATT_BLOCK, nkv), cur), pl.BlockSpec((ATT_BLOCK, nkv), prv),
                  pl.BlockSpec(sink.shape, lambda b, n: (0, 0))],
        out_specs=pl.BlockSpec((ATT_BLOCK, d), cur),
        out_shape=jax.ShapeDtypeStruct((t, d), BF16),
        compiler_params=pltpu.CompilerParams(dimension_semantics=("parallel", "parallel")),
        name="swa_seq",
    )(q, k, k, v, v, sink)


def _swa_step_kernel(q_ref, kn_ref, vn_ref, knt_ref, vnt_ref, kc_ref, vc_ref, sink_ref, o_ref, ko_ref, vo_ref):
    nb, _, w = kc_ref.shape
    kc = kc_ref[...]
    vc = vc_ref[...]
    kn = kn_ref[...]
    vn = vn_ref[...]
    q = q_ref[...]
    scale = HEAD_DIM ** -0.5
    kvs = range(KV_HEADS)
    ksl = [slice(h * HEAD_DIM, (h + 1) * HEAD_DIM) for h in kvs]
    gsl = [slice(h * GROUP, (h + 1) * GROUP) for h in kvs]
    qh = [_bf(q[:, gsl[h], :]) for h in kvs]
    s_old = [jnp.einsum("bgd,bdw->bgw", qh[h], _bf(kc[:, ksl[h], :]), preferred_element_type=F32) * scale
             for h in kvs]
    s_new = [jnp.sum(qh[h].astype(F32) * _bf(kn[:, :, ksl[h]]).astype(F32), axis=-1, keepdims=True) * scale
             for h in kvs]
    sink = [sink_ref[h] for h in kvs]
    m = [jnp.maximum(jnp.maximum(jnp.max(s_old[h], axis=-1, keepdims=True), s_new[h]), sink[h]) for h in kvs]
    p_old = [jnp.exp(s_old[h] - m[h]) for h in kvs]
    p_new = [jnp.exp(s_new[h] - m[h]) for h in kvs]
    den = [jnp.sum(p_old[h], axis=-1, keepdims=True) + p_new[h] + jnp.exp(sink[h] - m[h]) for h in kvs]
    o = [jnp.einsum("bgw,bdw->bgd", _bf(p_old[h] / den[h]), _bf(vc[:, ksl[h], :]), preferred_element_type=F32)
         for h in kvs]
    for h in kvs:
        o_ref[:, gsl[h], :] = o[h] + _bf(p_new[h] / den[h]).astype(F32) * _bf(vn[:, :, ksl[h]]).astype(F32)
    last = lax.broadcasted_iota(jnp.int32, kc.shape, 2) == w - 1

    def column(t_ref):
        t = t_ref[0]
        return jnp.stack([jnp.broadcast_to(t[:, b:b + 1], t.shape[:1] + (w,)) for b in range(nb)])

    ko_ref[...] = jnp.where(last, column(knt_ref), pltpu.roll(kc, w - 1, axis=2))
    vo_ref[...] = jnp.where(last, column(vnt_ref), pltpu.roll(vc, w - 1, axis=2))


def _swa_step(q, kn, vn, cache_k, cache_v, sink):
    b, nh, hd = q.shape
    nkv, w = cache_k.shape[1], cache_k.shape[2]
    nb = min(b, 16)
    i3 = lambda i: (i, 0, 0)
    cspec = pl.BlockSpec((nb, nkv, w), i3)
    nspec = pl.BlockSpec((nb, 1, nkv), i3)
    tspec = pl.BlockSpec((1, nkv, nb), i3)
    qspec = pl.BlockSpec((nb, nh, hd), i3)
    cols = lambda z: z.reshape(b // nb, nb, nkv).transpose(0, 2, 1)
    return pl.pallas_call(
        _swa_step_kernel,
        grid=(b // nb,),
        in_specs=[qspec, nspec, nspec, tspec, tspec, cspec, cspec, pl.BlockSpec(sink.shape, lambda i: (0, 0, 0))],
        out_specs=[qspec, cspec, cspec],
        out_shape=[jax.ShapeDtypeStruct(q.shape, F32), jax.ShapeDtypeStruct(cache_k.shape, F32),
                   jax.ShapeDtypeStruct(cache_v.shape, F32)],
        compiler_params=pltpu.CompilerParams(dimension_semantics=("parallel",)),
        name="swa_step",
    )(q, kn.reshape(b, 1, nkv), vn.reshape(b, 1, nkv), cols(kn), cols(vn), cache_k, cache_v, sink)


def kernel(x_prompt, x_sample, c_prompt, c_sample, state_wkv, state_shift, cache_k, cache_v, norm1_g, norm2_g, ada_w, ada_b, mlp_up, mlp_down, final_g, rw_mix, rw_wr, rw_wk, rw_wv, rw_wo, rw_w0, rw_w1, rw_w2, rw_a0, rw_a1, rw_a2, rw_g1, rw_g2, rw_kk, rw_ka, rw_rk, rw_lnx_g, rw_lnx_b, at_wqkv, at_bqkv, at_wo, at_sink):
    bp, seq, d = x_prompt.shape
    bs = x_sample.shape[0]
    assert x_sample.shape[1] == 1 and d % GW == 0 and seq % (8 * CHUNK) == 0
    nh = d // HEAD_DIM
    tp = bp * seq
    row = lambda vec: vec.reshape(1, -1)

    n_c = bp + bs
    pad = (-n_c) % 8
    c_all = jnp.concatenate([c_prompt, c_sample, jnp.zeros((pad, d), F32)], axis=0)
    ada = _ada(c_all, ada_w, ada_b)
    mod_p = [ada[i, :bp].reshape(bp, N_ADA, d) for i in range(2)]
    mod_s = [ada[i, bp:bp + bs] for i in range(2)]

    xp = x_prompt.reshape(tp, d)
    xs = x_sample.reshape(bs, d)

    head_id = jnp.arange(GW) // HEAD_DIM
    ones_bd = (head_id[:, None] == head_id[None, :]).astype(BF16)

    rwp = dict(mix=rw_mix[0], wr=_bf(rw_wr[0]), wk=_bf(rw_wk[0]), wv=_bf(rw_wv[0]),
               w1=_bf(rw_w1[0]), w2=_bf(rw_w2[0]), w0=row(rw_w0[0]),
               a1=_bf(rw_a1[0]), a2=_bf(rw_a2[0]), a0=row(rw_a0[0]),
               g1=_bf(rw_g1[0]), g2=_bf(rw_g2[0]), kk=row(rw_kk[0]), ka=row(rw_ka[0]), ones=ones_bd)
    n1 = row(norm1_g[0])
    post_rw = (row(rw_lnx_g[0]), row(rw_lnx_b[0]), row(rw_rk[0]), ones_bd)
    wo0, up0, down0 = _bf(rw_wo[0]), _bf(mlp_up[0]), _bf(mlp_down[0])
    n2 = row(norm2_g[0])

    r, k, v, lw, kk, a, g, hlast = _rwkv_proj(xp, None, mod_p[0], n1, rwp, 512, seq)
    tiles = seq // 512
    shift_p = hlast.reshape(bp, tiles, 8, d)[:, -1, -1]
    y, st = _wkv_seq(r, k, v, lw, kk, a, bp, seq)
    wkv_p = st.reshape(bp, d // GW, HEAD_DIM, HPG, HEAD_DIM).transpose(0, 1, 3, 2, 4).reshape(bp, nh, HEAD_DIM, HEAD_DIM)
    (xp,) = _post(xp, y, (r, k, v, g), mod_p[0], False, post_rw, wo0, n2, up0, down0, None, 256, seq)

    rs, ks, vs, lws, kks, as_, gs, shift_s = _rwkv_proj(xs, state_shift[0], mod_s[0], n1, rwp, bs, 1)
    vecs = jnp.stack([rs, jnp.exp(lws), ks, vs, -kks, kks * as_]).transpose(0, 2, 1).reshape(6, nh, HEAD_DIM, bs)
    st_s, ys = _wkv_step(state_wkv[0].transpose(1, 2, 3, 0), *vecs)
    wkv_s = st_s.transpose(3, 0, 1, 2)
    (xs,) = _post(xs, ys.reshape(d, bs).T, (rs, ks, vs, gs), mod_s[0], True, post_rw, wo0, n2, up0, down0, None, bs, 1)

    wqkv, bqkv, wo1 = _bf(at_wqkv[0]), row(at_bqkv[0]), _bf(at_wo[0])
    n1, n2 = row(norm1_g[1]), row(norm2_g[1])
    up1, down1 = _bf(mlp_up[1]), _bf(mlp_down[1])
    fg = row(final_g)
    nkv = KV_HEADS * HEAD_DIM

    q, kq, vq = _qkv(xp, mod_p[1], False, n1, wqkv, bqkv, _rope_tables(seq, 0, 1), 512, seq)
    o = _swa_seq(q, kq, vq, row(at_sink[0]), bp, seq)
    keep = min(WINDOW, seq)
    k_p = kq.reshape(bp, seq, nkv)[:, seq - keep:].reshape(bp, keep, KV_HEADS, HEAD_DIM)
    v_p = vq.reshape(bp, seq, nkv)[:, seq - keep:].reshape(bp, keep, KV_HEADS, HEAD_DIM)
    xp, y_p = _post(xp, o, None, mod_p[1], False, None, wo1, n2, up1, down1, fg, 512, seq)

    qs, kn, vn = _qkv(xs, mod_s[1], True, n1, wqkv, bqkv, _rope_tables(bs, PAST_LEN, 0), bs, 1)
    w_buf = cache_k.shape[2]
    to_minor = lambda c: c.transpose(0, 2, 3, 1).reshape(bs, nkv, w_buf)
    from_minor = lambda c: c.reshape(bs, KV_HEADS, HEAD_DIM, w_buf).transpose(0, 3, 1, 2)
    os_, k_s, v_s = _swa_step(qs.reshape(bs, nh, HEAD_DIM), kn, vn, to_minor(cache_k[0]), to_minor(cache_v[0]),
                              at_sink[0].reshape(KV_HEADS, GROUP, 1))
    xs, y_s = _post(xs, os_.reshape(bs, d), None, mod_s[1], True, None, wo1, n2, up1, down1, fg, bs, 1)

    return (y_p.reshape(bp, seq, d), y_s.reshape(bs, 1, d),
            wkv_p[None], wkv_s[None], shift_p[None], shift_s[None],
            k_p[None], from_minor(k_s)[None], v_p[None], from_minor(v_s)[None])
```

```python
import functools
import math

import jax
import jax.numpy as jnp
from jax import lax
from jax.experimental import pallas as pl
from jax.experimental.pallas import tpu as pltpu

F32 = jnp.float32
BF16 = jnp.bfloat16

HEAD_DIM = 64
KV_HEADS = 4
GROUP = 4
WINDOW = 128
ATT_BLOCK = 128
ROPE_DIM = HEAD_DIM // 4
ROPE_THETA = 500000.0
PAST_LEN = 8192
NORM_EPS = 1e-6
LNX_EPS = 64e-5
N_ADA = 6

LANE = 128
V7X_VMEM_BYTES = 64 * 1024 * 1024
VMEM_LIMIT = V7X_VMEM_BYTES * 7 // 8

CHUNK = 64
HPG = 2
GW = HPG * HEAD_DIM

TM_PROJ = 512
TM_SUB = 256
TM_POST_RWKV = 256
TM_POST = 512
MLP_FF_CHUNK = 1024
WKV_CHUNKS_PER_STEP = 4
ADA_COLS = 1024
ROPE_ROWS = 512
DECODE_ROWS = 16


def _bf(x):
    return x.astype(BF16)


def _dot(a, b):
    return jnp.dot(a, b, preferred_element_type=F32)


def _dot_nt(a, b):
    return lax.dot_general(a, b, (((1,), (1,)), ((), ())), preferred_element_type=F32)


def _rms(x, g):
    ms = jnp.mean(x * x, axis=-1, keepdims=True)
    return x * lax.rsqrt(ms + NORM_EPS) * g


def _prenorm(x, g, shift, scale):
    return _rms(x, g) * (1.0 + scale) + shift


def _mod(mod_ref, per_row, idx, d):
    if per_row:
        return mod_ref[:, idx * d:(idx + 1) * d]
    return mod_ref[0, idx:idx + 1, :]


def _pieces(x, n):
    out = []
    rem = x
    for i in range(n):
        p = rem.astype(BF16)
        out.append(p)
        if i + 1 < n:
            rem = rem - p.astype(F32)
    return out


def _head_sum(val, ones):
    gw = ones.shape[0]
    cols = [_dot(_bf(val[:, j * gw:(j + 1) * gw]), ones) for j in range(val.shape[1] // gw)]
    return jnp.concatenate(cols, axis=1)


def _ada_kernel(c_ref, w_ref, b_ref, o_ref):
    c = c_ref[...]
    s = c * jax.nn.sigmoid(c)
    o_ref[0] = _dot(_bf(s), _bf(w_ref[0])) + b_ref[0]


def _ada(c_all, ada_w, ada_b):
    depth, d, n = ada_w.shape
    m = c_all.shape[0]
    tn = ADA_COLS
    return pl.pallas_call(
        _ada_kernel,
        grid=(depth, n // tn),
        in_specs=[
            pl.BlockSpec((m, d), lambda i, j: (0, 0)),
            pl.BlockSpec((1, d, tn), lambda i, j: (i, 0, j)),
            pl.BlockSpec((1, 1, tn), lambda i, j: (i, 0, j)),
        ],
        out_specs=pl.BlockSpec((1, m, tn), lambda i, j: (i, 0, j)),
        out_shape=jax.ShapeDtypeStruct((depth, m, n), F32),
        compiler_params=pltpu.CompilerParams(dimension_semantics=("parallel", "parallel")),
        name="ada",
    )(c_all, ada_w, ada_b.reshape(depth, 1, n))


def _rwkv_proj_kernel(per_row, tiles_per_seq,
                      x_ref, xp_ref, mod_ref, ng_ref, mix_ref, wr_ref, wk_ref, wv_ref,
                      w1_ref, w2_ref, w0_ref, a1_ref, a2_ref, a0_ref, g1_ref, g2_ref,
                      kk_ref, ka_ref, ones_ref,
                      r_o, k_o, v_o, lw_o, kk_o, a_o, g_o, h_o, h_scr, xx_scr):
    d = x_ref.shape[1]
    tm = x_ref.shape[0]
    ng = ng_ref[...]
    shift = _mod(mod_ref, per_row, 0, d)
    scale = _mod(mod_ref, per_row, 1, d)
    h = _prenorm(x_ref[...], ng, shift, scale)
    if per_row:
        prev = xp_ref[...]
        h_o[...] = h
    else:
        hp = _prenorm(xp_ref[...], ng, shift, scale)
        first = (pl.program_id(0) % tiles_per_seq) == 0
        prow = jnp.where(first, 0.0, hp[7:8, :])
        rowid = lax.broadcasted_iota(jnp.int32, h.shape, 0)
        prev = jnp.where(rowid == 0, prow, pltpu.roll(h, 1, axis=0))
        h_o[0] = h[tm - 8:tm, :]
    h_scr[...] = h
    xx_scr[...] = prev - h

    sub = min(tm, TM_SUB)
    subs = [slice(i, i + sub) for i in range(0, tm, sub)]

    def mixed(j):
        return [_bf(h_scr[sl, :] + xx_scr[sl, :] * mix_ref[j:j + 1, :]) for sl in subs]

    def each(f, *cols):
        return [f(*vals) for vals in zip(*cols)]

    for sl, r in zip(subs, each(lambda m: _dot(m, wr_ref[...]), mixed(0))):
        r_o[sl, :] = r
    t1 = each(lambda m: _bf(jnp.tanh(_dot(m, w1_ref[...]))), mixed(1))
    for sl, wl in zip(subs, each(lambda t: w0_ref[...] + _dot(t, w2_ref[...]), t1)):
        lw_o[sl, :] = jax.nn.sigmoid(wl) * (-math.exp(-0.5))
    t4 = each(lambda m: _bf(_dot(m, a1_ref[...])), mixed(4))
    a = each(lambda t: jax.nn.sigmoid(a0_ref[...] + _dot(t, a2_ref[...])), t4)
    for sl, ai in zip(subs, a):
        a_o[sl, :] = ai
    k = each(lambda m: _dot(m, wk_ref[...]), mixed(2))
    kk = each(lambda ki: ki * kk_ref[...], k)
    ss = each(lambda kki: _head_sum(kki * kki, ones_ref[...]), kk)
    for sl, ki, ai, kki, ssi in zip(subs, k, a, kk, ss):
        k_o[sl, :] = ki * (1.0 + (ai - 1.0) * ka_ref[...])
        kk_o[sl, :] = kki * lax.rsqrt(jnp.maximum(ssi, 1e-24))
    for sl, v in zip(subs, each(lambda m: _dot(m, wv_ref[...]), mixed(3))):
        v_o[sl, :] = v
    t5 = each(lambda m: _bf(jax.nn.sigmoid(_dot(m, g1_ref[...]))), mixed(5))
    for sl, g in zip(subs, each(lambda t: _dot(t, g2_ref[...]), t5)):
        g_o[sl, :] = g


def _const_spec(shape):
    nd = len(shape)
    return pl.BlockSpec(shape, lambda *_: (0,) * nd, pipeline_mode=pl.Buffered(1))


def _rwkv_proj(x, prev_rows, mod, ng, p, tm, seq_len):
    t, d = x.shape
    per_row = prev_rows is not None
    nt = t // tm
    tiles_per_seq = 1 if per_row else seq_len // tm
    row = lambda i: (i, 0)
    if per_row:
        xp, xp_spec = prev_rows, pl.BlockSpec((tm, d), row)
        mod_spec = pl.BlockSpec((tm, N_ADA * d), row)
        h_shape, h_spec = jax.ShapeDtypeStruct((t, d), F32), pl.BlockSpec((tm, d), row)
    else:
        xp, xp_spec = x, pl.BlockSpec((8, d), lambda i: (jnp.maximum(i * (tm // 8) - 1, 0), 0))
        mod_spec = pl.BlockSpec((1, N_ADA, d), lambda i: (i // tiles_per_seq, 0, 0))
        h_shape, h_spec = jax.ShapeDtypeStruct((nt, 8, d), F32), pl.BlockSpec((1, 8, d), lambda i: (i, 0, 0))
    consts = [ng, p["mix"], p["wr"], p["wk"], p["wv"], p["w1"], p["w2"], p["w0"], p["a1"], p["a2"], p["a0"],
              p["g1"], p["g2"], p["kk"], p["ka"], p["ones"]]
    act = jax.ShapeDtypeStruct((t, d), F32)
    return pl.pallas_call(
        functools.partial(_rwkv_proj_kernel, per_row, tiles_per_seq),
        grid=(nt,),
        in_specs=[pl.BlockSpec((tm, d), row), xp_spec, mod_spec] + [_const_spec(c.shape) for c in consts],
        out_specs=[pl.BlockSpec((tm, d), row)] * 7 + [h_spec],
        out_shape=[act] * 7 + [h_shape],
        scratch_shapes=[pltpu.VMEM((tm, d), F32)] * 2,
        compiler_params=pltpu.CompilerParams(dimension_semantics=("parallel",), vmem_limit_bytes=VMEM_LIMIT),
        name="rwkv_proj_step" if per_row else "rwkv_proj_seq",
    )(x, xp, mod, *consts)


def _head_of_lane(shape, axis):
    return lax.broadcasted_iota(jnp.int32, shape, axis) // HEAD_DIM


def _bd(p, low_half):
    zero = jnp.zeros((), p.dtype)
    zeros = jnp.zeros((p.shape[0], LANE), p.dtype)
    per_col = LANE // HEAD_DIM
    ncol = GW // LANE
    blocks = []
    for h in range(HPG):
        c = h // per_col
        col = p[:, c * LANE:(c + 1) * LANE]
        col = jnp.where(low_half, col, zero) if h % per_col == 0 else jnp.where(low_half, zero, col)
        blocks.append(jnp.concatenate([col if j == c else zeros for j in range(ncol)], axis=1))
    return jnp.concatenate(blocks, axis=0)


def _wkv_prep_chunks(rs, ks, vs, lws, kks, as_, masks):
    tri, diag, strict, incl, eye, blk8, levels, lane_head = masks
    n = len(rs)

    def each(f, *args):
        return [f(*[a[i] for a in args]) for i in range(n)]

    def pp(xs, ys):
        rhs = each(lambda y: _bd(_bf(y), diag), ys)
        return each(lambda x, b: _dot(_bf(x), b), xs, rhs)

    def pick_diag(gram):
        out = gram[0:HEAD_DIM, :]
        for hh in range(1, HPG):
            out = jnp.where(lane_head == hh, gram[hh * HEAD_DIM:(hh + 1) * HEAD_DIM, :], out)
        return out

    def add(xs, ys):
        return each(lambda x, y: x + y, xs, ys)

    def masked(m, xs):
        return each(lambda x: jnp.where(m, x, 0.0), xs)

    c2 = each(lambda lw: _dot(tri, jnp.concatenate(_pieces(lw, 2), axis=1)), lws)
    c = each(lambda x: x[:, :GW] + x[:, GW:], c2)
    c_last = each(lambda ci: ci[CHUNK - 1:CHUNK, :], c)
    e_neg = each(lambda ci: jnp.exp(-ci), c)
    e_end = each(lambda ci, cl: jnp.exp(cl - ci), c, c_last)
    bv = each(lambda kk, a: kk * a, kks, as_)
    a_t = each(lambda kk, ci, lw: _bf(-kk * jnp.exp(ci - lw)), kks, c, lws)
    r_t = each(lambda r, ci: r * jnp.exp(ci), rs, c)
    b_t = each(lambda b, e: _bf(b * e), bv, e_neg)
    k_t = each(lambda k, e: _bf(k * e), ks, e_neg)
    b_e = each(lambda b, e: _bf(b * e), bv, e_end)
    k_e = each(lambda k, e: _bf(k * e), ks, e_end)

    bd = lambda y: _bd(_bf(y), diag)
    ar = each(lambda x, y: jnp.concatenate([x, _bf(y)], axis=0), a_t, r_t)
    xbk = each(lambda x, b, k: _dot_nt(x, jnp.concatenate([bd(b), bd(k)], axis=0)), ar, b_t, k_t)
    a_ab = each(lambda x: jnp.where(strict, x[:CHUNK, :GW], 0.0), xbk)
    a_rb = each(lambda x: jnp.where(incl, x[CHUNK:, :GW], 0.0), xbk)
    a_ak = each(lambda x: jnp.where(strict, x[:CHUNK, GW:], 0.0), xbk)
    a_rk = each(lambda x: jnp.where(incl, x[CHUNK:, GW:], 0.0), xbk)
    av = pp(a_ak, vs)

    n0 = masked(blk8, a_ab)
    n2 = pp(n0, n0)
    both = pp(each(lambda x, y: jnp.concatenate([x, eye + y], axis=0), n2, n0), n2)
    n4 = each(lambda x: x[:CHUNK], both)
    t01 = each(lambda x, y: eye + x + y[CHUNK:], n0, both)
    tinv = add(t01, pp(t01, n4))
    b = 8
    for lvl in levels:
        lower = lambda x: jnp.concatenate([x[r0:r0 + b] for r0 in range(b, CHUNK, 2 * b)], axis=0)
        z = pp(pp(each(lower, tinv), masked(lvl, a_ab)), tinv)

        def merged(t, zi):
            rows = []
            for j, r0 in enumerate(range(0, CHUNK, 2 * b)):
                rows += [t[r0:r0 + b], t[r0 + b:r0 + 2 * b] + zi[j * b:(j + 1) * b]]
            return jnp.concatenate(rows, axis=0)

        tinv = each(merged, tinv, z)
        b *= 2

    w12 = each(lambda t, a, x: _dot(_bf(t), jnp.concatenate([bd(a), bd(x)], axis=1)), tinv, a_t, av)
    w1 = each(lambda w: w[:, :GW], w12)
    w2 = each(lambda w: w[:, GW:], w12)
    zeros = jnp.zeros((GW, GW), BF16)

    def qy_rhs(wa, wb, v):
        return jnp.concatenate([jnp.concatenate([bd(wa), bd(wb)], axis=1),
                                jnp.concatenate([zeros, bd(v)], axis=1)], axis=0)

    qy = each(lambda x1, x2, wa, wb, v: _dot(jnp.concatenate([_bf(x1), _bf(x2)], axis=1), qy_rhs(wa, wb, v)),
              a_rb, a_rk, w1, w2, vs)
    qc = each(lambda r, x: r + x[:, :GW], r_t, qy)
    y0 = each(lambda x: x[:, GW:], qy)
    stack = lambda x, y: jnp.concatenate([x, y], axis=0)
    lhs = each(lambda wa, wb, v: _bf(jnp.concatenate([stack(wa, jnp.zeros_like(wa)).T, stack(wb, v).T], axis=0)),
               w1, w2, vs)
    gram = each(lambda x, b, k: _dot(x, stack(b, k)), lhs, b_e, k_e)
    mc = each(lambda gm: pick_diag(gm[:GW]), gram)
    cc = each(lambda gm: pick_diag(gm[GW:]), gram)
    return qc, y0, mc, cc, c_last


def _wkv_masks():
    t = lax.broadcasted_iota(jnp.int32, (CHUNK, GW), 0)
    s = lax.broadcasted_iota(jnp.int32, (CHUNK, GW), 1) % HEAD_DIM
    tri = (lax.broadcasted_iota(jnp.int32, (CHUNK, CHUNK), 1)
           <= lax.broadcasted_iota(jnp.int32, (CHUNK, CHUNK), 0)).astype(BF16)
    diag = lax.broadcasted_iota(jnp.int32, (CHUNK, LANE), 1) < HEAD_DIM
    strict = s < t
    incl = s <= t
    eye = (s == t).astype(F32)
    blk8 = strict & ((s // 8) == (t // 8))
    levels = []
    b = 8
    while b < CHUNK:
        levels.append(((s // (2 * b)) == (t // (2 * b))) & ((s // b) != (t // b)) & strict)
        b *= 2
    lane_head = _head_of_lane((HEAD_DIM, GW), 1)
    return tri, diag, strict, incl, eye, blk8, levels, lane_head


def _rwkv_epilogue(y, r, k, v, g, lg, lb, rk, ones):
    inv_n = 1.0 / HEAD_DIM
    mu = _head_sum(y, ones) * inv_n
    dy = y - mu
    var = _head_sum(dy * dy, ones) * inv_n
    yn = dy * lax.rsqrt(var + LNX_EPS) * lg + lb
    bonus = _head_sum(r * k * rk, ones) * v
    return (yn + bonus) * g


def _wkv_seq_kernel(nsub, ng, r_ref, k_ref, v_ref, lw_ref, kk_ref, a_ref, y_o, s_o, s_scr):
    step = pl.program_id(1)

    @pl.when(step == 0)
    def _():
        s_scr[...] = jnp.zeros_like(s_scr)

    masks = _wkv_masks()
    diag = masks[1]
    sls = [slice(u * CHUNK, (u + 1) * CHUNK) for u in range(nsub)]
    gls = [slice(g * GW, (g + 1) * GW) for g in range(ng)]
    states = [s_scr[:, gl] for gl in gls]

    chunks = lambda ref: [ref[sls[u], gls[g]] for u in range(nsub) for g in range(ng)]
    qc, y0, mc, cc, cl = _wkv_prep_chunks(chunks(r_ref), chunks(k_ref), chunks(v_ref), chunks(lw_ref),
                                          chunks(kk_ref), chunks(a_ref), masks)
    for u in range(nsub):
        sb = [_bf(s) for s in states]
        upd = [_dot(sb[g], _bd(_bf(mc[u * ng + g]), diag)) for g in range(ng)]
        yy = [_dot_nt(_bf(qc[u * ng + g]), _bd(sb[g], diag)) for g in range(ng)]
        for g in range(ng):
            i = u * ng + g
            states[g] = states[g] * jnp.exp(cl[i]) + upd[g] + cc[i]
            y_o[sls[u], gls[g]] = yy[g] + y0[i]
    for g, gl in enumerate(gls):
        s_scr[:, gl] = states[g]

    @pl.when(step == pl.num_programs(1) - 1)
    def _():
        for g in range(ng):
            s_o[0, g] = states[g]


def _wkv_seq(r, k, v, lw, kk, a, batch, seq_len):
    t, d = r.shape
    nsub = WKV_CHUNKS_PER_STEP
    rows = nsub * CHUNK
    ng = d // GW
    steps = seq_len // rows
    blk = pl.BlockSpec((rows, d), lambda i, j: (i * steps + j, 0))
    return pl.pallas_call(
        functools.partial(_wkv_seq_kernel, nsub, ng),
        grid=(batch, steps),
        in_specs=[blk] * 6,
        out_specs=[blk, pl.BlockSpec((1, ng, HEAD_DIM, GW), lambda i, j: (i, 0, 0, 0))],
        out_shape=[jax.ShapeDtypeStruct((t, d), F32), jax.ShapeDtypeStruct((batch, ng, HEAD_DIM, GW), F32)],
        scratch_shapes=[pltpu.VMEM((HEAD_DIM, d), F32)],
        compiler_params=pltpu.CompilerParams(dimension_semantics=("parallel", "arbitrary"),
                                             vmem_limit_bytes=VMEM_LIMIT),
        name="wkv_seq",
    )(r, k, v, lw, kk, a)


def _wkv_step_kernel(s_ref, r_ref, w_ref, k_ref, v_ref, a_ref, b_ref, so_ref, y_ref):
    s = s_ref[0]
    sa = jnp.sum(s * a_ref[0][None], axis=1)
    s = s * w_ref[0][None] + sa[:, None, :] * b_ref[0][None] + v_ref[0][:, None, :] * k_ref[0][None]
    so_ref[0] = s
    y_ref[0] = jnp.sum(s * r_ref[0][None], axis=1)


def _wkv_step(state, r, decay, k, v, a, b):
    nh, n, _, nb = state.shape
    sspec = pl.BlockSpec((1, n, n, nb), lambda i: (i, 0, 0, 0))
    vspec = pl.BlockSpec((1, n, nb), lambda i: (i, 0, 0))
    return pl.pallas_call(
        _wkv_step_kernel,
        grid=(nh,),
        in_specs=[sspec] + [vspec] * 6,
        out_specs=[sspec, vspec],
        out_shape=[jax.ShapeDtypeStruct(state.shape, F32), jax.ShapeDtypeStruct((nh, n, nb), F32)],
        compiler_params=pltpu.CompilerParams(dimension_semantics=("parallel",)),
        name="wkv_step",
    )(state, r, decay, k, v, a, b)


def _post_kernel(rwkv, final, per_row, *refs):
    refs = list(refs)
    x_ref = refs.pop(0)
    z_ref = refs.pop(0)
    if rwkv:
        r_ref, k_ref, v_ref, g_ref = refs[:4]
        refs = refs[4:]
    mod_ref = refs.pop(0)
    if rwkv:
        lg_ref, lb_ref, rk_ref, ones_ref = refs[:4]
        refs = refs[4:]
    wo_ref, n2_ref, up_ref, down_ref = refs[:4]
    refs = refs[4:]
    if final:
        fg_ref = refs.pop(0)
    x_o = refs.pop(0)
    d = x_ref.shape[1]

    z = z_ref[...]
    if rwkv:
        z = _rwkv_epilogue(z, r_ref[...], k_ref[...], v_ref[...], g_ref[...],
                           lg_ref[...], lb_ref[...], rk_ref[...], ones_ref[...])
    out = _dot(_bf(z), wo_ref[...])
    x1 = x_ref[...] + _mod(mod_ref, per_row, 2, d) * out
    h2 = _bf(_prenorm(x1, n2_ref[...], _mod(mod_ref, per_row, 3, d), _mod(mod_ref, per_row, 4, d)))
    dff = up_ref.shape[1]
    fc = MLP_FF_CHUNK
    acc = None
    for c in range(dff // fc):
        hid = jnp.maximum(_dot(h2, up_ref[:, c * fc:(c + 1) * fc]), 0.0)
        part = _dot(_bf(hid * hid), down_ref[c * fc:(c + 1) * fc, :])
        acc = part if acc is None else acc + part
    x2 = x1 + _mod(mod_ref, per_row, 5, d) * acc
    x_o[...] = x2
    if final:
        refs[0][...] = _rms(x2, fg_ref[...])


def _post(x, z, extra, mod, per_row, rw, wo, n2g, up, down, final_g, tm, seq_len):
    t, d = x.shape
    rwkv = rw is not None
    final = final_g is not None
    row = lambda i: (i, 0)
    tile = pl.BlockSpec((tm, d), row)
    tiles_per_seq = 1 if per_row else seq_len // tm
    mod_spec = (pl.BlockSpec((tm, N_ADA * d), row) if per_row
                else pl.BlockSpec((1, N_ADA, d), lambda i: (i // tiles_per_seq, 0, 0)))
    args = [x, z]
    specs = [tile, tile]
    if rwkv:
        args += list(extra)
        specs += [tile] * 4
    args.append(mod)
    specs.append(mod_spec)
    consts = (list(rw) if rwkv else []) + [wo, n2g, up, down] + ([final_g] if final else [])
    args += consts
    specs += [_const_spec(c.shape) for c in consts]
    act = jax.ShapeDtypeStruct((t, d), F32)
    outs = pl.pallas_call(
        functools.partial(_post_kernel, rwkv, final, per_row),
        grid=(t // tm,),
        in_specs=specs,
        out_specs=[tile] * (2 if final else 1),
        out_shape=[act] * (2 if final else 1),
        compiler_params=pltpu.CompilerParams(dimension_semantics=("parallel",), vmem_limit_bytes=VMEM_LIMIT),
        name=("post_rwkv" if rwkv else "post") + ("_final" if final else "") + ("_step" if per_row else "_seq"),
    )(*args)
    return outs


def _rope_table_kernel(base, step, invf_ref, c_o, sm_o, sp_o):
    shape = c_o.shape
    row = lax.broadcasted_iota(jnp.int32, shape, 0) + pl.program_id(0) * shape[0]
    pos = (base + step * row).astype(F32)
    ang = pos * invf_ref[...]
    lane = lax.broadcasted_iota(jnp.int32, shape, 1) % HEAD_DIM
    cos = jnp.cos(ang)
    sin = jnp.sin(ang)
    half = ROPE_DIM // 2
    c_o[...] = jnp.where(lane < ROPE_DIM, cos, 1.0)
    sm_o[...] = jnp.where(lane < half, -sin, 0.0)
    sp_o[...] = jnp.where((lane >= half) & (lane < ROPE_DIM), sin, 0.0)


def _rope_tables(n, base, step):
    half = ROPE_DIM // 2
    inv_freq = ROPE_THETA ** (-jnp.arange(half, dtype=F32) / half)
    lane = jnp.arange(LANE) % HEAD_DIM
    invf = jnp.where(lane < ROPE_DIM, inv_freq[lane % half], 0.0).reshape(1, LANE).astype(F32)
    tr = min(n, ROPE_ROWS)
    spec = pl.BlockSpec((tr, LANE), lambda i: (i, 0))
    shp = jax.ShapeDtypeStruct((n, LANE), F32)
    return pl.pallas_call(
        functools.partial(_rope_table_kernel, base, step),
        grid=(n // tr,),
        in_specs=[pl.BlockSpec((1, LANE), lambda i: (0, 0))],
        out_specs=[spec] * 3,
        out_shape=[shp] * 3,
        compiler_params=pltpu.CompilerParams(dimension_semantics=("parallel",)),
        name="rope_tables",
    )(invf)


def _qkv_kernel(per_row, x_ref, mod_ref, ng_ref, w_ref, b_ref, c_ref, sm_ref, sp_ref, q_o, k_o, v_o):
    tm, d = x_ref.shape
    nq, nk = q_o.shape[1], k_o.shape[1]
    sub = min(tm, TM_SUB)
    subs = [slice(i, i + sub) for i in range(0, tm, sub)]
    shift, scale = _mod(mod_ref, per_row, 0, d), _mod(mod_ref, per_row, 1, d)
    rows = lambda m, sl: m[sl, :] if per_row else m
    h = [_bf(_prenorm(x_ref[sl, :], ng_ref[...], rows(shift, sl), rows(scale, sl))) for sl in subs]
    qkv = [_dot(hh, w_ref[...]) + b_ref[...] for hh in h]
    for sl, z in zip(subs, qkv):
        c, sm, sp = c_ref[sl, :], sm_ref[sl, :], sp_ref[sl, :]

        def rope_into(o_ref, off, width):
            for j in range(width // LANE):
                zc = z[:, off + j * LANE:off + (j + 1) * LANE]
                o_ref[sl, j * LANE:(j + 1) * LANE] = (zc * c + pltpu.roll(zc, LANE - ROPE_DIM // 2, axis=1) * sm
                                                      + pltpu.roll(zc, ROPE_DIM // 2, axis=1) * sp)

        rope_into(q_o, 0, nq)
        rope_into(k_o, nq, nk)
        v_o[sl, :] = z[:, nq + nk:]


def _qkv(x, mod, per_row, ng, w, b, tables, tm, seq_len):
    t, d = x.shape
    nkv = KV_HEADS * HEAD_DIM
    row = lambda i: (i, 0)
    tiles_per_seq = 1 if per_row else seq_len // tm
    mod_spec = (pl.BlockSpec((tm, N_ADA * d), row) if per_row
                else pl.BlockSpec((1, N_ADA, d), lambda i: (i // tiles_per_seq, 0, 0)))
    tab_spec = pl.BlockSpec((tm, LANE), lambda i: (i % tiles_per_seq, 0))
    return pl.pallas_call(
        functools.partial(_qkv_kernel, per_row),
        grid=(t // tm,),
        in_specs=[pl.BlockSpec((tm, d), row), mod_spec, _const_spec(ng.shape), _const_spec(w.shape),
                  _const_spec(b.shape)] + [tab_spec] * 3,
        out_specs=[pl.BlockSpec((tm, d), row), pl.BlockSpec((tm, nkv), row), pl.BlockSpec((tm, nkv), row)],
        out_shape=[jax.ShapeDtypeStruct((t, d), F32), jax.ShapeDtypeStruct((t, nkv), F32),
                   jax.ShapeDtypeStruct((t, nkv), F32)],
        compiler_params=pltpu.CompilerParams(dimension_semantics=("parallel",), vmem_limit_bytes=VMEM_LIMIT),
        name="qkv_step" if per_row else "qkv_seq",
    )(x, mod, ng, w, b, *tables)


def _swa_seq_kernel(q_ref, kc_ref, kp_ref, vc_ref, vp_ref, sink_ref, o_ref):
    n = pl.program_id(1)
    q = q_ref[...]
    kall = jnp.concatenate([kp_ref[...], kc_ref[...]], axis=0)
    vall = jnp.concatenate([vp_ref[...], vc_ref[...]], axis=0)
    blk = ATT_BLOCK
    j = lax.broadcasted_iota(jnp.int32, (2 * blk, GROUP * blk), 0)
    i = lax.broadcasted_iota(jnp.int32, (2 * blk, GROUP * blk), 1) % blk
    mask = (j >= i + (blk - WINDOW)) & (j <= i + blk) & ((n > 0) | (j >= blk))
    kvs = range(KV_HEADS)
    hsl = [slice(h * HEAD_DIM, (h + 1) * HEAD_DIM) for h in range(KV_HEADS * GROUP)]
    eye_d = (lax.broadcasted_iota(jnp.int32, (HEAD_DIM, HEAD_DIM), 0)
             == lax.broadcasted_iota(jnp.int32, (HEAD_DIM, HEAD_DIM), 1)).astype(BF16)
    eye_q = (lax.broadcasted_iota(jnp.int32, (blk, blk), 0)
             == lax.broadcasted_iota(jnp.int32, (blk, blk), 1)).astype(BF16)
    ones = jnp.ones((16, 2 * blk), BF16)
    scale = HEAD_DIM ** -0.5
    kh = [_bf(kall[:, hsl[c]]) for c in kvs]
    vte = [jnp.concatenate([_bf(_dot_nt(eye_d, _bf(vall[:, hsl[c]]))), ones], axis=0) for c in kvs]
    qg = [_bf(jnp.concatenate([q[:, hsl[c * GROUP + g]] for g in range(GROUP)], axis=0) * scale) for c in kvs]
    s = [jnp.where(mask, _dot_nt(kh[c], qg[c]), -jnp.inf) for c in kvs]
    sink = [jnp.concatenate([jnp.broadcast_to(sink_ref[0:1, h:h + 1], (1, blk))
                             for h in range(c * GROUP, (c + 1) * GROUP)], axis=1) for c in kvs]
    m = [jnp.maximum(jnp.max(s[c], axis=0, keepdims=True), sink[c]) for c in kvs]
    p = [_bf(jnp.exp(s[c] - m[c])) for c in kvs]
    oe = [_dot(vte[c], p[c]) for c in kvs]
    ot = [_bf(oe[c][:HEAD_DIM] / (oe[c][HEAD_DIM:HEAD_DIM + 1] + jnp.exp(sink[c] - m[c]))) for c in kvs]
    for c in kvs:
        for pair in range(GROUP // 2):
            both = jnp.concatenate([ot[c][:, (2 * pair + e) * blk:(2 * pair + e + 1) * blk] for e in range(2)],
                                   axis=0)
            col = (c * GROUP + 2 * pair) * HEAD_DIM
            o_ref[:, col:col + 2 * HEAD_DIM] = _dot_nt(eye_q, both).astype(o_ref.dtype)


def _swa_seq(q, k, v, sink, batch, seq_len):
    t, d = q.shape
    nkv = k.shape[1]
    nb = seq_len // ATT_BLOCK
    cur = lambda b, n: (b * nb + n, 0)
    prv = lambda b, n: (b * nb + jnp.maximum(n - 1, 0), 0)
    return pl.pallas_call(
        _swa_seq_kernel,
        grid=(batch, nb),
        in_specs=[pl.BlockSpec((ATT_BLOCK, d), cur),
                  pl.BlockSpec((ATT_BLOCK, nkv), cur), pl.BlockSpec((ATT_BLOCK, nkv), prv),
                  pl.BlockSpec((ATT_BLOCK, nkv), cur), pl.BlockSpec((ATT_BLOCK, nkv), prv),
                  pl.BlockSpec(sink.shape, lambda b, n: (0, 0))],
        out_specs=pl.BlockSpec((ATT_BLOCK, d), cur),
        out_shape=jax.ShapeDtypeStruct((t, d), BF16),
        compiler_params=pltpu.CompilerParams(dimension_semantics=("parallel", "parallel")),
        name="swa_seq",
    )(q, k, k, v, v, sink)


def _swa_step_kernel(q_ref, kn_ref, vn_ref, knt_ref, vnt_ref, kc_ref, vc_ref, sink_ref, o_ref, ko_ref, vo_ref):
    nb, _, w = kc_ref.shape
    kc = kc_ref[...]
    vc = vc_ref[...]
    kn = kn_ref[...]
    vn = vn_ref[...]
    q = q_ref[...]
    scale = HEAD_DIM ** -0.5
    kvs = range(KV_HEADS)
    ksl = [slice(h * HEAD_DIM, (h + 1) * HEAD_DIM) for h in kvs]
    gsl = [slice(h * GROUP, (h + 1) * GROUP) for h in kvs]
    qh = [_bf(q[:, gsl[h], :]) for h in kvs]
    s_old = [jnp.einsum("bgd,bdw->bgw", qh[h], _bf(kc[:, ksl[h], :]), preferred_element_type=F32) * scale
             for h in kvs]
    s_new = [jnp.sum(qh[h].astype(F32) * _bf(kn[:, :, ksl[h]]).astype(F32), axis=-1, keepdims=True) * scale
             for h in kvs]
    sink = [sink_ref[h] for h in kvs]
    m = [jnp.maximum(jnp.maximum(jnp.max(s_old[h], axis=-1, keepdims=True), s_new[h]), sink[h]) for h in kvs]
    p_old = [jnp.exp(s_old[h] - m[h]) for h in kvs]
    p_new = [jnp.exp(s_new[h] - m[h]) for h in kvs]
    den = [jnp.sum(p_old[h], axis=-1, keepdims=True) + p_new[h] + jnp.exp(sink[h] - m[h]) for h in kvs]
    o = [jnp.einsum("bgw,bdw->bgd", _bf(p_old[h] / den[h]), _bf(vc[:, ksl[h], :]), preferred_element_type=F32)
         for h in kvs]
    for h in kvs:
        o_ref[:, gsl[h], :] = o[h] + _bf(p_new[h] / den[h]).astype(F32) * _bf(vn[:, :, ksl[h]]).astype(F32)
    last = lax.broadcasted_iota(jnp.int32, kc.shape, 2) == w - 1

    def column(t_ref):
        t = t_ref[0]
        return jnp.stack([jnp.broadcast_to(t[:, b:b + 1], t.shape[:1] + (w,)) for b in range(nb)])

    ko_ref[...] = jnp.where(last, column(knt_ref), pltpu.roll(kc, w - 1, axis=2))
    vo_ref[...] = jnp.where(last, column(vnt_ref), pltpu.roll(vc, w - 1, axis=2))


def _swa_step(q, kn, vn, cache_k, cache_v, sink):
    b, nh, hd = q.shape
    nkv, w = cache_k.shape[1], cache_k.shape[2]
    nb = min(b, DECODE_ROWS)
    i3 = lambda i: (i, 0, 0)
    cspec = pl.BlockSpec((nb, nkv, w), i3)
    nspec = pl.BlockSpec((nb, 1, nkv), i3)
    tspec = pl.BlockSpec((1, nkv, nb), i3)
    qspec = pl.BlockSpec((nb, nh, hd), i3)
    cols = lambda z: z.reshape(b // nb, nb, nkv).transpose(0, 2, 1)
    return pl.pallas_call(
        _swa_step_kernel,
        grid=(b // nb,),
        in_specs=[qspec, nspec, nspec, tspec, tspec, cspec, cspec, pl.BlockSpec(sink.shape, lambda i: (0, 0, 0))],
        out_specs=[qspec, cspec, cspec],
        out_shape=[jax.ShapeDtypeStruct(q.shape, F32), jax.ShapeDtypeStruct(cache_k.shape, F32),
                   jax.ShapeDtypeStruct(cache_v.shape, F32)],
        compiler_params=pltpu.CompilerParams(dimension_semantics=("parallel",)),
        name="swa_step",
    )(q, kn.reshape(b, 1, nkv), vn.reshape(b, 1, nkv), cols(kn), cols(vn), cache_k, cache_v, sink)


def kernel(x_prompt, x_sample, c_prompt, c_sample, state_wkv, state_shift, cache_k, cache_v, norm1_g, norm2_g, ada_w, ada_b, mlp_up, mlp_down, final_g, rw_mix, rw_wr, rw_wk, rw_wv, rw_wo, rw_w0, rw_w1, rw_w2, rw_a0, rw_a1, rw_a2, rw_g1, rw_g2, rw_kk, rw_ka, rw_rk, rw_lnx_g, rw_lnx_b, at_wqkv, at_bqkv, at_wo, at_sink):
    bp, seq, d = x_prompt.shape
    bs = x_sample.shape[0]
    assert x_sample.shape[1] == 1 and d % GW == 0 and d == rw_wr.shape[1]
    for rows in (TM_PROJ, TM_POST, TM_POST_RWKV, WKV_CHUNKS_PER_STEP * CHUNK, ATT_BLOCK):
        assert seq % rows == 0, (seq, rows)
    assert bs % 8 == 0 and cache_k.shape[2] <= WINDOW
    nh = d // HEAD_DIM
    tp = bp * seq
    row = lambda vec: vec.reshape(1, -1)

    n_c = bp + bs
    pad = (-n_c) % 8
    c_all = jnp.concatenate([c_prompt, c_sample, jnp.zeros((pad, d), F32)], axis=0)
    ada = _ada(c_all, ada_w, ada_b)
    mod_p = [ada[i, :bp].reshape(bp, N_ADA, d) for i in range(2)]
    mod_s = [ada[i, bp:bp + bs] for i in range(2)]

    xp = x_prompt.reshape(tp, d)
    xs = x_sample.reshape(bs, d)

    head_id = jnp.arange(GW) // HEAD_DIM
    ones_bd = (head_id[:, None] == head_id[None, :]).astype(BF16)

    rwp = dict(mix=rw_mix[0], wr=_bf(rw_wr[0]), wk=_bf(rw_wk[0]), wv=_bf(rw_wv[0]),
               w1=_bf(rw_w1[0]), w2=_bf(rw_w2[0]), w0=row(rw_w0[0]),
               a1=_bf(rw_a1[0]), a2=_bf(rw_a2[0]), a0=row(rw_a0[0]),
               g1=_bf(rw_g1[0]), g2=_bf(rw_g2[0]), kk=row(rw_kk[0]), ka=row(rw_ka[0]), ones=ones_bd)
    n1 = row(norm1_g[0])
    post_rw = (row(rw_lnx_g[0]), row(rw_lnx_b[0]), row(rw_rk[0]), ones_bd)
    wo0, up0, down0 = _bf(rw_wo[0]), _bf(mlp_up[0]), _bf(mlp_down[0])
    n2 = row(norm2_g[0])

    r, k, v, lw, kk, a, g, hlast = _rwkv_proj(xp, None, mod_p[0], n1, rwp, TM_PROJ, seq)
    shift_p = hlast.reshape(bp, seq // TM_PROJ, 8, d)[:, -1, -1]
    y, st = _wkv_seq(r, k, v, lw, kk, a, bp, seq)
    wkv_p = st.reshape(bp, d // GW, HEAD_DIM, HPG, HEAD_DIM).transpose(0, 1, 3, 2, 4).reshape(bp, nh, HEAD_DIM, HEAD_DIM)
    (xp,) = _post(xp, y, (r, k, v, g), mod_p[0], False, post_rw, wo0, n2, up0, down0, None, TM_POST_RWKV, seq)

    rs, ks, vs, lws, kks, as_, gs, shift_s = _rwkv_proj(xs, state_shift[0], mod_s[0], n1, rwp, bs, 1)
    vecs = jnp.stack([rs, jnp.exp(lws), ks, vs, -kks, kks * as_]).transpose(0, 2, 1).reshape(6, nh, HEAD_DIM, bs)
    st_s, ys = _wkv_step(state_wkv[0].transpose(1, 2, 3, 0), *vecs)
    wkv_s = st_s.transpose(3, 0, 1, 2)
    (xs,) = _post(xs, ys.reshape(d, bs).T, (rs, ks, vs, gs), mod_s[0], True, post_rw, wo0, n2, up0, down0, None, bs, 1)

    wqkv, bqkv, wo1 = _bf(at_wqkv[0]), row(at_bqkv[0]), _bf(at_wo[0])
    n1, n2 = row(norm1_g[1]), row(norm2_g[1])
    up1, down1 = _bf(mlp_up[1]), _bf(mlp_down[1])
    fg = row(final_g)
    nkv = KV_HEADS * HEAD_DIM

    q, kq, vq = _qkv(xp, mod_p[1], False, n1, wqkv, bqkv, _rope_tables(seq, 0, 1), TM_PROJ, seq)
    o = _swa_seq(q, kq, vq, row(at_sink[0]), bp, seq)
    keep = min(WINDOW, seq)
    k_p = kq.reshape(bp, seq, nkv)[:, seq - keep:].reshape(bp, keep, KV_HEADS, HEAD_DIM)
    v_p = vq.reshape(bp, seq, nkv)[:, seq - keep:].reshape(bp, keep, KV_HEADS, HEAD_DIM)
    xp, y_p = _post(xp, o, None, mod_p[1], False, None, wo1, n2, up1, down1, fg, TM_POST, seq)

    qs, kn, vn = _qkv(xs, mod_s[1], True, n1, wqkv, bqkv, _rope_tables(bs, PAST_LEN, 0), bs, 1)
    w_buf = cache_k.shape[2]
    to_minor = lambda c: c.transpose(0, 2, 3, 1).reshape(bs, nkv, w_buf)
    from_minor = lambda c: c.reshape(bs, KV_HEADS, HEAD_DIM, w_buf).transpose(0, 3, 1, 2)
    os_, k_s, v_s = _swa_step(qs.reshape(bs, nh, HEAD_DIM), kn, vn, to_minor(cache_k[0]), to_minor(cache_v[0]),
                              at_sink[0].reshape(KV_HEADS, GROUP, 1))
    xs, y_s = _post(xs, os_.reshape(bs, d), None, mod_s[1], True, None, wo1, n2, up1, down1, fg, bs, 1)

    return (y_p.reshape(bp, seq, d), y_s.reshape(bs, 1, d),
            wkv_p[None], wkv_s[None], shift_p[None], shift_s[None],
            k_p[None], from_minor(k_s)[None], v_p[None], from_minor(v_s)[None])
```

```python
import functools
import math

import jax
import jax.numpy as jnp
from jax import lax
from jax.experimental import pallas as pl
from jax.experimental.pallas import tpu as pltpu

F32 = jnp.float32
BF16 = jnp.bfloat16

HEAD_DIM = 64
KV_HEADS = 4
GROUP = 4
WINDOW = 128
ATT_BLOCK = 128
ROPE_DIM = HEAD_DIM // 4
ROPE_THETA = 500000.0
PAST_LEN = 8192
NORM_EPS = 1e-6
LNX_EPS = 64e-5
N_ADA = 6

LANE = 128
V7X_VMEM_BYTES = 64 * 1024 * 1024
VMEM_LIMIT = V7X_VMEM_BYTES * 7 // 8

CHUNK = 64
HPG = 2
GW = HPG * HEAD_DIM

TM_PROJ = 512
TM_SUB = 256
TM_POST_RWKV = 256
TM_POST = 512
MLP_FF_CHUNK = 1024
WKV_CHUNKS_PER_STEP = 4
ADA_COLS = 1024
ROPE_ROWS = 512
DECODE_ROWS = 16
SWA_BLOCKS_PER_STEP = 4


def _bf(x):
    return x.astype(BF16)


def _dot(a, b):
    return jnp.dot(a, b, preferred_element_type=F32)


def _dot_nt(a, b):
    return lax.dot_general(a, b, (((1,), (1,)), ((), ())), preferred_element_type=F32)


def _rms(x, g):
    ms = jnp.mean(x * x, axis=-1, keepdims=True)
    return x * lax.rsqrt(ms + NORM_EPS) * g


def _prenorm(x, g, shift, scale):
    return _rms(x, g) * (1.0 + scale) + shift


def _mod(mod_ref, per_row, idx, d):
    if per_row:
        return mod_ref[:, idx * d:(idx + 1) * d]
    return mod_ref[0, idx:idx + 1, :]


def _pieces(x, n):
    out = []
    rem = x
    for i in range(n):
        p = rem.astype(BF16)
        out.append(p)
        if i + 1 < n:
            rem = rem - p.astype(F32)
    return out


def _head_sum(val, ones):
    gw = ones.shape[0]
    cols = [_dot(_bf(val[:, j * gw:(j + 1) * gw]), ones) for j in range(val.shape[1] // gw)]
    return jnp.concatenate(cols, axis=1)


def _ada_kernel(c_ref, w_ref, b_ref, o_ref):
    c = c_ref[...]
    s = c * jax.nn.sigmoid(c)
    o_ref[0] = _dot(_bf(s), _bf(w_ref[0])) + b_ref[0]


def _ada(c_all, ada_w, ada_b):
    depth, d, n = ada_w.shape
    m = c_all.shape[0]
    tn = ADA_COLS
    return pl.pallas_call(
        _ada_kernel,
        grid=(depth, n // tn),
        in_specs=[
            pl.BlockSpec((m, d), lambda i, j: (0, 0)),
            pl.BlockSpec((1, d, tn), lambda i, j: (i, 0, j)),
            pl.BlockSpec((1, 1, tn), lambda i, j: (i, 0, j)),
        ],
        out_specs=pl.BlockSpec((1, m, tn), lambda i, j: (i, 0, j)),
        out_shape=jax.ShapeDtypeStruct((depth, m, n), F32),
        compiler_params=pltpu.CompilerParams(dimension_semantics=("parallel", "parallel")),
        name="ada",
    )(c_all, ada_w, ada_b.reshape(depth, 1, n))


def _rwkv_proj_kernel(per_row, tiles_per_seq,
                      x_ref, xp_ref, mod_ref, ng_ref, mix_ref, wr_ref, wk_ref, wv_ref,
                      w1_ref, w2_ref, w0_ref, a1_ref, a2_ref, a0_ref, g1_ref, g2_ref,
                      kk_ref, ka_ref, ones_ref,
                      r_o, k_o, v_o, lw_o, kk_o, a_o, g_o, h_o, h_scr, xx_scr):
    d = x_ref.shape[1]
    tm = x_ref.shape[0]
    ng = ng_ref[...]
    shift = _mod(mod_ref, per_row, 0, d)
    scale = _mod(mod_ref, per_row, 1, d)
    h = _prenorm(x_ref[...], ng, shift, scale)
    if per_row:
        prev = xp_ref[...]
        h_o[...] = h
    else:
        hp = _prenorm(xp_ref[...], ng, shift, scale)
        first = (pl.program_id(0) % tiles_per_seq) == 0
        prow = jnp.where(first, 0.0, hp[7:8, :])
        rowid = lax.broadcasted_iota(jnp.int32, h.shape, 0)
        prev = jnp.where(rowid == 0, prow, pltpu.roll(h, 1, axis=0))
        h_o[0] = h[tm - 8:tm, :]
    h_scr[...] = h
    xx_scr[...] = prev - h

    sub = min(tm, TM_SUB)
    subs = [slice(i, i + sub) for i in range(0, tm, sub)]

    def mixed(j):
        return [_bf(h_scr[sl, :] + xx_scr[sl, :] * mix_ref[j:j + 1, :]) for sl in subs]

    def each(f, *cols):
        return [f(*vals) for vals in zip(*cols)]

    for sl, r in zip(subs, each(lambda m: _dot(m, wr_ref[...]), mixed(0))):
        r_o[sl, :] = r
    t1 = each(lambda m: _bf(jnp.tanh(_dot(m, w1_ref[...]))), mixed(1))
    for sl, wl in zip(subs, each(lambda t: w0_ref[...] + _dot(t, w2_ref[...]), t1)):
        lw_o[sl, :] = jax.nn.sigmoid(wl) * (-math.exp(-0.5))
    t4 = each(lambda m: _bf(_dot(m, a1_ref[...])), mixed(4))
    a = each(lambda t: jax.nn.sigmoid(a0_ref[...] + _dot(t, a2_ref[...])), t4)
    for sl, ai in zip(subs, a):
        a_o[sl, :] = ai
    k = each(lambda m: _dot(m, wk_ref[...]), mixed(2))
    kk = each(lambda ki: ki * kk_ref[...], k)
    ss = each(lambda kki: _head_sum(kki * kki, ones_ref[...]), kk)
    for sl, ki, ai, kki, ssi in zip(subs, k, a, kk, ss):
        k_o[sl, :] = ki * (1.0 + (ai - 1.0) * ka_ref[...])
        kk_o[sl, :] = kki * lax.rsqrt(jnp.maximum(ssi, 1e-24))
    for sl, v in zip(subs, each(lambda m: _dot(m, wv_ref[...]), mixed(3))):
        v_o[sl, :] = v
    t5 = each(lambda m: _bf(jax.nn.sigmoid(_dot(m, g1_ref[...]))), mixed(5))
    for sl, g in zip(subs, each(lambda t: _dot(t, g2_ref[...]), t5)):
        g_o[sl, :] = g


def _const_spec(shape):
    nd = len(shape)
    return pl.BlockSpec(shape, lambda *_: (0,) * nd, pipeline_mode=pl.Buffered(1))


def _rwkv_proj(x, prev_rows, mod, ng, p, tm, seq_len):
    t, d = x.shape
    per_row = prev_rows is not None
    nt = t // tm
    tiles_per_seq = 1 if per_row else seq_len // tm
    row = lambda i: (i, 0)
    if per_row:
        xp, xp_spec = prev_rows, pl.BlockSpec((tm, d), row)
        mod_spec = pl.BlockSpec((tm, N_ADA * d), row)
        h_shape, h_spec = jax.ShapeDtypeStruct((t, d), F32), pl.BlockSpec((tm, d), row)
    else:
        xp, xp_spec = x, pl.BlockSpec((8, d), lambda i: (jnp.maximum(i * (tm // 8) - 1, 0), 0))
        mod_spec = pl.BlockSpec((1, N_ADA, d), lambda i: (i // tiles_per_seq, 0, 0))
        h_shape, h_spec = jax.ShapeDtypeStruct((nt, 8, d), F32), pl.BlockSpec((1, 8, d), lambda i: (i, 0, 0))
    consts = [ng, p["mix"], p["wr"], p["wk"], p["wv"], p["w1"], p["w2"], p["w0"], p["a1"], p["a2"], p["a0"],
              p["g1"], p["g2"], p["kk"], p["ka"], p["ones"]]
    act = jax.ShapeDtypeStruct((t, d), F32)
    return pl.pallas_call(
        functools.partial(_rwkv_proj_kernel, per_row, tiles_per_seq),
        grid=(nt,),
        in_specs=[pl.BlockSpec((tm, d), row), xp_spec, mod_spec] + [_const_spec(c.shape) for c in consts],
        out_specs=[pl.BlockSpec((tm, d), row)] * 7 + [h_spec],
        out_shape=[act] * 7 + [h_shape],
        scratch_shapes=[pltpu.VMEM((tm, d), F32)] * 2,
        compiler_params=pltpu.CompilerParams(dimension_semantics=("parallel",), vmem_limit_bytes=VMEM_LIMIT),
        name="rwkv_proj_step" if per_row else "rwkv_proj_seq",
    )(x, xp, mod, *consts)


def _head_of_lane(shape, axis):
    return lax.broadcasted_iota(jnp.int32, shape, axis) // HEAD_DIM


def _bd(p, low_half):
    zero = jnp.zeros((), p.dtype)
    zeros = jnp.zeros((p.shape[0], LANE), p.dtype)
    per_col = LANE // HEAD_DIM
    ncol = GW // LANE
    blocks = []
    for h in range(HPG):
        c = h // per_col
        col = p[:, c * LANE:(c + 1) * LANE]
        col = jnp.where(low_half, col, zero) if h % per_col == 0 else jnp.where(low_half, zero, col)
        blocks.append(jnp.concatenate([col if j == c else zeros for j in range(ncol)], axis=1))
    return jnp.concatenate(blocks, axis=0)


def _wkv_prep_chunks(rs, ks, vs, lws, kks, as_, masks):
    tri, diag, strict, incl, eye, blk8, levels, lane_head = masks
    n = len(rs)

    def each(f, *args):
        return [f(*[a[i] for a in args]) for i in range(n)]

    def pp(xs, ys):
        rhs = each(lambda y: _bd(_bf(y), diag), ys)
        return each(lambda x, b: _dot(_bf(x), b), xs, rhs)

    def pick_diag(gram):
        out = gram[0:HEAD_DIM, :]
        for hh in range(1, HPG):
            out = jnp.where(lane_head == hh, gram[hh * HEAD_DIM:(hh + 1) * HEAD_DIM, :], out)
        return out

    def add(xs, ys):
        return each(lambda x, y: x + y, xs, ys)

    def masked(m, xs):
        return each(lambda x: jnp.where(m, x, 0.0), xs)

    c2 = each(lambda lw: _dot(tri, jnp.concatenate(_pieces(lw, 2), axis=1)), lws)
    c = each(lambda x: x[:, :GW] + x[:, GW:], c2)
    c_last = each(lambda ci: ci[CHUNK - 1:CHUNK, :], c)
    e_neg = each(lambda ci: jnp.exp(-ci), c)
    e_end = each(lambda ci, cl: jnp.exp(cl - ci), c, c_last)
    bv = each(lambda kk, a: kk * a, kks, as_)
    a_t = each(lambda kk, ci, lw: _bf(-kk * jnp.exp(ci - lw)), kks, c, lws)
    r_t = each(lambda r, ci: r * jnp.exp(ci), rs, c)
    b_t = each(lambda b, e: _bf(b * e), bv, e_neg)
    k_t = each(lambda k, e: _bf(k * e), ks, e_neg)
    b_e = each(lambda b, e: _bf(b * e), bv, e_end)
    k_e = each(lambda k, e: _bf(k * e), ks, e_end)

    bd = lambda y: _bd(_bf(y), diag)
    ar = each(lambda x, y: jnp.concatenate([x, _bf(y)], axis=0), a_t, r_t)
    xbk = each(lambda x, b, k: _dot_nt(x, jnp.concatenate([bd(b), bd(k)], axis=0)), ar, b_t, k_t)
    a_ab = each(lambda x: jnp.where(strict, x[:CHUNK, :GW], 0.0), xbk)
    a_rb = each(lambda x: jnp.where(incl, x[CHUNK:, :GW], 0.0), xbk)
    a_ak = each(lambda x: jnp.where(strict, x[:CHUNK, GW:], 0.0), xbk)
    a_rk = each(lambda x: jnp.where(incl, x[CHUNK:, GW:], 0.0), xbk)
    av = pp(a_ak, vs)

    n0 = masked(blk8, a_ab)
    n2 = pp(n0, n0)
    both = pp(each(lambda x, y: jnp.concatenate([x, eye + y], axis=0), n2, n0), n2)
    n4 = each(lambda x: x[:CHUNK], both)
    t01 = each(lambda x, y: eye + x + y[CHUNK:], n0, both)
    tinv = add(t01, pp(t01, n4))
    b = 8
    for lvl in levels:
        lower = lambda x: jnp.concatenate([x[r0:r0 + b] for r0 in range(b, CHUNK, 2 * b)], axis=0)
        z = pp(pp(each(lower, tinv), masked(lvl, a_ab)), tinv)

        def merged(t, zi):
            rows = []
            for j, r0 in enumerate(range(0, CHUNK, 2 * b)):
                rows += [t[r0:r0 + b], t[r0 + b:r0 + 2 * b] + zi[j * b:(j + 1) * b]]
            return jnp.concatenate(rows, axis=0)

        tinv = each(merged, tinv, z)
        b *= 2

    w12 = each(lambda t, a, x: _dot(_bf(t), jnp.concatenate([bd(a), bd(x)], axis=1)), tinv, a_t, av)
    w1 = each(lambda w: w[:, :GW], w12)
    w2 = each(lambda w: w[:, GW:], w12)
    zeros = jnp.zeros((GW, GW), BF16)

    def qy_rhs(wa, wb, v):
        return jnp.concatenate([jnp.concatenate([bd(wa), bd(wb)], axis=1),
                                jnp.concatenate([zeros, bd(v)], axis=1)], axis=0)

    qy = each(lambda x1, x2, wa, wb, v: _dot(jnp.concatenate([_bf(x1), _bf(x2)], axis=1), qy_rhs(wa, wb, v)),
              a_rb, a_rk, w1, w2, vs)
    qc = each(lambda r, x: r + x[:, :GW], r_t, qy)
    y0 = each(lambda x: x[:, GW:], qy)
    stack = lambda x, y: jnp.concatenate([x, y], axis=0)
    lhs = each(lambda wa, wb, v: _bf(jnp.concatenate([stack(wa, jnp.zeros_like(wa)).T, stack(wb, v).T], axis=0)),
               w1, w2, vs)
    gram = each(lambda x, b, k: _dot(x, stack(b, k)), lhs, b_e, k_e)
    mc = each(lambda gm: pick_diag(gm[:GW]), gram)
    cc = each(lambda gm: pick_diag(gm[GW:]), gram)
    return qc, y0, mc, cc, c_last


def _wkv_masks():
    t = lax.broadcasted_iota(jnp.int32, (CHUNK, GW), 0)
    s = lax.broadcasted_iota(jnp.int32, (CHUNK, GW), 1) % HEAD_DIM
    tri = (lax.broadcasted_iota(jnp.int32, (CHUNK, CHUNK), 1)
           <= lax.broadcasted_iota(jnp.int32, (CHUNK, CHUNK), 0)).astype(BF16)
    diag = lax.broadcasted_iota(jnp.int32, (CHUNK, LANE), 1) < HEAD_DIM
    strict = s < t
    incl = s <= t
    eye = (s == t).astype(F32)
    blk8 = strict & ((s // 8) == (t // 8))
    levels = []
    b = 8
    while b < CHUNK:
        levels.append(((s // (2 * b)) == (t // (2 * b))) & ((s // b) != (t // b)) & strict)
        b *= 2
    lane_head = _head_of_lane((HEAD_DIM, GW), 1)
    return tri, diag, strict, incl, eye, blk8, levels, lane_head


def _rwkv_epilogue(y, r, k, v, g, lg, lb, rk, ones):
    inv_n = 1.0 / HEAD_DIM
    mu = _head_sum(y, ones) * inv_n
    dy = y - mu
    var = _head_sum(dy * dy, ones) * inv_n
    yn = dy * lax.rsqrt(var + LNX_EPS) * lg + lb
    bonus = _head_sum(r * k * rk, ones) * v
    return (yn + bonus) * g


def _wkv_seq_kernel(nsub, ng, r_ref, k_ref, v_ref, lw_ref, kk_ref, a_ref, y_o, s_o, s_scr):
    step = pl.program_id(1)

    @pl.when(step == 0)
    def _():
        s_scr[...] = jnp.zeros_like(s_scr)

    masks = _wkv_masks()
    diag = masks[1]
    sls = [slice(u * CHUNK, (u + 1) * CHUNK) for u in range(nsub)]
    gls = [slice(g * GW, (g + 1) * GW) for g in range(ng)]
    states = [s_scr[:, gl] for gl in gls]

    chunks = lambda ref: [ref[sls[u], gls[g]] for u in range(nsub) for g in range(ng)]
    qc, y0, mc, cc, cl = _wkv_prep_chunks(chunks(r_ref), chunks(k_ref), chunks(v_ref), chunks(lw_ref),
                                          chunks(kk_ref), chunks(a_ref), masks)
    for u in range(nsub):
        sb = [_bf(s) for s in states]
        upd = [_dot(sb[g], _bd(_bf(mc[u * ng + g]), diag)) for g in range(ng)]
        yy = [_dot_nt(_bf(qc[u * ng + g]), _bd(sb[g], diag)) for g in range(ng)]
        for g in range(ng):
            i = u * ng + g
            states[g] = states[g] * jnp.exp(cl[i]) + upd[g] + cc[i]
            y_o[sls[u], gls[g]] = yy[g] + y0[i]
    for g, gl in enumerate(gls):
        s_scr[:, gl] = states[g]

    @pl.when(step == pl.num_programs(1) - 1)
    def _():
        for g in range(ng):
            s_o[0, g] = states[g]


def _wkv_seq(r, k, v, lw, kk, a, batch, seq_len):
    t, d = r.shape
    nsub = WKV_CHUNKS_PER_STEP
    rows = nsub * CHUNK
    ng = d // GW
    steps = seq_len // rows
    blk = pl.BlockSpec((rows, d), lambda i, j: (i * steps + j, 0))
    return pl.pallas_call(
        functools.partial(_wkv_seq_kernel, nsub, ng),
        grid=(batch, steps),
        in_specs=[blk] * 6,
        out_specs=[blk, pl.BlockSpec((1, ng, HEAD_DIM, GW), lambda i, j: (i, 0, 0, 0))],
        out_shape=[jax.ShapeDtypeStruct((t, d), F32), jax.ShapeDtypeStruct((batch, ng, HEAD_DIM, GW), F32)],
        scratch_shapes=[pltpu.VMEM((HEAD_DIM, d), F32)],
        compiler_params=pltpu.CompilerParams(dimension_semantics=("parallel", "arbitrary"),
                                             vmem_limit_bytes=VMEM_LIMIT),
        name="wkv_seq",
    )(r, k, v, lw, kk, a)


def _wkv_step_kernel(s_ref, r_ref, w_ref, k_ref, v_ref, a_ref, b_ref, so_ref, y_ref):
    s = s_ref[0]
    sa = jnp.sum(s * a_ref[0][None], axis=1)
    s = s * w_ref[0][None] + sa[:, None, :] * b_ref[0][None] + v_ref[0][:, None, :] * k_ref[0][None]
    so_ref[0] = s
    y_ref[0] = jnp.sum(s * r_ref[0][None], axis=1)


def _wkv_step(state, r, decay, k, v, a, b):
    nh, n, _, nb = state.shape
    sspec = pl.BlockSpec((1, n, n, nb), lambda i: (i, 0, 0, 0))
    vspec = pl.BlockSpec((1, n, nb), lambda i: (i, 0, 0))
    return pl.pallas_call(
        _wkv_step_kernel,
        grid=(nh,),
        in_specs=[sspec] + [vspec] * 6,
        out_specs=[sspec, vspec],
        out_shape=[jax.ShapeDtypeStruct(state.shape, F32), jax.ShapeDtypeStruct((nh, n, nb), F32)],
        compiler_params=pltpu.CompilerParams(dimension_semantics=("parallel",)),
        name="wkv_step",
    )(state, r, decay, k, v, a, b)


def _post_kernel(rwkv, final, per_row, *refs):
    refs = list(refs)
    x_ref = refs.pop(0)
    z_ref = refs.pop(0)
    if rwkv:
        r_ref, k_ref, v_ref, g_ref = refs[:4]
        refs = refs[4:]
    mod_ref = refs.pop(0)
    if rwkv:
        lg_ref, lb_ref, rk_ref, ones_ref = refs[:4]
        refs = refs[4:]
    wo_ref, n2_ref, up_ref, down_ref = refs[:4]
    refs = refs[4:]
    if final:
        fg_ref = refs.pop(0)
    x_o = refs.pop(0)
    d = x_ref.shape[1]

    z = z_ref[...]
    if rwkv:
        z = _rwkv_epilogue(z, r_ref[...], k_ref[...], v_ref[...], g_ref[...],
                           lg_ref[...], lb_ref[...], rk_ref[...], ones_ref[...])
    out = _dot(_bf(z), wo_ref[...])
    x1 = x_ref[...] + _mod(mod_ref, per_row, 2, d) * out
    h2 = _bf(_prenorm(x1, n2_ref[...], _mod(mod_ref, per_row, 3, d), _mod(mod_ref, per_row, 4, d)))
    dff = up_ref.shape[1]
    fc = MLP_FF_CHUNK
    acc = None
    for c in range(dff // fc):
        hid = jnp.maximum(_dot(h2, up_ref[:, c * fc:(c + 1) * fc]), 0.0)
        part = _dot(_bf(hid * hid), down_ref[c * fc:(c + 1) * fc, :])
        acc = part if acc is None else acc + part
    x2 = x1 + _mod(mod_ref, per_row, 5, d) * acc
    x_o[...] = x2
    if final:
        refs[0][...] = _rms(x2, fg_ref[...])


def _post(x, z, extra, mod, per_row, rw, wo, n2g, up, down, final_g, tm, seq_len):
    t, d = x.shape
    rwkv = rw is not None
    final = final_g is not None
    row = lambda i: (i, 0)
    tile = pl.BlockSpec((tm, d), row)
    tiles_per_seq = 1 if per_row else seq_len // tm
    mod_spec = (pl.BlockSpec((tm, N_ADA * d), row) if per_row
                else pl.BlockSpec((1, N_ADA, d), lambda i: (i // tiles_per_seq, 0, 0)))
    args = [x, z]
    specs = [tile, tile]
    if rwkv:
        args += list(extra)
        specs += [tile] * 4
    args.append(mod)
    specs.append(mod_spec)
    consts = (list(rw) if rwkv else []) + [wo, n2g, up, down] + ([final_g] if final else [])
    args += consts
    specs += [_const_spec(c.shape) for c in consts]
    act = jax.ShapeDtypeStruct((t, d), F32)
    outs = pl.pallas_call(
        functools.partial(_post_kernel, rwkv, final, per_row),
        grid=(t // tm,),
        in_specs=specs,
        out_specs=[tile] * (2 if final else 1),
        out_shape=[act] * (2 if final else 1),
        compiler_params=pltpu.CompilerParams(dimension_semantics=("parallel",), vmem_limit_bytes=VMEM_LIMIT),
        name=("post_rwkv" if rwkv else "post") + ("_final" if final else "") + ("_step" if per_row else "_seq"),
    )(*args)
    return outs


def _rope_table_kernel(base, step, invf_ref, c_o, sm_o, sp_o):
    shape = c_o.shape
    row = lax.broadcasted_iota(jnp.int32, shape, 0) + pl.program_id(0) * shape[0]
    pos = (base + step * row).astype(F32)
    ang = pos * invf_ref[...]
    lane = lax.broadcasted_iota(jnp.int32, shape, 1) % HEAD_DIM
    cos = jnp.cos(ang)
    sin = jnp.sin(ang)
    half = ROPE_DIM // 2
    c_o[...] = jnp.where(lane < ROPE_DIM, cos, 1.0)
    sm_o[...] = jnp.where(lane < half, -sin, 0.0)
    sp_o[...] = jnp.where((lane >= half) & (lane < ROPE_DIM), sin, 0.0)


def _rope_tables(n, base, step):
    half = ROPE_DIM // 2
    inv_freq = ROPE_THETA ** (-jnp.arange(half, dtype=F32) / half)
    lane = jnp.arange(LANE) % HEAD_DIM
    invf = jnp.where(lane < ROPE_DIM, inv_freq[lane % half], 0.0).reshape(1, LANE).astype(F32)
    tr = min(n, ROPE_ROWS)
    spec = pl.BlockSpec((tr, LANE), lambda i: (i, 0))
    shp = jax.ShapeDtypeStruct((n, LANE), F32)
    return pl.pallas_call(
        functools.partial(_rope_table_kernel, base, step),
        grid=(n // tr,),
        in_specs=[pl.BlockSpec((1, LANE), lambda i: (0, 0))],
        out_specs=[spec] * 3,
        out_shape=[shp] * 3,
        compiler_params=pltpu.CompilerParams(dimension_semantics=("parallel",)),
        name="rope_tables",
    )(invf)


def _qkv_kernel(per_row, x_ref, mod_ref, ng_ref, w_ref, b_ref, c_ref, sm_ref, sp_ref, q_o, k_o, v_o):
    tm, d = x_ref.shape
    nq, nk = q_o.shape[1], k_o.shape[1]
    sub = min(tm, TM_SUB)
    subs = [slice(i, i + sub) for i in range(0, tm, sub)]
    shift, scale = _mod(mod_ref, per_row, 0, d), _mod(mod_ref, per_row, 1, d)
    rows = lambda m, sl: m[sl, :] if per_row else m
    h = [_bf(_prenorm(x_ref[sl, :], ng_ref[...], rows(shift, sl), rows(scale, sl))) for sl in subs]
    qkv = [_dot(hh, w_ref[...]) + b_ref[...] for hh in h]
    for sl, z in zip(subs, qkv):
        c, sm, sp = c_ref[sl, :], sm_ref[sl, :], sp_ref[sl, :]

        def rope_into(o_ref, off, width):
            for j in range(width // LANE):
                zc = z[:, off + j * LANE:off + (j + 1) * LANE]
                o_ref[sl, j * LANE:(j + 1) * LANE] = (zc * c + pltpu.roll(zc, LANE - ROPE_DIM // 2, axis=1) * sm
                                                      + pltpu.roll(zc, ROPE_DIM // 2, axis=1) * sp)

        rope_into(q_o, 0, nq)
        rope_into(k_o, nq, nk)
        v_o[sl, :] = z[:, nq + nk:]


def _qkv(x, mod, per_row, ng, w, b, tables, tm, seq_len):
    t, d = x.shape
    nkv = KV_HEADS * HEAD_DIM
    row = lambda i: (i, 0)
    tiles_per_seq = 1 if per_row else seq_len // tm
    mod_spec = (pl.BlockSpec((tm, N_ADA * d), row) if per_row
                else pl.BlockSpec((1, N_ADA, d), lambda i: (i // tiles_per_seq, 0, 0)))
    tab_spec = pl.BlockSpec((tm, LANE), lambda i: (i % tiles_per_seq, 0))
    return pl.pallas_call(
        functools.partial(_qkv_kernel, per_row),
        grid=(t // tm,),
        in_specs=[pl.BlockSpec((tm, d), row), mod_spec, _const_spec(ng.shape), _const_spec(w.shape),
                  _const_spec(b.shape)] + [tab_spec] * 3,
        out_specs=[pl.BlockSpec((tm, d), row), pl.BlockSpec((tm, nkv), row), pl.BlockSpec((tm, nkv), row)],
        out_shape=[jax.ShapeDtypeStruct((t, d), F32), jax.ShapeDtypeStruct((t, nkv), F32),
                   jax.ShapeDtypeStruct((t, nkv), F32)],
        compiler_params=pltpu.CompilerParams(dimension_semantics=("parallel",), vmem_limit_bytes=VMEM_LIMIT),
        name="qkv_step" if per_row else "qkv_seq",
    )(x, mod, ng, w, b, *tables)


def _swa_seq_kernel(q_ref, kc_ref, kp_ref, vc_ref, vp_ref, sink_ref, o_ref):
    n = pl.program_id(1)
    blk = ATT_BLOCK
    nq = q_ref.shape[0] // blk
    kall = jnp.concatenate([kp_ref[...], kc_ref[...]], axis=0)
    vall = jnp.concatenate([vp_ref[...], vc_ref[...]], axis=0)
    j = lax.broadcasted_iota(jnp.int32, (2 * blk, GROUP * blk), 0)
    i = lax.broadcasted_iota(jnp.int32, (2 * blk, GROUP * blk), 1) % blk
    band = (j >= i + (blk - WINDOW)) & (j <= i + blk)
    first = band & ((n > 0) | (j >= blk))
    hsl = [slice(h * HEAD_DIM, (h + 1) * HEAD_DIM) for h in range(KV_HEADS * GROUP)]
    for t in range(nq):
        _swa_block(q_ref[t * blk:(t + 1) * blk, :], kall[t * blk:(t + 2) * blk], vall[t * blk:(t + 2) * blk],
                   first if t == 0 else band, sink_ref, o_ref, slice(t * blk, (t + 1) * blk), hsl)


def _swa_block(q, kall, vall, mask, sink_ref, o_ref, rows, hsl):
    blk = ATT_BLOCK
    kvs = range(KV_HEADS)
    eye_d = (lax.broadcasted_iota(jnp.int32, (HEAD_DIM, HEAD_DIM), 0)
             == lax.broadcasted_iota(jnp.int32, (HEAD_DIM, HEAD_DIM), 1)).astype(BF16)
    eye_q = (lax.broadcasted_iota(jnp.int32, (blk, blk), 0)
             == lax.broadcasted_iota(jnp.int32, (blk, blk), 1)).astype(BF16)
    ones = jnp.ones((16, 2 * blk), BF16)
    scale = HEAD_DIM ** -0.5
    kh = [_bf(kall[:, hsl[c]]) for c in kvs]
    vte = [jnp.concatenate([_bf(_dot_nt(eye_d, _bf(vall[:, hsl[c]]))), ones], axis=0) for c in kvs]
    qg = [_bf(jnp.concatenate([q[:, hsl[c * GROUP + g]] for g in range(GROUP)], axis=0) * scale) for c in kvs]
    s = [jnp.where(mask, _dot_nt(kh[c], qg[c]), -jnp.inf) for c in kvs]
    sink = [jnp.concatenate([jnp.broadcast_to(sink_ref[0:1, h:h + 1], (1, blk))
                             for h in range(c * GROUP, (c + 1) * GROUP)], axis=1) for c in kvs]
    m = [jnp.maximum(jnp.max(s[c], axis=0, keepdims=True), sink[c]) for c in kvs]
    p = [_bf(jnp.exp(s[c] - m[c])) for c in kvs]
    oe = [_dot(vte[c], p[c]) for c in kvs]
    ot = [_bf(oe[c][:HEAD_DIM] / (oe[c][HEAD_DIM:HEAD_DIM + 1] + jnp.exp(sink[c] - m[c]))) for c in kvs]
    for c in kvs:
        for pair in range(GROUP // 2):
            both = jnp.concatenate([ot[c][:, (2 * pair + e) * blk:(2 * pair + e + 1) * blk] for e in range(2)],
                                   axis=0)
            col = (c * GROUP + 2 * pair) * HEAD_DIM
            o_ref[rows, col:col + 2 * HEAD_DIM] = _dot_nt(eye_q, both).astype(o_ref.dtype)


def _swa_seq(q, k, v, sink, batch, seq_len):
    t, d = q.shape
    nkv = k.shape[1]
    nq = SWA_BLOCKS_PER_STEP
    rows = nq * ATT_BLOCK
    nb = seq_len // rows
    cur = lambda b, n: (b * nb + n, 0)
    prv = lambda b, n: ((b * nb + n) * nq - jnp.minimum(n, 1), 0)
    return pl.pallas_call(
        _swa_seq_kernel,
        grid=(batch, nb),
        in_specs=[pl.BlockSpec((rows, d), cur),
                  pl.BlockSpec((rows, nkv), cur), pl.BlockSpec((ATT_BLOCK, nkv), prv),
                  pl.BlockSpec((rows, nkv), cur), pl.BlockSpec((ATT_BLOCK, nkv), prv),
                  pl.BlockSpec(sink.shape, lambda b, n: (0, 0))],
        out_specs=pl.BlockSpec((rows, d), cur),
        out_shape=jax.ShapeDtypeStruct((t, d), BF16),
        compiler_params=pltpu.CompilerParams(dimension_semantics=("parallel", "parallel")),
        name="swa_seq",
    )(q, k, k, v, v, sink)


def _swa_step_kernel(q_ref, kn_ref, vn_ref, knt_ref, vnt_ref, kc_ref, vc_ref, sink_ref, o_ref, ko_ref, vo_ref):
    nb, _, w = kc_ref.shape
    kc = kc_ref[...]
    vc = vc_ref[...]
    kn = kn_ref[...]
    vn = vn_ref[...]
    q = q_ref[...]
    scale = HEAD_DIM ** -0.5
    kvs = range(KV_HEADS)
    ksl = [slice(h * HEAD_DIM, (h + 1) * HEAD_DIM) for h in kvs]
    gsl = [slice(h * GROUP, (h + 1) * GROUP) for h in kvs]
    qh = [_bf(q[:, gsl[h], :]) for h in kvs]
    s_old = [jnp.einsum("bgd,bdw->bgw", qh[h], _bf(kc[:, ksl[h], :]), preferred_element_type=F32) * scale
             for h in kvs]
    s_new = [jnp.sum(qh[h].astype(F32) * _bf(kn[:, :, ksl[h]]).astype(F32), axis=-1, keepdims=True) * scale
             for h in kvs]
    sink = [sink_ref[h] for h in kvs]
    m = [jnp.maximum(jnp.maximum(jnp.max(s_old[h], axis=-1, keepdims=True), s_new[h]), sink[h]) for h in kvs]
    p_old = [jnp.exp(s_old[h] - m[h]) for h in kvs]
    p_new = [jnp.exp(s_new[h] - m[h]) for h in kvs]
    den = [jnp.sum(p_old[h], axis=-1, keepdims=True) + p_new[h] + jnp.exp(sink[h] - m[h]) for h in kvs]
    o = [jnp.einsum("bgw,bdw->bgd", _bf(p_old[h] / den[h]), _bf(vc[:, ksl[h], :]), preferred_element_type=F32)
         for h in kvs]
    for h in kvs:
        o_ref[:, gsl[h], :] = o[h] + _bf(p_new[h] / den[h]).astype(F32) * _bf(vn[:, :, ksl[h]]).astype(F32)
    last = lax.broadcasted_iota(jnp.int32, kc.shape, 2) == w - 1

    def column(t_ref):
        t = t_ref[0]
        return jnp.stack([jnp.broadcast_to(t[:, b:b + 1], t.shape[:1] + (w,)) for b in range(nb)])

    ko_ref[...] = jnp.where(last, column(knt_ref), pltpu.roll(kc, w - 1, axis=2))
    vo_ref[...] = jnp.where(last, column(vnt_ref), pltpu.roll(vc, w - 1, axis=2))


def _swa_step(q, kn, vn, cache_k, cache_v, sink):
    b, nh, hd = q.shape
    nkv, w = cache_k.shape[1], cache_k.shape[2]
    nb = min(b, DECODE_ROWS)
    i3 = lambda i: (i, 0, 0)
    cspec = pl.BlockSpec((nb, nkv, w), i3)
    nspec = pl.BlockSpec((nb, 1, nkv), i3)
    tspec = pl.BlockSpec((1, nkv, nb), i3)
    qspec = pl.BlockSpec((nb, nh, hd), i3)
    cols = lambda z: z.reshape(b // nb, nb, nkv).transpose(0, 2, 1)
    return pl.pallas_call(
        _swa_step_kernel,
        grid=(b // nb,),
        in_specs=[qspec, nspec, nspec, tspec, tspec, cspec, cspec, pl.BlockSpec(sink.shape, lambda i: (0, 0, 0))],
        out_specs=[qspec, cspec, cspec],
        out_shape=[jax.ShapeDtypeStruct(q.shape, F32), jax.ShapeDtypeStruct(cache_k.shape, F32),
                   jax.ShapeDtypeStruct(cache_v.shape, F32)],
        compiler_params=pltpu.CompilerParams(dimension_semantics=("parallel",)),
        name="swa_step",
    )(q, kn.reshape(b, 1, nkv), vn.reshape(b, 1, nkv), cols(kn), cols(vn), cache_k, cache_v, sink)


def kernel(x_prompt, x_sample, c_prompt, c_sample, state_wkv, state_shift, cache_k, cache_v, norm1_g, norm2_g, ada_w, ada_b, mlp_up, mlp_down, final_g, rw_mix, rw_wr, rw_wk, rw_wv, rw_wo, rw_w0, rw_w1, rw_w2, rw_a0, rw_a1, rw_a2, rw_g1, rw_g2, rw_kk, rw_ka, rw_rk, rw_lnx_g, rw_lnx_b, at_wqkv, at_bqkv, at_wo, at_sink):
    bp, seq, d = x_prompt.shape
    bs = x_sample.shape[0]
    assert x_sample.shape[1] == 1 and d % GW == 0 and d == rw_wr.shape[1]
    for rows in (TM_PROJ, TM_POST, TM_POST_RWKV, WKV_CHUNKS_PER_STEP * CHUNK, SWA_BLOCKS_PER_STEP * ATT_BLOCK):
        assert seq % rows == 0, (seq, rows)
    assert bs % 8 == 0 and cache_k.shape[2] <= WINDOW
    nh = d // HEAD_DIM
    tp = bp * seq
    row = lambda vec: vec.reshape(1, -1)

    n_c = bp + bs
    pad = (-n_c) % 8
    c_all = jnp.concatenate([c_prompt, c_sample, jnp.zeros((pad, d), F32)], axis=0)
    ada = _ada(c_all, ada_w, ada_b)
    mod_p = [ada[i, :bp].reshape(bp, N_ADA, d) for i in range(2)]
    mod_s = [ada[i, bp:bp + bs] for i in range(2)]

    xp = x_prompt.reshape(tp, d)
    xs = x_sample.reshape(bs, d)

    head_id = jnp.arange(GW) // HEAD_DIM
    ones_bd = (head_id[:, None] == head_id[None, :]).astype(BF16)

    rwp = dict(mix=rw_mix[0], wr=_bf(rw_wr[0]), wk=_bf(rw_wk[0]), wv=_bf(rw_wv[0]),
               w1=_bf(rw_w1[0]), w2=_bf(rw_w2[0]), w0=row(rw_w0[0]),
               a1=_bf(rw_a1[0]), a2=_bf(rw_a2[0]), a0=row(rw_a0[0]),
               g1=_bf(rw_g1[0]), g2=_bf(rw_g2[0]), kk=row(rw_kk[0]), ka=row(rw_ka[0]), ones=ones_bd)
    n1 = row(norm1_g[0])
    post_rw = (row(rw_lnx_g[0]), row(rw_lnx_b[0]), row(rw_rk[0]), ones_bd)
    wo0, up0, down0 = _bf(rw_wo[0]), _bf(mlp_up[0]), _bf(mlp_down[0])
    n2 = row(norm2_g[0])

    r, k, v, lw, kk, a, g, hlast = _rwkv_proj(xp, None, mod_p[0], n1, rwp, TM_PROJ, seq)
    shift_p = hlast.reshape(bp, seq // TM_PROJ, 8, d)[:, -1, -1]
    y, st = _wkv_seq(r, k, v, lw, kk, a, bp, seq)
    wkv_p = st.reshape(bp, d // GW, HEAD_DIM, HPG, HEAD_DIM).transpose(0, 1, 3, 2, 4).reshape(bp, nh, HEAD_DIM, HEAD_DIM)
    (xp,) = _post(xp, y, (r, k, v, g), mod_p[0], False, post_rw, wo0, n2, up0, down0, None, TM_POST_RWKV, seq)

    rs, ks, vs, lws, kks, as_, gs, shift_s = _rwkv_proj(xs, state_shift[0], mod_s[0], n1, rwp, bs, 1)
    vecs = jnp.stack([rs, jnp.exp(lws), ks, vs, -kks, kks * as_]).transpose(0, 2, 1).reshape(6, nh, HEAD_DIM, bs)
    st_s, ys = _wkv_step(state_wkv[0].transpose(1, 2, 3, 0), *vecs)
    wkv_s = st_s.transpose(3, 0, 1, 2)
    (xs,) = _post(xs, ys.reshape(d, bs).T, (rs, ks, vs, gs), mod_s[0], True, post_rw, wo0, n2, up0, down0, None, bs, 1)

    wqkv, bqkv, wo1 = _bf(at_wqkv[0]), row(at_bqkv[0]), _bf(at_wo[0])
    n1, n2 = row(norm1_g[1]), row(norm2_g[1])
    up1, down1 = _bf(mlp_up[1]), _bf(mlp_down[1])
    fg = row(final_g)
    nkv = KV_HEADS * HEAD_DIM

    q, kq, vq = _qkv(xp, mod_p[1], False, n1, wqkv, bqkv, _rope_tables(seq, 0, 1), TM_PROJ, seq)
    o = _swa_seq(q, kq, vq, row(at_sink[0]), bp, seq)
    keep = min(WINDOW, seq)
    k_p = kq.reshape(bp, seq, nkv)[:, seq - keep:].reshape(bp, keep, KV_HEADS, HEAD_DIM)
    v_p = vq.reshape(bp, seq, nkv)[:, seq - keep:].reshape(bp, keep, KV_HEADS, HEAD_DIM)
    xp, y_p = _post(xp, o, None, mod_p[1], False, None, wo1, n2, up1, down1, fg, TM_POST, seq)

    qs, kn, vn = _qkv(xs, mod_s[1], True, n1, wqkv, bqkv, _rope_tables(bs, PAST_LEN, 0), bs, 1)
    w_buf = cache_k.shape[2]
    to_minor = lambda c: c.transpose(0, 2, 3, 1).reshape(bs, nkv, w_buf)
    from_minor = lambda c: c.reshape(bs, KV_HEADS, HEAD_DIM, w_buf).transpose(0, 3, 1, 2)
    os_, k_s, v_s = _swa_step(qs.reshape(bs, nh, HEAD_DIM), kn, vn, to_minor(cache_k[0]), to_minor(cache_v[0]),
                              at_sink[0].reshape(KV_HEADS, GROUP, 1))
    xs, y_s = _post(xs, os_.reshape(bs, d), None, mod_s[1], True, None, wo1, n2, up1, down1, fg, bs, 1)

    return (y_p.reshape(bp, seq, d), y_s.reshape(bs, 1, d),
            wkv_p[None], wkv_s[None], shift_p[None], shift_s[None],
            k_p[None], from_minor(k_s)[None], v_p[None], from_minor(v_s)[None])
```

```python
import functools
import math

import jax
import jax.numpy as jnp
import numpy as np
from jax import lax
from jax.experimental import pallas as pl
from jax.experimental.pallas import tpu as pltpu

F32 = jnp.float32
BF16 = jnp.bfloat16

HEAD_DIM = 64
KV_HEADS = 4
GROUP = 4
WINDOW = 128
ATT_BLOCK = 128
ROPE_DIM = HEAD_DIM // 4
ROPE_THETA = 500000.0
PAST_LEN = 8192
NORM_EPS = 1e-6
LNX_EPS = 64e-5
N_ADA = 6

LANE = 128
V7X_VMEM_BYTES = 64 * 1024 * 1024
VMEM_LIMIT = V7X_VMEM_BYTES * 7 // 8

CHUNK = 64
HPG = 2
GW = HPG * HEAD_DIM

TM_PROJ = 512
TM_SUB = 256
TM_POST_RWKV = 256
TM_POST = 512
MLP_FF_CHUNK = 1024
WKV_CHUNKS_PER_STEP = 4
ADA_COLS = 1024
ROPE_ROWS = 512
DECODE_ROWS = 16
SWA_BLOCKS_PER_STEP = 4


def _bf(x):
    return x.astype(BF16)


def _dot(a, b):
    return jnp.dot(a, b, preferred_element_type=F32)


def _dot_nt(a, b):
    return lax.dot_general(a, b, (((1,), (1,)), ((), ())), preferred_element_type=F32)


def _rms(x, g):
    ms = jnp.mean(x * x, axis=-1, keepdims=True)
    return x * lax.rsqrt(ms + NORM_EPS) * g


def _prenorm(x, g, shift, scale):
    return _rms(x, g) * (1.0 + scale) + shift


def _mod(mod_ref, per_row, idx, d):
    if per_row:
        return mod_ref[:, idx * d:(idx + 1) * d]
    return mod_ref[0, idx:idx + 1, :]


def _pieces(x, n):
    out = []
    rem = x
    for i in range(n):
        p = rem.astype(BF16)
        out.append(p)
        if i + 1 < n:
            rem = rem - p.astype(F32)
    return out


def _head_sum(val, ones):
    gw = ones.shape[0]
    cols = [_dot(_bf(val[:, j * gw:(j + 1) * gw]), ones) for j in range(val.shape[1] // gw)]
    return jnp.concatenate(cols, axis=1)


def _ada_kernel(c_ref, w_ref, b_ref, o_ref):
    c = c_ref[...]
    s = c * jax.nn.sigmoid(c)
    o_ref[0] = _dot(_bf(s), _bf(w_ref[0])) + b_ref[0]


def _ada(c_all, ada_w, ada_b):
    depth, d, n = ada_w.shape
    m = c_all.shape[0]
    tn = ADA_COLS
    return pl.pallas_call(
        _ada_kernel,
        grid=(depth, n // tn),
        in_specs=[
            pl.BlockSpec((m, d), lambda i, j: (0, 0)),
            pl.BlockSpec((1, d, tn), lambda i, j: (i, 0, j)),
            pl.BlockSpec((1, 1, tn), lambda i, j: (i, 0, j)),
        ],
        out_specs=pl.BlockSpec((1, m, tn), lambda i, j: (i, 0, j)),
        out_shape=jax.ShapeDtypeStruct((depth, m, n), F32),
        compiler_params=pltpu.CompilerParams(dimension_semantics=("parallel", "parallel")),
        name="ada",
    )(c_all, ada_w, ada_b.reshape(depth, 1, n))


def _rwkv_proj_kernel(per_row, tiles_per_seq,
                      x_ref, xp_ref, mod_ref, ng_ref, mix_ref, wr_ref, wk_ref, wv_ref,
                      w1_ref, w2_ref, w0_ref, a1_ref, a2_ref, a0_ref, g1_ref, g2_ref,
                      kk_ref, ka_ref, ones_ref,
                      r_o, k_o, v_o, lw_o, kk_o, a_o, g_o, h_o, h_scr, xx_scr):
    d = x_ref.shape[1]
    tm = x_ref.shape[0]
    ng = ng_ref[...]
    shift = _mod(mod_ref, per_row, 0, d)
    scale = _mod(mod_ref, per_row, 1, d)
    h = _prenorm(x_ref[...], ng, shift, scale)
    if per_row:
        prev = xp_ref[...]
        h_o[...] = h
    else:
        hp = _prenorm(xp_ref[...], ng, shift, scale)
        first = (pl.program_id(0) % tiles_per_seq) == 0
        prow = jnp.where(first, 0.0, hp[7:8, :])
        rowid = lax.broadcasted_iota(jnp.int32, h.shape, 0)
        prev = jnp.where(rowid == 0, prow, pltpu.roll(h, 1, axis=0))
        h_o[0] = h[tm - 8:tm, :]
    h_scr[...] = h
    xx_scr[...] = prev - h

    sub = min(tm, TM_SUB)
    subs = [slice(i, i + sub) for i in range(0, tm, sub)]

    def mixed(j):
        return [_bf(h_scr[sl, :] + xx_scr[sl, :] * mix_ref[j:j + 1, :]) for sl in subs]

    def each(f, *cols):
        return [f(*vals) for vals in zip(*cols)]

    for sl, r in zip(subs, each(lambda m: _dot(m, wr_ref[...]), mixed(0))):
        r_o[sl, :] = r
    t1 = each(lambda m: _bf(jnp.tanh(_dot(m, w1_ref[...]))), mixed(1))
    for sl, wl in zip(subs, each(lambda t: w0_ref[...] + _dot(t, w2_ref[...]), t1)):
        lw_o[sl, :] = jax.nn.sigmoid(wl) * (-math.exp(-0.5))
    t4 = each(lambda m: _bf(_dot(m, a1_ref[...])), mixed(4))
    a = each(lambda t: jax.nn.sigmoid(a0_ref[...] + _dot(t, a2_ref[...])), t4)
    for sl, ai in zip(subs, a):
        a_o[sl, :] = ai
    k = each(lambda m: _dot(m, wk_ref[...]), mixed(2))
    kk = each(lambda ki: ki * kk_ref[...], k)
    ss = each(lambda kki: _head_sum(kki * kki, ones_ref[...]), kk)
    for sl, ki, ai, kki, ssi in zip(subs, k, a, kk, ss):
        k_o[sl, :] = ki * (1.0 + (ai - 1.0) * ka_ref[...])
        kk_o[sl, :] = kki * lax.rsqrt(jnp.maximum(ssi, 1e-24))
    for sl, v in zip(subs, each(lambda m: _dot(m, wv_ref[...]), mixed(3))):
        v_o[sl, :] = v
    t5 = each(lambda m: _bf(jax.nn.sigmoid(_dot(m, g1_ref[...]))), mixed(5))
    for sl, g in zip(subs, each(lambda t: _dot(t, g2_ref[...]), t5)):
        g_o[sl, :] = g


def _const_spec(shape):
    nd = len(shape)
    return pl.BlockSpec(shape, lambda *_: (0,) * nd, pipeline_mode=pl.Buffered(1))


def _rwkv_proj(x, prev_rows, mod, ng, p, tm, seq_len):
    t, d = x.shape
    per_row = prev_rows is not None
    nt = t // tm
    tiles_per_seq = 1 if per_row else seq_len // tm
    row = lambda i: (i, 0)
    if per_row:
        xp, xp_spec = prev_rows, pl.BlockSpec((tm, d), row)
        mod_spec = pl.BlockSpec((tm, N_ADA * d), row)
        h_shape, h_spec = jax.ShapeDtypeStruct((t, d), F32), pl.BlockSpec((tm, d), row)
    else:
        xp, xp_spec = x, pl.BlockSpec((8, d), lambda i: (jnp.maximum(i * (tm // 8) - 1, 0), 0))
        mod_spec = pl.BlockSpec((1, N_ADA, d), lambda i: (i // tiles_per_seq, 0, 0))
        h_shape, h_spec = jax.ShapeDtypeStruct((nt, 8, d), F32), pl.BlockSpec((1, 8, d), lambda i: (i, 0, 0))
    consts = [ng, p["mix"], p["wr"], p["wk"], p["wv"], p["w1"], p["w2"], p["w0"], p["a1"], p["a2"], p["a0"],
              p["g1"], p["g2"], p["kk"], p["ka"], p["ones"]]
    act = jax.ShapeDtypeStruct((t, d), F32)
    return pl.pallas_call(
        functools.partial(_rwkv_proj_kernel, per_row, tiles_per_seq),
        grid=(nt,),
        in_specs=[pl.BlockSpec((tm, d), row), xp_spec, mod_spec] + [_const_spec(c.shape) for c in consts],
        out_specs=[pl.BlockSpec((tm, d), row)] * 7 + [h_spec],
        out_shape=[act] * 7 + [h_shape],
        scratch_shapes=[pltpu.VMEM((tm, d), F32)] * 2,
        compiler_params=pltpu.CompilerParams(dimension_semantics=("parallel",), vmem_limit_bytes=VMEM_LIMIT),
        name="rwkv_proj_step" if per_row else "rwkv_proj_seq",
    )(x, xp, mod, *consts)


def _head_of_lane(shape, axis):
    return lax.broadcasted_iota(jnp.int32, shape, axis) // HEAD_DIM


def _bd(p, low_half):
    zero = jnp.zeros((), p.dtype)
    zeros = jnp.zeros((p.shape[0], LANE), p.dtype)
    per_col = LANE // HEAD_DIM
    ncol = GW // LANE
    blocks = []
    for h in range(HPG):
        c = h // per_col
        col = p[:, c * LANE:(c + 1) * LANE]
        col = jnp.where(low_half, col, zero) if h % per_col == 0 else jnp.where(low_half, zero, col)
        blocks.append(jnp.concatenate([col if j == c else zeros for j in range(ncol)], axis=1))
    return jnp.concatenate(blocks, axis=0)


def _wkv_prep_chunks(rs, ks, vs, lws, kks, as_, masks):
    tri, diag, strict, incl, eye, blk8, levels, lane_head = masks
    n = len(rs)

    def each(f, *args):
        return [f(*[a[i] for a in args]) for i in range(n)]

    def pp(xs, ys):
        rhs = each(lambda y: _bd(_bf(y), diag), ys)
        return each(lambda x, b: _dot(_bf(x), b), xs, rhs)

    def pick_diag(gram):
        out = gram[0:HEAD_DIM, :]
        for hh in range(1, HPG):
            out = jnp.where(lane_head == hh, gram[hh * HEAD_DIM:(hh + 1) * HEAD_DIM, :], out)
        return out

    def add(xs, ys):
        return each(lambda x, y: x + y, xs, ys)

    def masked(m, xs):
        return each(lambda x: jnp.where(m, x, 0.0), xs)

    c2 = each(lambda lw: _dot(tri, jnp.concatenate(_pieces(lw, 2), axis=1)), lws)
    c = each(lambda x: x[:, :GW] + x[:, GW:], c2)
    c_last = each(lambda ci: ci[CHUNK - 1:CHUNK, :], c)
    e_neg = each(lambda ci: jnp.exp(-ci), c)
    e_end = each(lambda ci, cl: jnp.exp(cl - ci), c, c_last)
    bv = each(lambda kk, a: kk * a, kks, as_)
    a_t = each(lambda kk, ci, lw: _bf(-kk * jnp.exp(ci - lw)), kks, c, lws)
    r_t = each(lambda r, ci: r * jnp.exp(ci), rs, c)
    b_t = each(lambda b, e: _bf(b * e), bv, e_neg)
    k_t = each(lambda k, e: _bf(k * e), ks, e_neg)
    b_e = each(lambda b, e: _bf(b * e), bv, e_end)
    k_e = each(lambda k, e: _bf(k * e), ks, e_end)

    bd = lambda y: _bd(_bf(y), diag)
    ar = each(lambda x, y: jnp.concatenate([x, _bf(y)], axis=0), a_t, r_t)
    xbk = each(lambda x, b, k: _dot_nt(x, jnp.concatenate([bd(b), bd(k)], axis=0)), ar, b_t, k_t)
    a_ab = each(lambda x: jnp.where(strict, x[:CHUNK, :GW], 0.0), xbk)
    a_rb = each(lambda x: jnp.where(incl, x[CHUNK:, :GW], 0.0), xbk)
    a_ak = each(lambda x: jnp.where(strict, x[:CHUNK, GW:], 0.0), xbk)
    a_rk = each(lambda x: jnp.where(incl, x[CHUNK:, GW:], 0.0), xbk)
    av = pp(a_ak, vs)

    n0 = masked(blk8, a_ab)
    n2 = pp(n0, n0)
    both = pp(each(lambda x, y: jnp.concatenate([x, eye + y], axis=0), n2, n0), n2)
    n4 = each(lambda x: x[:CHUNK], both)
    t01 = each(lambda x, y: eye + x + y[CHUNK:], n0, both)
    tinv = add(t01, pp(t01, n4))
    b = 8
    for lvl in levels:
        lower = lambda x: jnp.concatenate([x[r0:r0 + b] for r0 in range(b, CHUNK, 2 * b)], axis=0)
        z = pp(pp(each(lower, tinv), masked(lvl, a_ab)), tinv)

        def merged(t, zi):
            rows = []
            for j, r0 in enumerate(range(0, CHUNK, 2 * b)):
                rows += [t[r0:r0 + b], t[r0 + b:r0 + 2 * b] + zi[j * b:(j + 1) * b]]
            return jnp.concatenate(rows, axis=0)

        tinv = each(merged, tinv, z)
        b *= 2

    w12 = each(lambda t, a, x: _dot(_bf(t), jnp.concatenate([bd(a), bd(x)], axis=1)), tinv, a_t, av)
    w1 = each(lambda w: w[:, :GW], w12)
    w2 = each(lambda w: w[:, GW:], w12)
    zeros = jnp.zeros((GW, GW), BF16)

    def qy_rhs(wa, wb, v):
        return jnp.concatenate([jnp.concatenate([bd(wa), bd(wb)], axis=1),
                                jnp.concatenate([zeros, bd(v)], axis=1)], axis=0)

    qy = each(lambda x1, x2, wa, wb, v: _dot(jnp.concatenate([_bf(x1), _bf(x2)], axis=1), qy_rhs(wa, wb, v)),
              a_rb, a_rk, w1, w2, vs)
    qc = each(lambda r, x: r + x[:, :GW], r_t, qy)
    y0 = each(lambda x: x[:, GW:], qy)
    stack = lambda x, y: jnp.concatenate([x, y], axis=0)
    lhs = each(lambda wa, wb, v: _bf(jnp.concatenate([stack(wa, jnp.zeros_like(wa)).T, stack(wb, v).T], axis=0)),
               w1, w2, vs)
    gram = each(lambda x, b, k: _dot(x, stack(b, k)), lhs, b_e, k_e)
    mc = each(lambda gm: pick_diag(gm[:GW]), gram)
    cc = each(lambda gm: pick_diag(gm[GW:]), gram)
    return qc, y0, mc, cc, c_last


def _wkv_masks():
    t = lax.broadcasted_iota(jnp.int32, (CHUNK, GW), 0)
    s = lax.broadcasted_iota(jnp.int32, (CHUNK, GW), 1) % HEAD_DIM
    tri = (lax.broadcasted_iota(jnp.int32, (CHUNK, CHUNK), 1)
           <= lax.broadcasted_iota(jnp.int32, (CHUNK, CHUNK), 0)).astype(BF16)
    diag = lax.broadcasted_iota(jnp.int32, (CHUNK, LANE), 1) < HEAD_DIM
    strict = s < t
    incl = s <= t
    eye = (s == t).astype(F32)
    blk8 = strict & ((s // 8) == (t // 8))
    levels = []
    b = 8
    while b < CHUNK:
        levels.append(((s // (2 * b)) == (t // (2 * b))) & ((s // b) != (t // b)) & strict)
        b *= 2
    lane_head = _head_of_lane((HEAD_DIM, GW), 1)
    return tri, diag, strict, incl, eye, blk8, levels, lane_head


def _rwkv_epilogue(y, r, k, v, g, lg, lb, rk, ones):
    inv_n = 1.0 / HEAD_DIM
    mu = _head_sum(y, ones) * inv_n
    dy = y - mu
    var = _head_sum(dy * dy, ones) * inv_n
    yn = dy * lax.rsqrt(var + LNX_EPS) * lg + lb
    bonus = _head_sum(r * k * rk, ones) * v
    return (yn + bonus) * g


def _wkv_seq_kernel(nsub, ng, r_ref, k_ref, v_ref, lw_ref, kk_ref, a_ref, y_o, s_o, s_scr):
    step = pl.program_id(1)

    @pl.when(step == 0)
    def _():
        s_scr[...] = jnp.zeros_like(s_scr)

    masks = _wkv_masks()
    diag = masks[1]
    sls = [slice(u * CHUNK, (u + 1) * CHUNK) for u in range(nsub)]
    gls = [slice(g * GW, (g + 1) * GW) for g in range(ng)]
    states = [s_scr[:, gl] for gl in gls]

    chunks = lambda ref: [ref[sls[u], gls[g]] for u in range(nsub) for g in range(ng)]
    qc, y0, mc, cc, cl = _wkv_prep_chunks(chunks(r_ref), chunks(k_ref), chunks(v_ref), chunks(lw_ref),
                                          chunks(kk_ref), chunks(a_ref), masks)
    for u in range(nsub):
        sb = [_bf(s) for s in states]
        upd = [_dot(sb[g], _bd(_bf(mc[u * ng + g]), diag)) for g in range(ng)]
        yy = [_dot_nt(_bf(qc[u * ng + g]), _bd(sb[g], diag)) for g in range(ng)]
        for g in range(ng):
            i = u * ng + g
            states[g] = states[g] * jnp.exp(cl[i]) + upd[g] + cc[i]
            y_o[sls[u], gls[g]] = yy[g] + y0[i]
    for g, gl in enumerate(gls):
        s_scr[:, gl] = states[g]

    @pl.when(step == pl.num_programs(1) - 1)
    def _():
        for g in range(ng):
            s_o[0, g] = states[g]


def _wkv_seq(r, k, v, lw, kk, a, batch, seq_len):
    t, d = r.shape
    nsub = WKV_CHUNKS_PER_STEP
    rows = nsub * CHUNK
    ng = d // GW
    steps = seq_len // rows
    blk = pl.BlockSpec((rows, d), lambda i, j: (i * steps + j, 0))
    return pl.pallas_call(
        functools.partial(_wkv_seq_kernel, nsub, ng),
        grid=(batch, steps),
        in_specs=[blk] * 6,
        out_specs=[blk, pl.BlockSpec((1, ng, HEAD_DIM, GW), lambda i, j: (i, 0, 0, 0))],
        out_shape=[jax.ShapeDtypeStruct((t, d), F32), jax.ShapeDtypeStruct((batch, ng, HEAD_DIM, GW), F32)],
        scratch_shapes=[pltpu.VMEM((HEAD_DIM, d), F32)],
        compiler_params=pltpu.CompilerParams(dimension_semantics=("parallel", "arbitrary"),
                                             vmem_limit_bytes=VMEM_LIMIT),
        name="wkv_seq",
    )(r, k, v, lw, kk, a)


def _wkv_step_kernel(s_ref, r_ref, w_ref, k_ref, v_ref, a_ref, b_ref, so_ref, y_ref):
    s = s_ref[0]
    sa = jnp.sum(s * a_ref[0][None], axis=1)
    s = s * w_ref[0][None] + sa[:, None, :] * b_ref[0][None] + v_ref[0][:, None, :] * k_ref[0][None]
    so_ref[0] = s
    y_ref[0] = jnp.sum(s * r_ref[0][None], axis=1)


def _wkv_step(state, r, decay, k, v, a, b):
    nh, n, _, nb = state.shape
    sspec = pl.BlockSpec((1, n, n, nb), lambda i: (i, 0, 0, 0))
    vspec = pl.BlockSpec((1, n, nb), lambda i: (i, 0, 0))
    return pl.pallas_call(
        _wkv_step_kernel,
        grid=(nh,),
        in_specs=[sspec] + [vspec] * 6,
        out_specs=[sspec, vspec],
        out_shape=[jax.ShapeDtypeStruct(state.shape, F32), jax.ShapeDtypeStruct((nh, n, nb), F32)],
        compiler_params=pltpu.CompilerParams(dimension_semantics=("parallel",)),
        name="wkv_step",
    )(state, r, decay, k, v, a, b)


def _post_kernel(rwkv, final, per_row, *refs):
    refs = list(refs)
    x_ref = refs.pop(0)
    z_ref = refs.pop(0)
    if rwkv:
        r_ref, k_ref, v_ref, g_ref = refs[:4]
        refs = refs[4:]
    mod_ref = refs.pop(0)
    if rwkv:
        lg_ref, lb_ref, rk_ref, ones_ref = refs[:4]
        refs = refs[4:]
    wo_ref, n2_ref, up_ref, down_ref = refs[:4]
    refs = refs[4:]
    if final:
        fg_ref = refs.pop(0)
    x_o = refs.pop(0)
    d = x_ref.shape[1]

    z = z_ref[...]
    if rwkv:
        z = _rwkv_epilogue(z, r_ref[...], k_ref[...], v_ref[...], g_ref[...],
                           lg_ref[...], lb_ref[...], rk_ref[...], ones_ref[...])
    out = _dot(_bf(z), wo_ref[...])
    x1 = x_ref[...] + _mod(mod_ref, per_row, 2, d) * out
    h2 = _bf(_prenorm(x1, n2_ref[...], _mod(mod_ref, per_row, 3, d), _mod(mod_ref, per_row, 4, d)))
    dff = up_ref.shape[2]
    fc = MLP_FF_CHUNK
    acc = None
    for c in range(dff // fc):
        hid = jnp.maximum(_dot(h2, up_ref[0, :, c * fc:(c + 1) * fc]), 0.0)
        part = _dot(_bf(hid * hid), down_ref[0, c * fc:(c + 1) * fc, :])
        acc = part if acc is None else acc + part
    x2 = x1 + _mod(mod_ref, per_row, 5, d) * acc
    x_o[...] = x2
    if final:
        refs[0][...] = _rms(x2, fg_ref[...])


def _post(x, z, extra, mod, per_row, rw, wo, n2g, layer, up, down, final_g, tm, seq_len):
    t, d = x.shape
    rwkv = rw is not None
    final = final_g is not None
    row = lambda i: (i, 0)
    tile = pl.BlockSpec((tm, d), row)
    tiles_per_seq = 1 if per_row else seq_len // tm
    mod_spec = (pl.BlockSpec((tm, N_ADA * d), row) if per_row
                else pl.BlockSpec((1, N_ADA, d), lambda i: (i // tiles_per_seq, 0, 0)))
    args = [x, z]
    specs = [tile, tile]
    if rwkv:
        args += list(extra)
        specs += [tile] * 4
    args.append(mod)
    specs.append(mod_spec)
    layer_spec = lambda w: pl.BlockSpec((1,) + w.shape[1:], lambda i: (layer, 0, 0), pipeline_mode=pl.Buffered(1))
    consts = (list(rw) if rwkv else []) + [wo, n2g]
    args += consts + [up, down] + ([final_g] if final else [])
    specs += ([_const_spec(c.shape) for c in consts] + [layer_spec(up), layer_spec(down)]
              + ([_const_spec(final_g.shape)] if final else []))
    act = jax.ShapeDtypeStruct((t, d), F32)
    outs = pl.pallas_call(
        functools.partial(_post_kernel, rwkv, final, per_row),
        grid=(t // tm,),
        in_specs=specs,
        out_specs=[tile] * (2 if final else 1),
        out_shape=[act] * (2 if final else 1),
        compiler_params=pltpu.CompilerParams(dimension_semantics=("parallel",), vmem_limit_bytes=VMEM_LIMIT),
        name=("post_rwkv" if rwkv else "post") + ("_final" if final else "") + ("_step" if per_row else "_seq"),
    )(*args)
    return outs


def _rope_table_kernel(base, step, invf_ref, c_o, sm_o, sp_o):
    shape = c_o.shape
    row = lax.broadcasted_iota(jnp.int32, shape, 0) + pl.program_id(0) * shape[0]
    pos = (base + step * row).astype(F32)
    ang = pos * invf_ref[...]
    lane = lax.broadcasted_iota(jnp.int32, shape, 1) % HEAD_DIM
    cos = jnp.cos(ang)
    sin = jnp.sin(ang)
    half = ROPE_DIM // 2
    c_o[...] = jnp.where(lane < ROPE_DIM, cos, 1.0)
    sm_o[...] = jnp.where(lane < half, -sin, 0.0)
    sp_o[...] = jnp.where((lane >= half) & (lane < ROPE_DIM), sin, 0.0)


def _rope_tables(n, base, step):
    half = ROPE_DIM // 2
    inv_freq = np.float32(ROPE_THETA) ** (-np.arange(half, dtype=np.float32) / np.float32(half))
    lane = np.arange(LANE) % HEAD_DIM
    invf = jnp.asarray(np.where(lane < ROPE_DIM, inv_freq[lane % half], 0.0).reshape(1, LANE), F32)
    tr = min(n, ROPE_ROWS)
    spec = pl.BlockSpec((tr, LANE), lambda i: (i, 0))
    shp = jax.ShapeDtypeStruct((n, LANE), F32)
    return pl.pallas_call(
        functools.partial(_rope_table_kernel, base, step),
        grid=(n // tr,),
        in_specs=[pl.BlockSpec((1, LANE), lambda i: (0, 0))],
        out_specs=[spec] * 3,
        out_shape=[shp] * 3,
        compiler_params=pltpu.CompilerParams(dimension_semantics=("parallel",)),
        name="rope_tables",
    )(invf)


def _qkv_kernel(per_row, x_ref, mod_ref, ng_ref, w_ref, b_ref, c_ref, sm_ref, sp_ref, q_o, k_o, v_o):
    tm, d = x_ref.shape
    nq, nk = q_o.shape[1], k_o.shape[1]
    sub = min(tm, TM_SUB)
    subs = [slice(i, i + sub) for i in range(0, tm, sub)]
    shift, scale = _mod(mod_ref, per_row, 0, d), _mod(mod_ref, per_row, 1, d)
    rows = lambda m, sl: m[sl, :] if per_row else m
    h = [_bf(_prenorm(x_ref[sl, :], ng_ref[...], rows(shift, sl), rows(scale, sl))) for sl in subs]
    qkv = [_dot(hh, w_ref[...]) + b_ref[...] for hh in h]
    for sl, z in zip(subs, qkv):
        c, sm, sp = c_ref[sl, :], sm_ref[sl, :], sp_ref[sl, :]

        def rope_into(o_ref, off, width):
            for j in range(width // LANE):
                zc = z[:, off + j * LANE:off + (j + 1) * LANE]
                o_ref[sl, j * LANE:(j + 1) * LANE] = (zc * c + pltpu.roll(zc, LANE - ROPE_DIM // 2, axis=1) * sm
                                                      + pltpu.roll(zc, ROPE_DIM // 2, axis=1) * sp)

        rope_into(q_o, 0, nq)
        rope_into(k_o, nq, nk)
        v_o[sl, :] = z[:, nq + nk:]


def _qkv(x, mod, per_row, ng, w, b, tables, tm, seq_len):
    t, d = x.shape
    nkv = KV_HEADS * HEAD_DIM
    row = lambda i: (i, 0)
    tiles_per_seq = 1 if per_row else seq_len // tm
    mod_spec = (pl.BlockSpec((tm, N_ADA * d), row) if per_row
                else pl.BlockSpec((1, N_ADA, d), lambda i: (i // tiles_per_seq, 0, 0)))
    tab_spec = pl.BlockSpec((tm, LANE), lambda i: (i % tiles_per_seq, 0))
    return pl.pallas_call(
        functools.partial(_qkv_kernel, per_row),
        grid=(t // tm,),
        in_specs=[pl.BlockSpec((tm, d), row), mod_spec, _const_spec(ng.shape), _const_spec(w.shape),
                  _const_spec(b.shape)] + [tab_spec] * 3,
        out_specs=[pl.BlockSpec((tm, d), row), pl.BlockSpec((tm, nkv), row), pl.BlockSpec((tm, nkv), row)],
        out_shape=[jax.ShapeDtypeStruct((t, d), F32), jax.ShapeDtypeStruct((t, nkv), F32),
                   jax.ShapeDtypeStruct((t, nkv), F32)],
        compiler_params=pltpu.CompilerParams(dimension_semantics=("parallel",), vmem_limit_bytes=VMEM_LIMIT),
        name="qkv_step" if per_row else "qkv_seq",
    )(x, mod, ng, w, b, *tables)


def _swa_seq_kernel(q_ref, kc_ref, kp_ref, vc_ref, vp_ref, sink_ref, o_ref):
    n = pl.program_id(1)
    blk = ATT_BLOCK
    nq = q_ref.shape[0] // blk
    kall = jnp.concatenate([kp_ref[...], kc_ref[...]], axis=0)
    vall = jnp.concatenate([vp_ref[...], vc_ref[...]], axis=0)
    j = lax.broadcasted_iota(jnp.int32, (2 * blk, GROUP * blk), 0)
    i = lax.broadcasted_iota(jnp.int32, (2 * blk, GROUP * blk), 1) % blk
    band = (j >= i + (blk - WINDOW)) & (j <= i + blk)
    first = band & ((n > 0) | (j >= blk))
    hsl = [slice(h * HEAD_DIM, (h + 1) * HEAD_DIM) for h in range(KV_HEADS * GROUP)]
    for t in range(nq):
        _swa_block(q_ref[t * blk:(t + 1) * blk, :], kall[t * blk:(t + 2) * blk], vall[t * blk:(t + 2) * blk],
                   first if t == 0 else band, sink_ref, o_ref, slice(t * blk, (t + 1) * blk), hsl)


def _swa_block(q, kall, vall, mask, sink_ref, o_ref, rows, hsl):
    blk = ATT_BLOCK
    kvs = range(KV_HEADS)
    eye_d = (lax.broadcasted_iota(jnp.int32, (HEAD_DIM, HEAD_DIM), 0)
             == lax.broadcasted_iota(jnp.int32, (HEAD_DIM, HEAD_DIM), 1)).astype(BF16)
    eye_q = (lax.broadcasted_iota(jnp.int32, (blk, blk), 0)
             == lax.broadcasted_iota(jnp.int32, (blk, blk), 1)).astype(BF16)
    ones = jnp.ones((16, 2 * blk), BF16)
    scale = HEAD_DIM ** -0.5
    kh = [_bf(kall[:, hsl[c]]) for c in kvs]
    vte = [jnp.concatenate([_bf(_dot_nt(eye_d, _bf(vall[:, hsl[c]]))), ones], axis=0) for c in kvs]
    qg = [_bf(jnp.concatenate([q[:, hsl[c * GROUP + g]] for g in range(GROUP)], axis=0) * scale) for c in kvs]
    s = [jnp.where(mask, _dot_nt(kh[c], qg[c]), -jnp.inf) for c in kvs]
    sink = [jnp.concatenate([jnp.broadcast_to(sink_ref[0:1, h:h + 1], (1, blk))
                             for h in range(c * GROUP, (c + 1) * GROUP)], axis=1) for c in kvs]
    m = [jnp.maximum(jnp.max(s[c], axis=0, keepdims=True), sink[c]) for c in kvs]
    p = [_bf(jnp.exp(s[c] - m[c])) for c in kvs]
    oe = [_dot(vte[c], p[c]) for c in kvs]
    ot = [_bf(oe[c][:HEAD_DIM] / (oe[c][HEAD_DIM:HEAD_DIM + 1] + jnp.exp(sink[c] - m[c]))) for c in kvs]
    for c in kvs:
        for pair in range(GROUP // 2):
            both = jnp.concatenate([ot[c][:, (2 * pair + e) * blk:(2 * pair + e + 1) * blk] for e in range(2)],
                                   axis=0)
            col = (c * GROUP + 2 * pair) * HEAD_DIM
            o_ref[rows, col:col + 2 * HEAD_DIM] = _dot_nt(eye_q, both).astype(o_ref.dtype)


def _swa_seq(q, k, v, sink, batch, seq_len):
    t, d = q.shape
    nkv = k.shape[1]
    nq = SWA_BLOCKS_PER_STEP
    rows = nq * ATT_BLOCK
    nb = seq_len // rows
    cur = lambda b, n: (b * nb + n, 0)
    prv = lambda b, n: ((b * nb + n) * nq - jnp.minimum(n, 1), 0)
    return pl.pallas_call(
        _swa_seq_kernel,
        grid=(batch, nb),
        in_specs=[pl.BlockSpec((rows, d), cur),
                  pl.BlockSpec((rows, nkv), cur), pl.BlockSpec((ATT_BLOCK, nkv), prv),
                  pl.BlockSpec((rows, nkv), cur), pl.BlockSpec((ATT_BLOCK, nkv), prv),
                  pl.BlockSpec(sink.shape, lambda b, n: (0, 0))],
        out_specs=pl.BlockSpec((rows, d), cur),
        out_shape=jax.ShapeDtypeStruct((t, d), BF16),
        compiler_params=pltpu.CompilerParams(dimension_semantics=("parallel", "parallel")),
        name="swa_seq",
    )(q, k, k, v, v, sink)


def _swa_step_kernel(q_ref, kn_ref, vn_ref, knt_ref, vnt_ref, kc_ref, vc_ref, sink_ref, o_ref, ko_ref, vo_ref):
    nb, _, w = kc_ref.shape
    kc = kc_ref[...]
    vc = vc_ref[...]
    kn = kn_ref[...]
    vn = vn_ref[...]
    q = q_ref[...]
    scale = HEAD_DIM ** -0.5
    kvs = range(KV_HEADS)
    ksl = [slice(h * HEAD_DIM, (h + 1) * HEAD_DIM) for h in kvs]
    gsl = [slice(h * GROUP, (h + 1) * GROUP) for h in kvs]
    qh = [_bf(q[:, gsl[h], :]) for h in kvs]
    s_old = [jnp.einsum("bgd,bdw->bgw", qh[h], _bf(kc[:, ksl[h], :]), preferred_element_type=F32) * scale
             for h in kvs]
    s_new = [jnp.sum(qh[h].astype(F32) * _bf(kn[:, :, ksl[h]]).astype(F32), axis=-1, keepdims=True) * scale
             for h in kvs]
    sink = [sink_ref[h] for h in kvs]
    m = [jnp.maximum(jnp.maximum(jnp.max(s_old[h], axis=-1, keepdims=True), s_new[h]), sink[h]) for h in kvs]
    p_old = [jnp.exp(s_old[h] - m[h]) for h in kvs]
    p_new = [jnp.exp(s_new[h] - m[h]) for h in kvs]
    den = [jnp.sum(p_old[h], axis=-1, keepdims=True) + p_new[h] + jnp.exp(sink[h] - m[h]) for h in kvs]
    o = [jnp.einsum("bgw,bdw->bgd", _bf(p_old[h] / den[h]), _bf(vc[:, ksl[h], :]), preferred_element_type=F32)
         for h in kvs]
    for h in kvs:
        o_ref[:, gsl[h], :] = o[h] + _bf(p_new[h] / den[h]).astype(F32) * _bf(vn[:, :, ksl[h]]).astype(F32)
    last = lax.broadcasted_iota(jnp.int32, kc.shape, 2) == w - 1

    def column(t_ref):
        t = t_ref[0]
        return jnp.stack([jnp.broadcast_to(t[:, b:b + 1], t.shape[:1] + (w,)) for b in range(nb)])

    ko_ref[...] = jnp.where(last, column(knt_ref), pltpu.roll(kc, w - 1, axis=2))
    vo_ref[...] = jnp.where(last, column(vnt_ref), pltpu.roll(vc, w - 1, axis=2))


def _swa_step(q, kn, vn, cache_k, cache_v, sink):
    b, nh, hd = q.shape
    nkv, w = cache_k.shape[1], cache_k.shape[2]
    nb = min(b, DECODE_ROWS)
    i3 = lambda i: (i, 0, 0)
    cspec = pl.BlockSpec((nb, nkv, w), i3)
    nspec = pl.BlockSpec((nb, 1, nkv), i3)
    tspec = pl.BlockSpec((1, nkv, nb), i3)
    qspec = pl.BlockSpec((nb, nh, hd), i3)
    cols = lambda z: z.reshape(b // nb, nb, nkv).transpose(0, 2, 1)
    return pl.pallas_call(
        _swa_step_kernel,
        grid=(b // nb,),
        in_specs=[qspec, nspec, nspec, tspec, tspec, cspec, cspec, pl.BlockSpec(sink.shape, lambda i: (0, 0, 0))],
        out_specs=[qspec, cspec, cspec],
        out_shape=[jax.ShapeDtypeStruct(q.shape, F32), jax.ShapeDtypeStruct(cache_k.shape, F32),
                   jax.ShapeDtypeStruct(cache_v.shape, F32)],
        compiler_params=pltpu.CompilerParams(dimension_semantics=("parallel",)),
        name="swa_step",
    )(q, kn.reshape(b, 1, nkv), vn.reshape(b, 1, nkv), cols(kn), cols(vn), cache_k, cache_v, sink)


def kernel(x_prompt, x_sample, c_prompt, c_sample, state_wkv, state_shift, cache_k, cache_v, norm1_g, norm2_g, ada_w, ada_b, mlp_up, mlp_down, final_g, rw_mix, rw_wr, rw_wk, rw_wv, rw_wo, rw_w0, rw_w1, rw_w2, rw_a0, rw_a1, rw_a2, rw_g1, rw_g2, rw_kk, rw_ka, rw_rk, rw_lnx_g, rw_lnx_b, at_wqkv, at_bqkv, at_wo, at_sink):
    bp, seq, d = x_prompt.shape
    bs = x_sample.shape[0]
    assert x_sample.shape[1] == 1 and d % GW == 0 and d == rw_wr.shape[1]
    for rows in (TM_PROJ, TM_POST, TM_POST_RWKV, WKV_CHUNKS_PER_STEP * CHUNK, SWA_BLOCKS_PER_STEP * ATT_BLOCK):
        assert seq % rows == 0, (seq, rows)
    assert bs % 8 == 0 and cache_k.shape[2] <= WINDOW
    nh = d // HEAD_DIM
    tp = bp * seq
    row = lambda vec: vec.reshape(1, -1)

    n_c = bp + bs
    pad = (-n_c) % 8
    c_all = jnp.concatenate([c_prompt, c_sample, jnp.zeros((pad, d), F32)], axis=0)
    ada = _ada(c_all, ada_w, ada_b)
    mod_p = [ada[i, :bp].reshape(bp, N_ADA, d) for i in range(2)]
    mod_s = [ada[i, bp:bp + bs] for i in range(2)]

    xp = x_prompt.reshape(tp, d)
    xs = x_sample.reshape(bs, d)

    head_id = np.arange(GW) // HEAD_DIM
    ones_bd = jnp.asarray(head_id[:, None] == head_id[None, :], BF16)

    rwp = dict(mix=rw_mix[0], wr=_bf(rw_wr[0]), wk=_bf(rw_wk[0]), wv=_bf(rw_wv[0]),
               w1=_bf(rw_w1[0]), w2=_bf(rw_w2[0]), w0=row(rw_w0[0]),
               a1=_bf(rw_a1[0]), a2=_bf(rw_a2[0]), a0=row(rw_a0[0]),
               g1=_bf(rw_g1[0]), g2=_bf(rw_g2[0]), kk=row(rw_kk[0]), ka=row(rw_ka[0]), ones=ones_bd)
    n1 = row(norm1_g[0])
    post_rw = (row(rw_lnx_g[0]), row(rw_lnx_b[0]), row(rw_rk[0]), ones_bd)
    wo0, up, down = _bf(rw_wo[0]), _bf(mlp_up), _bf(mlp_down)
    n2 = row(norm2_g[0])

    r, k, v, lw, kk, a, g, hlast = _rwkv_proj(xp, None, mod_p[0], n1, rwp, TM_PROJ, seq)
    shift_p = hlast.reshape(bp, seq // TM_PROJ, 8, d)[:, -1, -1]
    y, st = _wkv_seq(r, k, v, lw, kk, a, bp, seq)
    wkv_p = st.reshape(bp, d // GW, HEAD_DIM, HPG, HEAD_DIM).transpose(0, 1, 3, 2, 4).reshape(bp, nh, HEAD_DIM, HEAD_DIM)
    (xp,) = _post(xp, y, (r, k, v, g), mod_p[0], False, post_rw, wo0, n2, 0, up, down, None, TM_POST_RWKV, seq)

    rs, ks, vs, lws, kks, as_, gs, shift_s = _rwkv_proj(xs, state_shift[0], mod_s[0], n1, rwp, bs, 1)
    vecs = jnp.stack([rs, jnp.exp(lws), ks, vs, -kks, kks * as_]).transpose(0, 2, 1).reshape(6, nh, HEAD_DIM, bs)
    st_s, ys = _wkv_step(state_wkv[0].transpose(1, 2, 3, 0), *vecs)
    wkv_s = st_s.transpose(3, 0, 1, 2)
    (xs,) = _post(xs, ys.reshape(d, bs).T, (rs, ks, vs, gs), mod_s[0], True, post_rw, wo0, n2, 0, up, down, None, bs, 1)

    wqkv, bqkv, wo1 = _bf(at_wqkv[0]), row(at_bqkv[0]), _bf(at_wo[0])
    n1, n2 = row(norm1_g[1]), row(norm2_g[1])
    fg = row(final_g)
    nkv = KV_HEADS * HEAD_DIM

    q, kq, vq = _qkv(xp, mod_p[1], False, n1, wqkv, bqkv, _rope_tables(seq, 0, 1), TM_PROJ, seq)
    o = _swa_seq(q, kq, vq, row(at_sink[0]), bp, seq)
    keep = min(WINDOW, seq)
    k_p = kq.reshape(bp, seq, nkv)[:, seq - keep:].reshape(bp, keep, KV_HEADS, HEAD_DIM)
    v_p = vq.reshape(bp, seq, nkv)[:, seq - keep:].reshape(bp, keep, KV_HEADS, HEAD_DIM)
    xp, y_p = _post(xp, o, None, mod_p[1], False, None, wo1, n2, 1, up, down, fg, TM_POST, seq)

    qs, kn, vn = _qkv(xs, mod_s[1], True, n1, wqkv, bqkv, _rope_tables(bs, PAST_LEN, 0), bs, 1)
    w_buf = cache_k.shape[2]
    to_minor = lambda c: c.transpose(0, 2, 3, 1).reshape(bs, nkv, w_buf)
    from_minor = lambda c: c.reshape(bs, KV_HEADS, HEAD_DIM, w_buf).transpose(0, 3, 1, 2)
    os_, k_s, v_s = _swa_step(qs.reshape(bs, nh, HEAD_DIM), kn, vn, to_minor(cache_k[0]), to_minor(cache_v[0]),
                              at_sink[0].reshape(KV_HEADS, GROUP, 1))
    xs, y_s = _post(xs, os_.reshape(bs, d), None, mod_s[1], True, None, wo1, n2, 1, up, down, fg, bs, 1)

    return (y_p.reshape(bp, seq, d), y_s.reshape(bs, 1, d),
            wkv_p[None], wkv_s[None], shift_p[None], shift_s[None],
            k_p[None], from_minor(k_s)[None], v_p[None], from_minor(v_s)[None])
```

```python
import functools
import math

import jax
import jax.numpy as jnp
import numpy as np
from jax import lax
from jax.experimental import pallas as pl
from jax.experimental.pallas import tpu as pltpu

F32 = jnp.float32
BF16 = jnp.bfloat16

HEAD_DIM = 64
KV_HEADS = 4
GROUP = 4
WINDOW = 128
ATT_BLOCK = 128
ROPE_DIM = HEAD_DIM // 4
ROPE_THETA = 500000.0
PAST_LEN = 8192
NORM_EPS = 1e-6
LNX_EPS = 64e-5
N_ADA = 6

LANE = 128
V7X_VMEM_BYTES = 64 * 1024 * 1024
VMEM_LIMIT = V7X_VMEM_BYTES * 7 // 8

CHUNK = 64
HPG = 2
GW = HPG * HEAD_DIM

TM_PROJ = 512
TM_SUB = 256
TM_POST_RWKV = 512
TM_POST = 512
MLP_FF_CHUNK = 1024
WKV_CHUNKS_PER_STEP = 4
ADA_COLS = 1024
ROPE_ROWS = 512
DECODE_ROWS = 16
SWA_BLOCKS_PER_STEP = 4


def _bf(x):
    return x.astype(BF16)


def _dot(a, b):
    return jnp.dot(a, b, preferred_element_type=F32)


def _dot_nt(a, b):
    return lax.dot_general(a, b, (((1,), (1,)), ((), ())), preferred_element_type=F32)


def _rms(x, g):
    ms = jnp.mean(x * x, axis=-1, keepdims=True)
    return x * lax.rsqrt(ms + NORM_EPS) * g


def _prenorm(x, g, shift, scale):
    return _rms(x, g) * (1.0 + scale) + shift


def _mod(mod_ref, per_row, idx, d):
    if per_row:
        return mod_ref[:, idx * d:(idx + 1) * d]
    return mod_ref[0, idx:idx + 1, :]


def _pieces(x, n):
    out = []
    rem = x
    for i in range(n):
        p = rem.astype(BF16)
        out.append(p)
        if i + 1 < n:
            rem = rem - p.astype(F32)
    return out


def _head_sum(val, ones):
    gw = ones.shape[0]
    cols = [_dot(_bf(val[:, j * gw:(j + 1) * gw]), ones) for j in range(val.shape[1] // gw)]
    return jnp.concatenate(cols, axis=1)


def _ada_kernel(c_ref, w_ref, b_ref, o_ref):
    c = c_ref[...]
    s = c * jax.nn.sigmoid(c)
    o_ref[0] = _dot(_bf(s), _bf(w_ref[0])) + b_ref[0]


def _ada(c_all, ada_w, ada_b):
    depth, d, n = ada_w.shape
    m = c_all.shape[0]
    tn = ADA_COLS
    return pl.pallas_call(
        _ada_kernel,
        grid=(depth, n // tn),
        in_specs=[
            pl.BlockSpec((m, d), lambda i, j: (0, 0)),
            pl.BlockSpec((1, d, tn), lambda i, j: (i, 0, j)),
            pl.BlockSpec((1, 1, tn), lambda i, j: (i, 0, j)),
        ],
        out_specs=pl.BlockSpec((1, m, tn), lambda i, j: (i, 0, j)),
        out_shape=jax.ShapeDtypeStruct((depth, m, n), F32),
        compiler_params=pltpu.CompilerParams(dimension_semantics=("parallel", "parallel")),
        name="ada",
    )(c_all, ada_w, ada_b.reshape(depth, 1, n))


def _rwkv_proj_kernel(per_row, tiles_per_seq,
                      x_ref, xp_ref, mod_ref, ng_ref, mix_ref, wr_ref, wk_ref, wv_ref,
                      w1_ref, w2_ref, w0_ref, a1_ref, a2_ref, a0_ref, g1_ref, g2_ref,
                      kk_ref, ka_ref, ones_ref,
                      r_o, k_o, v_o, lw_o, kk_o, a_o, g_o, h_o, h_scr, xx_scr):
    d = x_ref.shape[1]
    tm = x_ref.shape[0]
    ng = ng_ref[...]
    shift = _mod(mod_ref, per_row, 0, d)
    scale = _mod(mod_ref, per_row, 1, d)
    h = _prenorm(x_ref[...], ng, shift, scale)
    if per_row:
        prev = xp_ref[...]
        h_o[...] = h
    else:
        hp = _prenorm(xp_ref[...], ng, shift, scale)
        first = (pl.program_id(0) % tiles_per_seq) == 0
        prow = jnp.where(first, 0.0, hp[7:8, :])
        rowid = lax.broadcasted_iota(jnp.int32, h.shape, 0)
        prev = jnp.where(rowid == 0, prow, pltpu.roll(h, 1, axis=0))
        h_o[0] = h[tm - 8:tm, :]
    h_scr[...] = h
    xx_scr[...] = prev - h

    sub = min(tm, TM_SUB)
    subs = [slice(i, i + sub) for i in range(0, tm, sub)]

    def mixed(j):
        return [_bf(h_scr[sl, :] + xx_scr[sl, :] * mix_ref[j:j + 1, :]) for sl in subs]

    def each(f, *cols):
        return [f(*vals) for vals in zip(*cols)]

    for sl, r in zip(subs, each(lambda m: _dot(m, wr_ref[...]), mixed(0))):
        r_o[sl, :] = r
    t1 = each(lambda m: _bf(jnp.tanh(_dot(m, w1_ref[...]))), mixed(1))
    for sl, wl in zip(subs, each(lambda t: w0_ref[...] + _dot(t, w2_ref[...]), t1)):
        lw_o[sl, :] = jax.nn.sigmoid(wl) * (-math.exp(-0.5))
    t4 = each(lambda m: _bf(_dot(m, a1_ref[...])), mixed(4))
    a = each(lambda t: jax.nn.sigmoid(a0_ref[...] + _dot(t, a2_ref[...])), t4)
    for sl, ai in zip(subs, a):
        a_o[sl, :] = ai
    k = each(lambda m: _dot(m, wk_ref[...]), mixed(2))
    kk = each(lambda ki: ki * kk_ref[...], k)
    ss = each(lambda kki: _head_sum(kki * kki, ones_ref[...]), kk)
    for sl, ki, ai, kki, ssi in zip(subs, k, a, kk, ss):
        k_o[sl, :] = ki * (1.0 + (ai - 1.0) * ka_ref[...])
        kk_o[sl, :] = kki * lax.rsqrt(jnp.maximum(ssi, 1e-24))
    for sl, v in zip(subs, each(lambda m: _dot(m, wv_ref[...]), mixed(3))):
        v_o[sl, :] = v
    t5 = each(lambda m: _bf(jax.nn.sigmoid(_dot(m, g1_ref[...]))), mixed(5))
    for sl, g in zip(subs, each(lambda t: _dot(t, g2_ref[...]), t5)):
        g_o[sl, :] = g


def _const_spec(shape):
    nd = len(shape)
    return pl.BlockSpec(shape, lambda *_: (0,) * nd, pipeline_mode=pl.Buffered(1))


def _rwkv_proj(x, prev_rows, mod, ng, p, tm, seq_len):
    t, d = x.shape
    per_row = prev_rows is not None
    nt = t // tm
    tiles_per_seq = 1 if per_row else seq_len // tm
    row = lambda i: (i, 0)
    if per_row:
        xp, xp_spec = prev_rows, pl.BlockSpec((tm, d), row)
        mod_spec = pl.BlockSpec((tm, N_ADA * d), row)
        h_shape, h_spec = jax.ShapeDtypeStruct((t, d), F32), pl.BlockSpec((tm, d), row)
    else:
        xp, xp_spec = x, pl.BlockSpec((8, d), lambda i: (jnp.maximum(i * (tm // 8) - 1, 0), 0))
        mod_spec = pl.BlockSpec((1, N_ADA, d), lambda i: (i // tiles_per_seq, 0, 0))
        h_shape, h_spec = jax.ShapeDtypeStruct((nt, 8, d), F32), pl.BlockSpec((1, 8, d), lambda i: (i, 0, 0))
    consts = [ng, p["mix"], p["wr"], p["wk"], p["wv"], p["w1"], p["w2"], p["w0"], p["a1"], p["a2"], p["a0"],
              p["g1"], p["g2"], p["kk"], p["ka"], p["ones"]]
    act = jax.ShapeDtypeStruct((t, d), F32)
    return pl.pallas_call(
        functools.partial(_rwkv_proj_kernel, per_row, tiles_per_seq),
        grid=(nt,),
        in_specs=[pl.BlockSpec((tm, d), row), xp_spec, mod_spec] + [_const_spec(c.shape) for c in consts],
        out_specs=[pl.BlockSpec((tm, d), row)] * 7 + [h_spec],
        out_shape=[act] * 7 + [h_shape],
        scratch_shapes=[pltpu.VMEM((tm, d), F32)] * 2,
        compiler_params=pltpu.CompilerParams(dimension_semantics=("parallel",), vmem_limit_bytes=VMEM_LIMIT),
        name="rwkv_proj_step" if per_row else "rwkv_proj_seq",
    )(x, xp, mod, *consts)


def _head_of_lane(shape, axis):
    return lax.broadcasted_iota(jnp.int32, shape, axis) // HEAD_DIM


def _bd(p, low_half):
    zero = jnp.zeros((), p.dtype)
    zeros = jnp.zeros((p.shape[0], LANE), p.dtype)
    per_col = LANE // HEAD_DIM
    ncol = GW // LANE
    blocks = []
    for h in range(HPG):
        c = h // per_col
        col = p[:, c * LANE:(c + 1) * LANE]
        col = jnp.where(low_half, col, zero) if h % per_col == 0 else jnp.where(low_half, zero, col)
        blocks.append(jnp.concatenate([col if j == c else zeros for j in range(ncol)], axis=1))
    return jnp.concatenate(blocks, axis=0)


def _wkv_prep_chunks(rs, ks, vs, lws, kks, as_, masks):
    tri, diag, strict, incl, eye, blk8, levels, lane_head = masks
    n = len(rs)

    def each(f, *args):
        return [f(*[a[i] for a in args]) for i in range(n)]

    def pp(xs, ys):
        rhs = each(lambda y: _bd(_bf(y), diag), ys)
        return each(lambda x, b: _dot(_bf(x), b), xs, rhs)

    def pick_diag(gram):
        out = gram[0:HEAD_DIM, :]
        for hh in range(1, HPG):
            out = jnp.where(lane_head == hh, gram[hh * HEAD_DIM:(hh + 1) * HEAD_DIM, :], out)
        return out

    def add(xs, ys):
        return each(lambda x, y: x + y, xs, ys)

    def masked(m, xs):
        return each(lambda x: jnp.where(m, x, 0.0), xs)

    c2 = each(lambda lw: _dot(tri, jnp.concatenate(_pieces(lw, 2), axis=1)), lws)
    c = each(lambda x: x[:, :GW] + x[:, GW:], c2)
    c_last = each(lambda ci: ci[CHUNK - 1:CHUNK, :], c)
    e_neg = each(lambda ci: jnp.exp(-ci), c)
    e_end = each(lambda ci, cl: jnp.exp(cl - ci), c, c_last)
    bv = each(lambda kk, a: kk * a, kks, as_)
    a_t = each(lambda kk, ci, lw: _bf(-kk * jnp.exp(ci - lw)), kks, c, lws)
    r_t = each(lambda r, ci: r * jnp.exp(ci), rs, c)
    b_t = each(lambda b, e: _bf(b * e), bv, e_neg)
    k_t = each(lambda k, e: _bf(k * e), ks, e_neg)
    b_e = each(lambda b, e: _bf(b * e), bv, e_end)
    k_e = each(lambda k, e: _bf(k * e), ks, e_end)

    bd = lambda y: _bd(_bf(y), diag)
    ar = each(lambda x, y: jnp.concatenate([x, _bf(y)], axis=0), a_t, r_t)
    xbk = each(lambda x, b, k: _dot_nt(x, jnp.concatenate([bd(b), bd(k)], axis=0)), ar, b_t, k_t)
    a_ab = each(lambda x: jnp.where(strict, x[:CHUNK, :GW], 0.0), xbk)
    a_rb = each(lambda x: jnp.where(incl, x[CHUNK:, :GW], 0.0), xbk)
    a_ak = each(lambda x: jnp.where(strict, x[:CHUNK, GW:], 0.0), xbk)
    a_rk = each(lambda x: jnp.where(incl, x[CHUNK:, GW:], 0.0), xbk)
    av = pp(a_ak, vs)

    n0 = masked(blk8, a_ab)
    n2 = pp(n0, n0)
    both = pp(each(lambda x, y: jnp.concatenate([x, eye + y], axis=0), n2, n0), n2)
    n4 = each(lambda x: x[:CHUNK], both)
    t01 = each(lambda x, y: eye + x + y[CHUNK:], n0, both)
    tinv = add(t01, pp(t01, n4))
    b = 8
    for lvl in levels:
        lower = lambda x: jnp.concatenate([x[r0:r0 + b] for r0 in range(b, CHUNK, 2 * b)], axis=0)
        z = pp(pp(each(lower, tinv), masked(lvl, a_ab)), tinv)

        def merged(t, zi):
            rows = []
            for j, r0 in enumerate(range(0, CHUNK, 2 * b)):
                rows += [t[r0:r0 + b], t[r0 + b:r0 + 2 * b] + zi[j * b:(j + 1) * b]]
            return jnp.concatenate(rows, axis=0)

        tinv = each(merged, tinv, z)
        b *= 2

    w12 = each(lambda t, a, x: _dot(_bf(t), jnp.concatenate([bd(a), bd(x)], axis=1)), tinv, a_t, av)
    w1 = each(lambda w: w[:, :GW], w12)
    w2 = each(lambda w: w[:, GW:], w12)
    zeros = jnp.zeros((GW, GW), BF16)

    def qy_rhs(wa, wb, v):
        return jnp.concatenate([jnp.concatenate([bd(wa), bd(wb)], axis=1),
                                jnp.concatenate([zeros, bd(v)], axis=1)], axis=0)

    qy = each(lambda x1, x2, wa, wb, v: _dot(jnp.concatenate([_bf(x1), _bf(x2)], axis=1), qy_rhs(wa, wb, v)),
              a_rb, a_rk, w1, w2, vs)
    qc = each(lambda r, x: r + x[:, :GW], r_t, qy)
    y0 = each(lambda x: x[:, GW:], qy)
    stack = lambda x, y: jnp.concatenate([x, y], axis=0)
    lhs = each(lambda wa, wb, v: _bf(jnp.concatenate([stack(wa, jnp.zeros_like(wa)).T, stack(wb, v).T], axis=0)),
               w1, w2, vs)
    gram = each(lambda x, b, k: _dot(x, stack(b, k)), lhs, b_e, k_e)
    mc = each(lambda gm: pick_diag(gm[:GW]), gram)
    cc = each(lambda gm: pick_diag(gm[GW:]), gram)
    return qc, y0, mc, cc, c_last


def _wkv_masks():
    t = lax.broadcasted_iota(jnp.int32, (CHUNK, GW), 0)
    s = lax.broadcasted_iota(jnp.int32, (CHUNK, GW), 1) % HEAD_DIM
    tri = (lax.broadcasted_iota(jnp.int32, (CHUNK, CHUNK), 1)
           <= lax.broadcasted_iota(jnp.int32, (CHUNK, CHUNK), 0)).astype(BF16)
    diag = lax.broadcasted_iota(jnp.int32, (CHUNK, LANE), 1) < HEAD_DIM
    strict = s < t
    incl = s <= t
    eye = (s == t).astype(F32)
    blk8 = strict & ((s // 8) == (t // 8))
    levels = []
    b = 8
    while b < CHUNK:
        levels.append(((s // (2 * b)) == (t // (2 * b))) & ((s // b) != (t // b)) & strict)
        b *= 2
    lane_head = _head_of_lane((HEAD_DIM, GW), 1)
    return tri, diag, strict, incl, eye, blk8, levels, lane_head


def _rwkv_epilogue(y, r, k, v, g, lg, lb, rk, ones):
    inv_n = 1.0 / HEAD_DIM
    mu = _head_sum(y, ones) * inv_n
    dy = y - mu
    var = _head_sum(dy * dy, ones) * inv_n
    yn = dy * lax.rsqrt(var + LNX_EPS) * lg + lb
    bonus = _head_sum(r * k * rk, ones) * v
    return (yn + bonus) * g


def _wkv_seq_kernel(nsub, ng, r_ref, k_ref, v_ref, lw_ref, kk_ref, a_ref, y_o, s_o, s_scr):
    step = pl.program_id(1)

    @pl.when(step == 0)
    def _():
        s_scr[...] = jnp.zeros_like(s_scr)

    masks = _wkv_masks()
    diag = masks[1]
    sls = [slice(u * CHUNK, (u + 1) * CHUNK) for u in range(nsub)]
    gls = [slice(g * GW, (g + 1) * GW) for g in range(ng)]
    states = [s_scr[:, gl] for gl in gls]

    chunks = lambda ref: [ref[sls[u], gls[g]] for u in range(nsub) for g in range(ng)]
    qc, y0, mc, cc, cl = _wkv_prep_chunks(chunks(r_ref), chunks(k_ref), chunks(v_ref), chunks(lw_ref),
                                          chunks(kk_ref), chunks(a_ref), masks)
    for u in range(nsub):
        sb = [_bf(s) for s in states]
        upd = [_dot(sb[g], _bd(_bf(mc[u * ng + g]), diag)) for g in range(ng)]
        yy = [_dot_nt(_bf(qc[u * ng + g]), _bd(sb[g], diag)) for g in range(ng)]
        for g in range(ng):
            i = u * ng + g
            states[g] = states[g] * jnp.exp(cl[i]) + upd[g] + cc[i]
            y_o[sls[u], gls[g]] = yy[g] + y0[i]
    for g, gl in enumerate(gls):
        s_scr[:, gl] = states[g]

    @pl.when(step == pl.num_programs(1) - 1)
    def _():
        for g in range(ng):
            s_o[0, g] = states[g]


def _wkv_seq(r, k, v, lw, kk, a, batch, seq_len):
    t, d = r.shape
    nsub = WKV_CHUNKS_PER_STEP
    rows = nsub * CHUNK
    ng = d // GW
    steps = seq_len // rows
    blk = pl.BlockSpec((rows, d), lambda i, j: (i * steps + j, 0))
    return pl.pallas_call(
        functools.partial(_wkv_seq_kernel, nsub, ng),
        grid=(batch, steps),
        in_specs=[blk] * 6,
        out_specs=[blk, pl.BlockSpec((1, ng, HEAD_DIM, GW), lambda i, j: (i, 0, 0, 0))],
        out_shape=[jax.ShapeDtypeStruct((t, d), F32), jax.ShapeDtypeStruct((batch, ng, HEAD_DIM, GW), F32)],
        scratch_shapes=[pltpu.VMEM((HEAD_DIM, d), F32)],
        compiler_params=pltpu.CompilerParams(dimension_semantics=("parallel", "arbitrary"),
                                             vmem_limit_bytes=VMEM_LIMIT),
        name="wkv_seq",
    )(r, k, v, lw, kk, a)


def _wkv_step_kernel(s_ref, r_ref, w_ref, k_ref, v_ref, a_ref, b_ref, so_ref, y_ref):
    s = s_ref[0]
    sa = jnp.sum(s * a_ref[0][None], axis=1)
    s = s * w_ref[0][None] + sa[:, None, :] * b_ref[0][None] + v_ref[0][:, None, :] * k_ref[0][None]
    so_ref[0] = s
    y_ref[0] = jnp.sum(s * r_ref[0][None], axis=1)


def _wkv_step(state, r, decay, k, v, a, b):
    nh, n, _, nb = state.shape
    sspec = pl.BlockSpec((1, n, n, nb), lambda i: (i, 0, 0, 0))
    vspec = pl.BlockSpec((1, n, nb), lambda i: (i, 0, 0))
    return pl.pallas_call(
        _wkv_step_kernel,
        grid=(nh,),
        in_specs=[sspec] + [vspec] * 6,
        out_specs=[sspec, vspec],
        out_shape=[jax.ShapeDtypeStruct(state.shape, F32), jax.ShapeDtypeStruct((nh, n, nb), F32)],
        compiler_params=pltpu.CompilerParams(dimension_semantics=("parallel",)),
        name="wkv_step",
    )(state, r, decay, k, v, a, b)


def _post_kernel(rwkv, final, per_row, *refs):
    refs = list(refs)
    x_ref = refs.pop(0)
    z_ref = refs.pop(0)
    if rwkv:
        r_ref, k_ref, v_ref, g_ref = refs[:4]
        refs = refs[4:]
    mod_ref = refs.pop(0)
    if rwkv:
        lg_ref, lb_ref, rk_ref, ones_ref = refs[:4]
        refs = refs[4:]
    wo_ref, n2_ref, up_ref, down_ref = refs[:4]
    refs = refs[4:]
    if final:
        fg_ref = refs.pop(0)
    x_o = refs.pop(0)
    d = x_ref.shape[1]

    z = z_ref[...]
    if rwkv:
        z = _rwkv_epilogue(z, r_ref[...], k_ref[...], v_ref[...], g_ref[...],
                           lg_ref[...], lb_ref[...], rk_ref[...], ones_ref[...])
    out = _dot(_bf(z), wo_ref[...])
    x1 = x_ref[...] + _mod(mod_ref, per_row, 2, d) * out
    h2 = _bf(_prenorm(x1, n2_ref[...], _mod(mod_ref, per_row, 3, d), _mod(mod_ref, per_row, 4, d)))
    dff = up_ref.shape[2]
    fc = MLP_FF_CHUNK
    acc = None
    for c in range(dff // fc):
        hid = jnp.maximum(_dot(h2, up_ref[0, :, c * fc:(c + 1) * fc]), 0.0)
        part = _dot(_bf(hid * hid), down_ref[0, c * fc:(c + 1) * fc, :])
        acc = part if acc is None else acc + part
    x2 = x1 + _mod(mod_ref, per_row, 5, d) * acc
    x_o[...] = x2
    if final:
        refs[0][...] = _rms(x2, fg_ref[...])


def _post(x, z, extra, mod, per_row, rw, wo, n2g, layer, up, down, final_g, tm, seq_len):
    t, d = x.shape
    rwkv = rw is not None
    final = final_g is not None
    row = lambda i: (i, 0)
    tile = pl.BlockSpec((tm, d), row)
    tiles_per_seq = 1 if per_row else seq_len // tm
    mod_spec = (pl.BlockSpec((tm, N_ADA * d), row) if per_row
                else pl.BlockSpec((1, N_ADA, d), lambda i: (i // tiles_per_seq, 0, 0)))
    args = [x, z]
    specs = [tile, tile]
    if rwkv:
        args += list(extra)
        specs += [tile] * 4
    args.append(mod)
    specs.append(mod_spec)
    layer_spec = lambda w: pl.BlockSpec((1,) + w.shape[1:], lambda i: (layer, 0, 0), pipeline_mode=pl.Buffered(1))
    consts = (list(rw) if rwkv else []) + [wo, n2g]
    args += consts + [up, down] + ([final_g] if final else [])
    specs += ([_const_spec(c.shape) for c in consts] + [layer_spec(up), layer_spec(down)]
              + ([_const_spec(final_g.shape)] if final else []))
    act = jax.ShapeDtypeStruct((t, d), F32)
    outs = pl.pallas_call(
        functools.partial(_post_kernel, rwkv, final, per_row),
        grid=(t // tm,),
        in_specs=specs,
        out_specs=[tile] * (2 if final else 1),
        out_shape=[act] * (2 if final else 1),
        compiler_params=pltpu.CompilerParams(dimension_semantics=("parallel",), vmem_limit_bytes=VMEM_LIMIT),
        name=("post_rwkv" if rwkv else "post") + ("_final" if final else "") + ("_step" if per_row else "_seq"),
    )(*args)
    return outs


def _rope_table_kernel(base, step, invf_ref, c_o, sm_o, sp_o):
    shape = c_o.shape
    row = lax.broadcasted_iota(jnp.int32, shape, 0) + pl.program_id(0) * shape[0]
    pos = (base + step * row).astype(F32)
    ang = pos * invf_ref[...]
    lane = lax.broadcasted_iota(jnp.int32, shape, 1) % HEAD_DIM
    cos = jnp.cos(ang)
    sin = jnp.sin(ang)
    half = ROPE_DIM // 2
    c_o[...] = jnp.where(lane < ROPE_DIM, cos, 1.0)
    sm_o[...] = jnp.where(lane < half, -sin, 0.0)
    sp_o[...] = jnp.where((lane >= half) & (lane < ROPE_DIM), sin, 0.0)


def _rope_tables(n, base, step):
    half = ROPE_DIM // 2
    inv_freq = np.float32(ROPE_THETA) ** (-np.arange(half, dtype=np.float32) / np.float32(half))
    lane = np.arange(LANE) % HEAD_DIM
    invf = jnp.asarray(np.where(lane < ROPE_DIM, inv_freq[lane % half], 0.0).reshape(1, LANE), F32)
    tr = min(n, ROPE_ROWS)
    spec = pl.BlockSpec((tr, LANE), lambda i: (i, 0))
    shp = jax.ShapeDtypeStruct((n, LANE), F32)
    return pl.pallas_call(
        functools.partial(_rope_table_kernel, base, step),
        grid=(n // tr,),
        in_specs=[pl.BlockSpec((1, LANE), lambda i: (0, 0))],
        out_specs=[spec] * 3,
        out_shape=[shp] * 3,
        compiler_params=pltpu.CompilerParams(dimension_semantics=("parallel",)),
        name="rope_tables",
    )(invf)


def _qkv_kernel(per_row, x_ref, mod_ref, ng_ref, w_ref, b_ref, c_ref, sm_ref, sp_ref, q_o, k_o, v_o):
    tm, d = x_ref.shape
    nq, nk = q_o.shape[1], k_o.shape[1]
    sub = min(tm, TM_SUB)
    subs = [slice(i, i + sub) for i in range(0, tm, sub)]
    shift, scale = _mod(mod_ref, per_row, 0, d), _mod(mod_ref, per_row, 1, d)
    rows = lambda m, sl: m[sl, :] if per_row else m
    h = [_bf(_prenorm(x_ref[sl, :], ng_ref[...], rows(shift, sl), rows(scale, sl))) for sl in subs]
    qkv = [_dot(hh, w_ref[...]) + b_ref[...] for hh in h]
    for sl, z in zip(subs, qkv):
        c, sm, sp = c_ref[sl, :], sm_ref[sl, :], sp_ref[sl, :]

        def rope_into(o_ref, off, width):
            for j in range(width // LANE):
                zc = z[:, off + j * LANE:off + (j + 1) * LANE]
                o_ref[sl, j * LANE:(j + 1) * LANE] = (zc * c + pltpu.roll(zc, LANE - ROPE_DIM // 2, axis=1) * sm
                                                      + pltpu.roll(zc, ROPE_DIM // 2, axis=1) * sp)

        rope_into(q_o, 0, nq)
        rope_into(k_o, nq, nk)
        v_o[sl, :] = z[:, nq + nk:]


def _qkv(x, mod, per_row, ng, w, b, tables, tm, seq_len):
    t, d = x.shape
    nkv = KV_HEADS * HEAD_DIM
    row = lambda i: (i, 0)
    tiles_per_seq = 1 if per_row else seq_len // tm
    mod_spec = (pl.BlockSpec((tm, N_ADA * d), row) if per_row
                else pl.BlockSpec((1, N_ADA, d), lambda i: (i // tiles_per_seq, 0, 0)))
    tab_spec = pl.BlockSpec((tm, LANE), lambda i: (i % tiles_per_seq, 0))
    return pl.pallas_call(
        functools.partial(_qkv_kernel, per_row),
        grid=(t // tm,),
        in_specs=[pl.BlockSpec((tm, d), row), mod_spec, _const_spec(ng.shape), _const_spec(w.shape),
                  _const_spec(b.shape)] + [tab_spec] * 3,
        out_specs=[pl.BlockSpec((tm, d), row), pl.BlockSpec((tm, nkv), row), pl.BlockSpec((tm, nkv), row)],
        out_shape=[jax.ShapeDtypeStruct((t, d), F32), jax.ShapeDtypeStruct((t, nkv), F32),
                   jax.ShapeDtypeStruct((t, nkv), F32)],
        compiler_params=pltpu.CompilerParams(dimension_semantics=("parallel",), vmem_limit_bytes=VMEM_LIMIT),
        name="qkv_step" if per_row else "qkv_seq",
    )(x, mod, ng, w, b, *tables)


def _swa_seq_kernel(q_ref, kc_ref, kp_ref, vc_ref, vp_ref, sink_ref, o_ref):
    n = pl.program_id(1)
    blk = ATT_BLOCK
    nq = q_ref.shape[0] // blk
    kall = jnp.concatenate([kp_ref[...], kc_ref[...]], axis=0)
    vall = jnp.concatenate([vp_ref[...], vc_ref[...]], axis=0)
    j = lax.broadcasted_iota(jnp.int32, (2 * blk, GROUP * blk), 0)
    i = lax.broadcasted_iota(jnp.int32, (2 * blk, GROUP * blk), 1) % blk
    band = (j >= i + (blk - WINDOW)) & (j <= i + blk)
    first = band & ((n > 0) | (j >= blk))
    hsl = [slice(h * HEAD_DIM, (h + 1) * HEAD_DIM) for h in range(KV_HEADS * GROUP)]
    for t in range(nq):
        _swa_block(q_ref[t * blk:(t + 1) * blk, :], kall[t * blk:(t + 2) * blk], vall[t * blk:(t + 2) * blk],
                   first if t == 0 else band, sink_ref, o_ref, slice(t * blk, (t + 1) * blk), hsl)


def _swa_block(q, kall, vall, mask, sink_ref, o_ref, rows, hsl):
    blk = ATT_BLOCK
    kvs = range(KV_HEADS)
    eye_d = (lax.broadcasted_iota(jnp.int32, (HEAD_DIM, HEAD_DIM), 0)
             == lax.broadcasted_iota(jnp.int32, (HEAD_DIM, HEAD_DIM), 1)).astype(BF16)
    eye_q = (lax.broadcasted_iota(jnp.int32, (blk, blk), 0)
             == lax.broadcasted_iota(jnp.int32, (blk, blk), 1)).astype(BF16)
    ones = jnp.ones((16, 2 * blk), BF16)
    scale = HEAD_DIM ** -0.5
    kh = [_bf(kall[:, hsl[c]]) for c in kvs]
    vte = [jnp.concatenate([_bf(_dot_nt(eye_d, _bf(vall[:, hsl[c]]))), ones], axis=0) for c in kvs]
    qg = [_bf(jnp.concatenate([q[:, hsl[c * GROUP + g]] for g in range(GROUP)], axis=0) * scale) for c in kvs]
    s = [jnp.where(mask, _dot_nt(kh[c], qg[c]), -jnp.inf) for c in kvs]
    sink = [jnp.concatenate([jnp.broadcast_to(sink_ref[0:1, h:h + 1], (1, blk))
                             for h in range(c * GROUP, (c + 1) * GROUP)], axis=1) for c in kvs]
    m = [jnp.maximum(jnp.max(s[c], axis=0, keepdims=True), sink[c]) for c in kvs]
    p = [_bf(jnp.exp(s[c] - m[c])) for c in kvs]
    oe = [_dot(vte[c], p[c]) for c in kvs]
    ot = [_bf(oe[c][:HEAD_DIM] / (oe[c][HEAD_DIM:HEAD_DIM + 1] + jnp.exp(sink[c] - m[c]))) for c in kvs]
    for c in kvs:
        for pair in range(GROUP // 2):
            both = jnp.concatenate([ot[c][:, (2 * pair + e) * blk:(2 * pair + e + 1) * blk] for e in range(2)],
                                   axis=0)
            col = (c * GROUP + 2 * pair) * HEAD_DIM
            o_ref[rows, col:col + 2 * HEAD_DIM] = _dot_nt(eye_q, both).astype(o_ref.dtype)


def _swa_seq(q, k, v, sink, batch, seq_len):
    t, d = q.shape
    nkv = k.shape[1]
    nq = SWA_BLOCKS_PER_STEP
    rows = nq * ATT_BLOCK
    nb = seq_len // rows
    cur = lambda b, n: (b * nb + n, 0)
    prv = lambda b, n: ((b * nb + n) * nq - jnp.minimum(n, 1), 0)
    return pl.pallas_call(
        _swa_seq_kernel,
        grid=(batch, nb),
        in_specs=[pl.BlockSpec((rows, d), cur),
                  pl.BlockSpec((rows, nkv), cur), pl.BlockSpec((ATT_BLOCK, nkv), prv),
                  pl.BlockSpec((rows, nkv), cur), pl.BlockSpec((ATT_BLOCK, nkv), prv),
                  pl.BlockSpec(sink.shape, lambda b, n: (0, 0))],
        out_specs=pl.BlockSpec((rows, d), cur),
        out_shape=jax.ShapeDtypeStruct((t, d), BF16),
        compiler_params=pltpu.CompilerParams(dimension_semantics=("parallel", "parallel")),
        name="swa_seq",
    )(q, k, k, v, v, sink)


def _swa_step_kernel(q_ref, kn_ref, vn_ref, knt_ref, vnt_ref, kc_ref, vc_ref, sink_ref, o_ref, ko_ref, vo_ref):
    nb, _, w = kc_ref.shape
    kc = kc_ref[...]
    vc = vc_ref[...]
    kn = kn_ref[...]
    vn = vn_ref[...]
    q = q_ref[...]
    scale = HEAD_DIM ** -0.5
    kvs = range(KV_HEADS)
    ksl = [slice(h * HEAD_DIM, (h + 1) * HEAD_DIM) for h in kvs]
    gsl = [slice(h * GROUP, (h + 1) * GROUP) for h in kvs]
    qh = [_bf(q[:, gsl[h], :]) for h in kvs]
    s_old = [jnp.einsum("bgd,bdw->bgw", qh[h], _bf(kc[:, ksl[h], :]), preferred_element_type=F32) * scale
             for h in kvs]
    s_new = [jnp.sum(qh[h].astype(F32) * _bf(kn[:, :, ksl[h]]).astype(F32), axis=-1, keepdims=True) * scale
             for h in kvs]
    sink = [sink_ref[h] for h in kvs]
    m = [jnp.maximum(jnp.maximum(jnp.max(s_old[h], axis=-1, keepdims=True), s_new[h]), sink[h]) for h in kvs]
    p_old = [jnp.exp(s_old[h] - m[h]) for h in kvs]
    p_new = [jnp.exp(s_new[h] - m[h]) for h in kvs]
    den = [jnp.sum(p_old[h], axis=-1, keepdims=True) + p_new[h] + jnp.exp(sink[h] - m[h]) for h in kvs]
    o = [jnp.einsum("bgw,bdw->bgd", _bf(p_old[h] / den[h]), _bf(vc[:, ksl[h], :]), preferred_element_type=F32)
         for h in kvs]
    for h in kvs:
        o_ref[:, gsl[h], :] = o[h] + _bf(p_new[h] / den[h]).astype(F32) * _bf(vn[:, :, ksl[h]]).astype(F32)
    last = lax.broadcasted_iota(jnp.int32, kc.shape, 2) == w - 1

    def column(t_ref):
        t = t_ref[0]
        return jnp.stack([jnp.broadcast_to(t[:, b:b + 1], t.shape[:1] + (w,)) for b in range(nb)])

    ko_ref[...] = jnp.where(last, column(knt_ref), pltpu.roll(kc, w - 1, axis=2))
    vo_ref[...] = jnp.where(last, column(vnt_ref), pltpu.roll(vc, w - 1, axis=2))


def _swa_step(q, kn, vn, cache_k, cache_v, sink):
    b, nh, hd = q.shape
    nkv, w = cache_k.shape[1], cache_k.shape[2]
    nb = min(b, DECODE_ROWS)
    i3 = lambda i: (i, 0, 0)
    cspec = pl.BlockSpec((nb, nkv, w), i3)
    nspec = pl.BlockSpec((nb, 1, nkv), i3)
    tspec = pl.BlockSpec((1, nkv, nb), i3)
    qspec = pl.BlockSpec((nb, nh, hd), i3)
    cols = lambda z: z.reshape(b // nb, nb, nkv).transpose(0, 2, 1)
    return pl.pallas_call(
        _swa_step_kernel,
        grid=(b // nb,),
        in_specs=[qspec, nspec, nspec, tspec, tspec, cspec, cspec, pl.BlockSpec(sink.shape, lambda i: (0, 0, 0))],
        out_specs=[qspec, cspec, cspec],
        out_shape=[jax.ShapeDtypeStruct(q.shape, F32), jax.ShapeDtypeStruct(cache_k.shape, F32),
                   jax.ShapeDtypeStruct(cache_v.shape, F32)],
        compiler_params=pltpu.CompilerParams(dimension_semantics=("parallel",)),
        name="swa_step",
    )(q, kn.reshape(b, 1, nkv), vn.reshape(b, 1, nkv), cols(kn), cols(vn), cache_k, cache_v, sink)


def kernel(x_prompt, x_sample, c_prompt, c_sample, state_wkv, state_shift, cache_k, cache_v, norm1_g, norm2_g, ada_w, ada_b, mlp_up, mlp_down, final_g, rw_mix, rw_wr, rw_wk, rw_wv, rw_wo, rw_w0, rw_w1, rw_w2, rw_a0, rw_a1, rw_a2, rw_g1, rw_g2, rw_kk, rw_ka, rw_rk, rw_lnx_g, rw_lnx_b, at_wqkv, at_bqkv, at_wo, at_sink):
    bp, seq, d = x_prompt.shape
    bs = x_sample.shape[0]
    assert x_sample.shape[1] == 1 and d % GW == 0 and d == rw_wr.shape[1]
    for rows in (TM_PROJ, TM_POST, TM_POST_RWKV, WKV_CHUNKS_PER_STEP * CHUNK, SWA_BLOCKS_PER_STEP * ATT_BLOCK):
        assert seq % rows == 0, (seq, rows)
    assert bs % 8 == 0 and cache_k.shape[2] <= WINDOW
    nh = d // HEAD_DIM
    tp = bp * seq
    row = lambda vec: vec.reshape(1, -1)

    n_c = bp + bs
    pad = (-n_c) % 8
    c_all = jnp.concatenate([c_prompt, c_sample, jnp.zeros((pad, d), F32)], axis=0)
    ada = _ada(c_all, ada_w, ada_b)
    mod_p = [ada[i, :bp].reshape(bp, N_ADA, d) for i in range(2)]
    mod_s = [ada[i, bp:bp + bs] for i in range(2)]

    xp = x_prompt.reshape(tp, d)
    xs = x_sample.reshape(bs, d)

    head_id = np.arange(GW) // HEAD_DIM
    ones_bd = jnp.asarray(head_id[:, None] == head_id[None, :], BF16)

    rwp = dict(mix=rw_mix[0], wr=_bf(rw_wr[0]), wk=_bf(rw_wk[0]), wv=_bf(rw_wv[0]),
               w1=_bf(rw_w1[0]), w2=_bf(rw_w2[0]), w0=row(rw_w0[0]),
               a1=_bf(rw_a1[0]), a2=_bf(rw_a2[0]), a0=row(rw_a0[0]),
               g1=_bf(rw_g1[0]), g2=_bf(rw_g2[0]), kk=row(rw_kk[0]), ka=row(rw_ka[0]), ones=ones_bd)
    n1 = row(norm1_g[0])
    post_rw = (row(rw_lnx_g[0]), row(rw_lnx_b[0]), row(rw_rk[0]), ones_bd)
    wo0, up, down = _bf(rw_wo[0]), _bf(mlp_up), _bf(mlp_down)
    n2 = row(norm2_g[0])

    r, k, v, lw, kk, a, g, hlast = _rwkv_proj(xp, None, mod_p[0], n1, rwp, TM_PROJ, seq)
    shift_p = hlast.reshape(bp, seq // TM_PROJ, 8, d)[:, -1, -1]
    y, st = _wkv_seq(r, k, v, lw, kk, a, bp, seq)
    wkv_p = st.reshape(bp, d // GW, HEAD_DIM, HPG, HEAD_DIM).transpose(0, 1, 3, 2, 4).reshape(bp, nh, HEAD_DIM, HEAD_DIM)
    (xp,) = _post(xp, y, (r, k, v, g), mod_p[0], False, post_rw, wo0, n2, 0, up, down, None, TM_POST_RWKV, seq)

    rs, ks, vs, lws, kks, as_, gs, shift_s = _rwkv_proj(xs, state_shift[0], mod_s[0], n1, rwp, bs, 1)
    vecs = jnp.stack([rs, jnp.exp(lws), ks, vs, -kks, kks * as_]).transpose(0, 2, 1).reshape(6, nh, HEAD_DIM, bs)
    st_s, ys = _wkv_step(state_wkv[0].transpose(1, 2, 3, 0), *vecs)
    wkv_s = st_s.transpose(3, 0, 1, 2)
    (xs,) = _post(xs, ys.reshape(d, bs).T, (rs, ks, vs, gs), mod_s[0], True, post_rw, wo0, n2, 0, up, down, None, bs, 1)

    wqkv, bqkv, wo1 = _bf(at_wqkv[0]), row(at_bqkv[0]), _bf(at_wo[0])
    n1, n2 = row(norm1_g[1]), row(norm2_g[1])
    fg = row(final_g)
    nkv = KV_HEADS * HEAD_DIM

    q, kq, vq = _qkv(xp, mod_p[1], False, n1, wqkv, bqkv, _rope_tables(seq, 0, 1), TM_PROJ, seq)
    o = _swa_seq(q, kq, vq, row(at_sink[0]), bp, seq)
    keep = min(WINDOW, seq)
    k_p = kq.reshape(bp, seq, nkv)[:, seq - keep:].reshape(bp, keep, KV_HEADS, HEAD_DIM)
    v_p = vq.reshape(bp, seq, nkv)[:, seq - keep:].reshape(bp, keep, KV_HEADS, HEAD_DIM)
    xp, y_p = _post(xp, o, None, mod_p[1], False, None, wo1, n2, 1, up, down, fg, TM_POST, seq)

    qs, kn, vn = _qkv(xs, mod_s[1], True, n1, wqkv, bqkv, _rope_tables(bs, PAST_LEN, 0), bs, 1)
    w_buf = cache_k.shape[2]
    to_minor = lambda c: c.transpose(0, 2, 3, 1).reshape(bs, nkv, w_buf)
    from_minor = lambda c: c.reshape(bs, KV_HEADS, HEAD_DIM, w_buf).transpose(0, 3, 1, 2)
    os_, k_s, v_s = _swa_step(qs.reshape(bs, nh, HEAD_DIM), kn, vn, to_minor(cache_k[0]), to_minor(cache_v[0]),
                              at_sink[0].reshape(KV_HEADS, GROUP, 1))
    xs, y_s = _post(xs, os_.reshape(bs, d), None, mod_s[1], True, None, wo1, n2, 1, up, down, fg, bs, 1)

    return (y_p.reshape(bp, seq, d), y_s.reshape(bs, 1, d),
            wkv_p[None], wkv_s[None], shift_p[None], shift_s[None],
            k_p[None], from_minor(k_s)[None], v_p[None], from_minor(v_s)[None])
```

```python
import functools
import math

import jax
import jax.numpy as jnp
import numpy as np
from jax import lax
from jax.experimental import pallas as pl
from jax.experimental.pallas import tpu as pltpu

F32 = jnp.float32
BF16 = jnp.bfloat16

HEAD_DIM = 64
KV_HEADS = 4
GROUP = 4
WINDOW = 128
ATT_BLOCK = 128
ROPE_DIM = HEAD_DIM // 4
ROPE_THETA = 500000.0
PAST_LEN = 8192
NORM_EPS = 1e-6
LNX_EPS = 64e-5
N_ADA = 6

LANE = 128
V7X_VMEM_BYTES = 64 * 1024 * 1024
VMEM_LIMIT = V7X_VMEM_BYTES * 7 // 8

CHUNK = 64
HPG = 2
GW = HPG * HEAD_DIM

TM_PROJ = 512
TM_SUB = 256
TM_POST_RWKV = 512
TM_POST = 512
MLP_FF_CHUNK = 1024
WKV_CHUNKS_PER_STEP = 8
ADA_COLS = 2048
ROPE_ROWS = 512
DECODE_ROWS = 32
SWA_BLOCKS_PER_STEP = 4


def _bf(x):
    return x.astype(BF16)


def _dot(a, b):
    return jnp.dot(a, b, preferred_element_type=F32)


def _dot_nt(a, b):
    return lax.dot_general(a, b, (((1,), (1,)), ((), ())), preferred_element_type=F32)


def _rms(x, g):
    ms = jnp.mean(x * x, axis=-1, keepdims=True)
    return x * lax.rsqrt(ms + NORM_EPS) * g


def _prenorm(x, g, shift, scale):
    return _rms(x, g) * (1.0 + scale) + shift


def _mod(mod_ref, per_row, idx, d):
    if per_row:
        return mod_ref[:, idx * d:(idx + 1) * d]
    return mod_ref[0, idx:idx + 1, :]


def _pieces(x, n):
    out = []
    rem = x
    for i in range(n):
        p = rem.astype(BF16)
        out.append(p)
        if i + 1 < n:
            rem = rem - p.astype(F32)
    return out


def _head_sum(val, ones):
    gw = ones.shape[0]
    cols = [_dot(_bf(val[:, j * gw:(j + 1) * gw]), ones) for j in range(val.shape[1] // gw)]
    return jnp.concatenate(cols, axis=1)


def _ada_kernel(c_ref, w_ref, b_ref, o_ref):
    c = c_ref[...]
    s = c * jax.nn.sigmoid(c)
    o_ref[0] = _dot(_bf(s), _bf(w_ref[0])) + b_ref[0]


def _ada(c_all, ada_w, ada_b):
    depth, d, n = ada_w.shape
    m = c_all.shape[0]
    tn = ADA_COLS
    return pl.pallas_call(
        _ada_kernel,
        grid=(depth, n // tn),
        in_specs=[
            pl.BlockSpec((m, d), lambda i, j: (0, 0)),
            pl.BlockSpec((1, d, tn), lambda i, j: (i, 0, j)),
            pl.BlockSpec((1, 1, tn), lambda i, j: (i, 0, j)),
        ],
        out_specs=pl.BlockSpec((1, m, tn), lambda i, j: (i, 0, j)),
        out_shape=jax.ShapeDtypeStruct((depth, m, n), F32),
        compiler_params=pltpu.CompilerParams(dimension_semantics=("parallel", "parallel")),
        name="ada",
    )(c_all, ada_w, ada_b.reshape(depth, 1, n))


def _rwkv_proj_kernel(per_row, tiles_per_seq,
                      x_ref, xp_ref, mod_ref, ng_ref, mix_ref, wr_ref, wk_ref, wv_ref,
                      w1_ref, w2_ref, w0_ref, a1_ref, a2_ref, a0_ref, g1_ref, g2_ref,
                      kk_ref, ka_ref, ones_ref,
                      r_o, k_o, v_o, lw_o, kk_o, a_o, g_o, h_o, h_scr, xx_scr):
    d = x_ref.shape[1]
    tm = x_ref.shape[0]
    ng = ng_ref[...]
    shift = _mod(mod_ref, per_row, 0, d)
    scale = _mod(mod_ref, per_row, 1, d)
    h = _prenorm(x_ref[...], ng, shift, scale)
    if per_row:
        prev = xp_ref[...]
        h_o[...] = h
    else:
        hp = _prenorm(xp_ref[...], ng, shift, scale)
        first = (pl.program_id(0) % tiles_per_seq) == 0
        prow = jnp.where(first, 0.0, hp[7:8, :])
        rowid = lax.broadcasted_iota(jnp.int32, h.shape, 0)
        prev = jnp.where(rowid == 0, prow, pltpu.roll(h, 1, axis=0))
        h_o[0] = h[tm - 8:tm, :]
    h_scr[...] = h
    xx_scr[...] = prev - h

    sub = min(tm, TM_SUB)
    subs = [slice(i, i + sub) for i in range(0, tm, sub)]

    def mixed(j):
        return [_bf(h_scr[sl, :] + xx_scr[sl, :] * mix_ref[j:j + 1, :]) for sl in subs]

    def each(f, *cols):
        return [f(*vals) for vals in zip(*cols)]

    for sl, r in zip(subs, each(lambda m: _dot(m, wr_ref[...]), mixed(0))):
        r_o[sl, :] = r
    t1 = each(lambda m: _bf(jnp.tanh(_dot(m, w1_ref[...]))), mixed(1))
    for sl, wl in zip(subs, each(lambda t: w0_ref[...] + _dot(t, w2_ref[...]), t1)):
        lw_o[sl, :] = jax.nn.sigmoid(wl) * (-math.exp(-0.5))
    t4 = each(lambda m: _bf(_dot(m, a1_ref[...])), mixed(4))
    a = each(lambda t: jax.nn.sigmoid(a0_ref[...] + _dot(t, a2_ref[...])), t4)
    for sl, ai in zip(subs, a):
        a_o[sl, :] = ai
    k = each(lambda m: _dot(m, wk_ref[...]), mixed(2))
    kk = each(lambda ki: ki * kk_ref[...], k)
    ss = each(lambda kki: _head_sum(kki * kki, ones_ref[...]), kk)
    for sl, ki, ai, kki, ssi in zip(subs, k, a, kk, ss):
        k_o[sl, :] = ki * (1.0 + (ai - 1.0) * ka_ref[...])
        kk_o[sl, :] = kki * lax.rsqrt(jnp.maximum(ssi, 1e-24))
    for sl, v in zip(subs, each(lambda m: _dot(m, wv_ref[...]), mixed(3))):
        v_o[sl, :] = v
    t5 = each(lambda m: _bf(jax.nn.sigmoid(_dot(m, g1_ref[...]))), mixed(5))
    for sl, g in zip(subs, each(lambda t: _dot(t, g2_ref[...]), t5)):
        g_o[sl, :] = g


def _const_spec(shape):
    nd = len(shape)
    return pl.BlockSpec(shape, lambda *_: (0,) * nd, pipeline_mode=pl.Buffered(1))


def _rwkv_proj(x, prev_rows, mod, ng, p, tm, seq_len):
    t, d = x.shape
    per_row = prev_rows is not None
    nt = t // tm
    tiles_per_seq = 1 if per_row else seq_len // tm
    row = lambda i: (i, 0)
    if per_row:
        xp, xp_spec = prev_rows, pl.BlockSpec((tm, d), row)
        mod_spec = pl.BlockSpec((tm, N_ADA * d), row)
        h_shape, h_spec = jax.ShapeDtypeStruct((t, d), F32), pl.BlockSpec((tm, d), row)
    else:
        xp, xp_spec = x, pl.BlockSpec((8, d), lambda i: (jnp.maximum(i * (tm // 8) - 1, 0), 0))
        mod_spec = pl.BlockSpec((1, N_ADA, d), lambda i: (i // tiles_per_seq, 0, 0))
        h_shape, h_spec = jax.ShapeDtypeStruct((nt, 8, d), F32), pl.BlockSpec((1, 8, d), lambda i: (i, 0, 0))
    consts = [ng, p["mix"], p["wr"], p["wk"], p["wv"], p["w1"], p["w2"], p["w0"], p["a1"], p["a2"], p["a0"],
              p["g1"], p["g2"], p["kk"], p["ka"], p["ones"]]
    act = jax.ShapeDtypeStruct((t, d), F32)
    return pl.pallas_call(
        functools.partial(_rwkv_proj_kernel, per_row, tiles_per_seq),
        grid=(nt,),
        in_specs=[pl.BlockSpec((tm, d), row), xp_spec, mod_spec] + [_const_spec(c.shape) for c in consts],
        out_specs=[pl.BlockSpec((tm, d), row)] * 7 + [h_spec],
        out_shape=[act] * 7 + [h_shape],
        scratch_shapes=[pltpu.VMEM((tm, d), F32)] * 2,
        compiler_params=pltpu.CompilerParams(dimension_semantics=("parallel",), vmem_limit_bytes=VMEM_LIMIT),
        name="rwkv_proj_step" if per_row else "rwkv_proj_seq",
    )(x, xp, mod, *consts)


def _head_of_lane(shape, axis):
    return lax.broadcasted_iota(jnp.int32, shape, axis) // HEAD_DIM


def _bd(p, low_half):
    zero = jnp.zeros((), p.dtype)
    zeros = jnp.zeros((p.shape[0], LANE), p.dtype)
    per_col = LANE // HEAD_DIM
    ncol = GW // LANE
    blocks = []
    for h in range(HPG):
        c = h // per_col
        col = p[:, c * LANE:(c + 1) * LANE]
        col = jnp.where(low_half, col, zero) if h % per_col == 0 else jnp.where(low_half, zero, col)
        blocks.append(jnp.concatenate([col if j == c else zeros for j in range(ncol)], axis=1))
    return jnp.concatenate(blocks, axis=0)


def _wkv_prep_chunks(rs, ks, vs, lws, kks, as_, masks):
    tri, diag, strict, incl, eye, blk8, levels, lane_head = masks
    n = len(rs)

    def each(f, *args):
        return [f(*[a[i] for a in args]) for i in range(n)]

    def pp(xs, ys):
        rhs = each(lambda y: _bd(_bf(y), diag), ys)
        return each(lambda x, b: _dot(_bf(x), b), xs, rhs)

    def pick_diag(gram):
        out = gram[0:HEAD_DIM, :]
        for hh in range(1, HPG):
            out = jnp.where(lane_head == hh, gram[hh * HEAD_DIM:(hh + 1) * HEAD_DIM, :], out)
        return out

    def add(xs, ys):
        return each(lambda x, y: x + y, xs, ys)

    def masked(m, xs):
        return each(lambda x: jnp.where(m, x, 0.0), xs)

    c2 = each(lambda lw: _dot(tri, jnp.concatenate(_pieces(lw, 2), axis=1)), lws)
    c = each(lambda x: x[:, :GW] + x[:, GW:], c2)
    c_last = each(lambda ci: ci[CHUNK - 1:CHUNK, :], c)
    e_neg = each(lambda ci: jnp.exp(-ci), c)
    e_end = each(lambda ci, cl: jnp.exp(cl - ci), c, c_last)
    bv = each(lambda kk, a: kk * a, kks, as_)
    a_t = each(lambda kk, ci, lw: _bf(-kk * jnp.exp(ci - lw)), kks, c, lws)
    r_t = each(lambda r, ci: r * jnp.exp(ci), rs, c)
    b_t = each(lambda b, e: _bf(b * e), bv, e_neg)
    k_t = each(lambda k, e: _bf(k * e), ks, e_neg)
    b_e = each(lambda b, e: _bf(b * e), bv, e_end)
    k_e = each(lambda k, e: _bf(k * e), ks, e_end)

    bd = lambda y: _bd(_bf(y), diag)
    ar = each(lambda x, y: jnp.concatenate([x, _bf(y)], axis=0), a_t, r_t)
    xbk = each(lambda x, b, k: _dot_nt(x, jnp.concatenate([bd(b), bd(k)], axis=0)), ar, b_t, k_t)
    a_ab = each(lambda x: jnp.where(strict, x[:CHUNK, :GW], 0.0), xbk)
    a_rb = each(lambda x: jnp.where(incl, x[CHUNK:, :GW], 0.0), xbk)
    a_ak = each(lambda x: jnp.where(strict, x[:CHUNK, GW:], 0.0), xbk)
    a_rk = each(lambda x: jnp.where(incl, x[CHUNK:, GW:], 0.0), xbk)
    av = pp(a_ak, vs)

    n0 = masked(blk8, a_ab)
    n2 = pp(n0, n0)
    both = pp(each(lambda x, y: jnp.concatenate([x, eye + y], axis=0), n2, n0), n2)
    n4 = each(lambda x: x[:CHUNK], both)
    t01 = each(lambda x, y: eye + x + y[CHUNK:], n0, both)
    tinv = add(t01, pp(t01, n4))
    b = 8
    for lvl in levels:
        lower = lambda x: jnp.concatenate([x[r0:r0 + b] for r0 in range(b, CHUNK, 2 * b)], axis=0)
        z = pp(pp(each(lower, tinv), masked(lvl, a_ab)), tinv)

        def merged(t, zi):
            rows = []
            for j, r0 in enumerate(range(0, CHUNK, 2 * b)):
                rows += [t[r0:r0 + b], t[r0 + b:r0 + 2 * b] + zi[j * b:(j + 1) * b]]
            return jnp.concatenate(rows, axis=0)

        tinv = each(merged, tinv, z)
        b *= 2

    w12 = each(lambda t, a, x: _dot(_bf(t), jnp.concatenate([bd(a), bd(x)], axis=1)), tinv, a_t, av)
    w1 = each(lambda w: w[:, :GW], w12)
    w2 = each(lambda w: w[:, GW:], w12)
    zeros = jnp.zeros((GW, GW), BF16)

    def qy_rhs(wa, wb, v):
        return jnp.concatenate([jnp.concatenate([bd(wa), bd(wb)], axis=1),
                                jnp.concatenate([zeros, bd(v)], axis=1)], axis=0)

    qy = each(lambda x1, x2, wa, wb, v: _dot(jnp.concatenate([_bf(x1), _bf(x2)], axis=1), qy_rhs(wa, wb, v)),
              a_rb, a_rk, w1, w2, vs)
    qc = each(lambda r, x: r + x[:, :GW], r_t, qy)
    y0 = each(lambda x: x[:, GW:], qy)
    stack = lambda x, y: jnp.concatenate([x, y], axis=0)
    lhs = each(lambda wa, wb, v: _bf(jnp.concatenate([stack(wa, jnp.zeros_like(wa)).T, stack(wb, v).T], axis=0)),
               w1, w2, vs)
    gram = each(lambda x, b, k: _dot(x, stack(b, k)), lhs, b_e, k_e)
    mc = each(lambda gm: pick_diag(gm[:GW]), gram)
    cc = each(lambda gm: pick_diag(gm[GW:]), gram)
    return qc, y0, mc, cc, c_last


def _wkv_masks():
    t = lax.broadcasted_iota(jnp.int32, (CHUNK, GW), 0)
    s = lax.broadcasted_iota(jnp.int32, (CHUNK, GW), 1) % HEAD_DIM
    tri = (lax.broadcasted_iota(jnp.int32, (CHUNK, CHUNK), 1)
           <= lax.broadcasted_iota(jnp.int32, (CHUNK, CHUNK), 0)).astype(BF16)
    diag = lax.broadcasted_iota(jnp.int32, (CHUNK, LANE), 1) < HEAD_DIM
    strict = s < t
    incl = s <= t
    eye = (s == t).astype(F32)
    blk8 = strict & ((s // 8) == (t // 8))
    levels = []
    b = 8
    while b < CHUNK:
        levels.append(((s // (2 * b)) == (t // (2 * b))) & ((s // b) != (t // b)) & strict)
        b *= 2
    lane_head = _head_of_lane((HEAD_DIM, GW), 1)
    return tri, diag, strict, incl, eye, blk8, levels, lane_head


def _rwkv_epilogue(y, r, k, v, g, lg, lb, rk, ones):
    inv_n = 1.0 / HEAD_DIM
    mu = _head_sum(y, ones) * inv_n
    dy = y - mu
    var = _head_sum(dy * dy, ones) * inv_n
    yn = dy * lax.rsqrt(var + LNX_EPS) * lg + lb
    bonus = _head_sum(r * k * rk, ones) * v
    return (yn + bonus) * g


def _wkv_seq_kernel(nsub, ng, r_ref, k_ref, v_ref, lw_ref, kk_ref, a_ref, y_o, s_o, s_scr):
    step = pl.program_id(1)

    @pl.when(step == 0)
    def _():
        s_scr[...] = jnp.zeros_like(s_scr)

    masks = _wkv_masks()
    diag = masks[1]
    sls = [slice(u * CHUNK, (u + 1) * CHUNK) for u in range(nsub)]
    gls = [slice(g * GW, (g + 1) * GW) for g in range(ng)]
    states = [s_scr[:, gl] for gl in gls]

    chunks = lambda ref: [ref[sls[u], gls[g]] for u in range(nsub) for g in range(ng)]
    qc, y0, mc, cc, cl = _wkv_prep_chunks(chunks(r_ref), chunks(k_ref), chunks(v_ref), chunks(lw_ref),
                                          chunks(kk_ref), chunks(a_ref), masks)
    for u in range(nsub):
        sb = [_bf(s) for s in states]
        upd = [_dot(sb[g], _bd(_bf(mc[u * ng + g]), diag)) for g in range(ng)]
        yy = [_dot_nt(_bf(qc[u * ng + g]), _bd(sb[g], diag)) for g in range(ng)]
        for g in range(ng):
            i = u * ng + g
            states[g] = states[g] * jnp.exp(cl[i]) + upd[g] + cc[i]
            y_o[sls[u], gls[g]] = yy[g] + y0[i]
    for g, gl in enumerate(gls):
        s_scr[:, gl] = states[g]

    @pl.when(step == pl.num_programs(1) - 1)
    def _():
        for g in range(ng):
            s_o[0, g] = states[g]


def _wkv_seq(r, k, v, lw, kk, a, batch, seq_len):
    t, d = r.shape
    nsub = WKV_CHUNKS_PER_STEP
    rows = nsub * CHUNK
    ng = d // GW
    steps = seq_len // rows
    blk = pl.BlockSpec((rows, d), lambda i, j: (i * steps + j, 0))
    return pl.pallas_call(
        functools.partial(_wkv_seq_kernel, nsub, ng),
        grid=(batch, steps),
        in_specs=[blk] * 6,
        out_specs=[blk, pl.BlockSpec((1, ng, HEAD_DIM, GW), lambda i, j: (i, 0, 0, 0))],
        out_shape=[jax.ShapeDtypeStruct((t, d), F32), jax.ShapeDtypeStruct((batch, ng, HEAD_DIM, GW), F32)],
        scratch_shapes=[pltpu.VMEM((HEAD_DIM, d), F32)],
        compiler_params=pltpu.CompilerParams(dimension_semantics=("parallel", "arbitrary"),
                                             vmem_limit_bytes=VMEM_LIMIT),
        name="wkv_seq",
    )(r, k, v, lw, kk, a)


def _wkv_step_kernel(s_ref, r_ref, w_ref, k_ref, v_ref, a_ref, b_ref, so_ref, y_ref):
    s = s_ref[0]
    sa = jnp.sum(s * a_ref[0][None], axis=1)
    s = s * w_ref[0][None] + sa[:, None, :] * b_ref[0][None] + v_ref[0][:, None, :] * k_ref[0][None]
    so_ref[0] = s
    y_ref[0] = jnp.sum(s * r_ref[0][None], axis=1)


def _wkv_step(state, r, decay, k, v, a, b):
    nh, n, _, nb = state.shape
    sspec = pl.BlockSpec((1, n, n, nb), lambda i: (i, 0, 0, 0))
    vspec = pl.BlockSpec((1, n, nb), lambda i: (i, 0, 0))
    return pl.pallas_call(
        _wkv_step_kernel,
        grid=(nh,),
        in_specs=[sspec] + [vspec] * 6,
        out_specs=[sspec, vspec],
        out_shape=[jax.ShapeDtypeStruct(state.shape, F32), jax.ShapeDtypeStruct((nh, n, nb), F32)],
        compiler_params=pltpu.CompilerParams(dimension_semantics=("parallel",)),
        name="wkv_step",
    )(state, r, decay, k, v, a, b)


def _post_kernel(rwkv, final, per_row, *refs):
    refs = list(refs)
    x_ref = refs.pop(0)
    z_ref = refs.pop(0)
    if rwkv:
        r_ref, k_ref, v_ref, g_ref = refs[:4]
        refs = refs[4:]
    mod_ref = refs.pop(0)
    if rwkv:
        lg_ref, lb_ref, rk_ref, ones_ref = refs[:4]
        refs = refs[4:]
    wo_ref, n2_ref, up_ref, down_ref = refs[:4]
    refs = refs[4:]
    if final:
        fg_ref = refs.pop(0)
    x_o = refs.pop(0)
    d = x_ref.shape[1]

    z = z_ref[...]
    if rwkv:
        z = _rwkv_epilogue(z, r_ref[...], k_ref[...], v_ref[...], g_ref[...],
                           lg_ref[...], lb_ref[...], rk_ref[...], ones_ref[...])
    out = _dot(_bf(z), wo_ref[...])
    x1 = x_ref[...] + _mod(mod_ref, per_row, 2, d) * out
    h2 = _bf(_prenorm(x1, n2_ref[...], _mod(mod_ref, per_row, 3, d), _mod(mod_ref, per_row, 4, d)))
    dff = up_ref.shape[2]
    fc = MLP_FF_CHUNK
    acc = None
    for c in range(dff // fc):
        hid = jnp.maximum(_dot(h2, up_ref[0, :, c * fc:(c + 1) * fc]), 0.0)
        part = _dot(_bf(hid * hid), down_ref[0, c * fc:(c + 1) * fc, :])
        acc = part if acc is None else acc + part
    x2 = x1 + _mod(mod_ref, per_row, 5, d) * acc
    x_o[...] = x2
    if final:
        refs[0][...] = _rms(x2, fg_ref[...])


def _post(x, z, extra, mod, per_row, rw, wo, n2g, layer, up, down, final_g, tm, seq_len):
    t, d = x.shape
    rwkv = rw is not None
    final = final_g is not None
    row = lambda i: (i, 0)
    tile = pl.BlockSpec((tm, d), row)
    tiles_per_seq = 1 if per_row else seq_len // tm
    mod_spec = (pl.BlockSpec((tm, N_ADA * d), row) if per_row
                else pl.BlockSpec((1, N_ADA, d), lambda i: (i // tiles_per_seq, 0, 0)))
    args = [x, z]
    specs = [tile, tile]
    if rwkv:
        args += list(extra)
        specs += [tile] * 4
    args.append(mod)
    specs.append(mod_spec)
    layer_spec = lambda w: pl.BlockSpec((1,) + w.shape[1:], lambda i: (layer, 0, 0), pipeline_mode=pl.Buffered(1))
    consts = (list(rw) if rwkv else []) + [wo, n2g]
    args += consts + [up, down] + ([final_g] if final else [])
    specs += ([_const_spec(c.shape) for c in consts] + [layer_spec(up), layer_spec(down)]
              + ([_const_spec(final_g.shape)] if final else []))
    act = jax.ShapeDtypeStruct((t, d), F32)
    outs = pl.pallas_call(
        functools.partial(_post_kernel, rwkv, final, per_row),
        grid=(t // tm,),
        in_specs=specs,
        out_specs=[tile] * (2 if final else 1),
        out_shape=[act] * (2 if final else 1),
        compiler_params=pltpu.CompilerParams(dimension_semantics=("parallel",), vmem_limit_bytes=VMEM_LIMIT),
        name=("post_rwkv" if rwkv else "post") + ("_final" if final else "") + ("_step" if per_row else "_seq"),
    )(*args)
    return outs


def _rope_table_kernel(base, step, invf_ref, c_o, sm_o, sp_o):
    shape = c_o.shape
    row = lax.broadcasted_iota(jnp.int32, shape, 0) + pl.program_id(0) * shape[0]
    pos = (base + step * row).astype(F32)
    ang = pos * invf_ref[...]
    lane = lax.broadcasted_iota(jnp.int32, shape, 1) % HEAD_DIM
    cos = jnp.cos(ang)
    sin = jnp.sin(ang)
    half = ROPE_DIM // 2
    c_o[...] = jnp.where(lane < ROPE_DIM, cos, 1.0)
    sm_o[...] = jnp.where(lane < half, -sin, 0.0)
    sp_o[...] = jnp.where((lane >= half) & (lane < ROPE_DIM), sin, 0.0)


def _rope_tables(n, base, step):
    half = ROPE_DIM // 2
    inv_freq = np.float32(ROPE_THETA) ** (-np.arange(half, dtype=np.float32) / np.float32(half))
    lane = np.arange(LANE) % HEAD_DIM
    invf = jnp.asarray(np.where(lane < ROPE_DIM, inv_freq[lane % half], 0.0).reshape(1, LANE), F32)
    tr = min(n, ROPE_ROWS)
    spec = pl.BlockSpec((tr, LANE), lambda i: (i, 0))
    shp = jax.ShapeDtypeStruct((n, LANE), F32)
    return pl.pallas_call(
        functools.partial(_rope_table_kernel, base, step),
        grid=(n // tr,),
        in_specs=[pl.BlockSpec((1, LANE), lambda i: (0, 0))],
        out_specs=[spec] * 3,
        out_shape=[shp] * 3,
        compiler_params=pltpu.CompilerParams(dimension_semantics=("parallel",)),
        name="rope_tables",
    )(invf)


def _qkv_kernel(per_row, x_ref, mod_ref, ng_ref, w_ref, b_ref, c_ref, sm_ref, sp_ref, q_o, k_o, v_o):
    tm, d = x_ref.shape
    nq, nk = q_o.shape[1], k_o.shape[1]
    sub = min(tm, TM_SUB)
    subs = [slice(i, i + sub) for i in range(0, tm, sub)]
    shift, scale = _mod(mod_ref, per_row, 0, d), _mod(mod_ref, per_row, 1, d)
    rows = lambda m, sl: m[sl, :] if per_row else m
    h = [_bf(_prenorm(x_ref[sl, :], ng_ref[...], rows(shift, sl), rows(scale, sl))) for sl in subs]
    qkv = [_dot(hh, w_ref[...]) + b_ref[...] for hh in h]
    for sl, z in zip(subs, qkv):
        c, sm, sp = c_ref[sl, :], sm_ref[sl, :], sp_ref[sl, :]

        def rope_into(o_ref, off, width):
            for j in range(width // LANE):
                zc = z[:, off + j * LANE:off + (j + 1) * LANE]
                o_ref[sl, j * LANE:(j + 1) * LANE] = (zc * c + pltpu.roll(zc, LANE - ROPE_DIM // 2, axis=1) * sm
                                                      + pltpu.roll(zc, ROPE_DIM // 2, axis=1) * sp)

        rope_into(q_o, 0, nq)
        rope_into(k_o, nq, nk)
        v_o[sl, :] = z[:, nq + nk:]


def _qkv(x, mod, per_row, ng, w, b, tables, tm, seq_len):
    t, d = x.shape
    nkv = KV_HEADS * HEAD_DIM
    row = lambda i: (i, 0)
    tiles_per_seq = 1 if per_row else seq_len // tm
    mod_spec = (pl.BlockSpec((tm, N_ADA * d), row) if per_row
                else pl.BlockSpec((1, N_ADA, d), lambda i: (i // tiles_per_seq, 0, 0)))
    tab_spec = pl.BlockSpec((tm, LANE), lambda i: (i % tiles_per_seq, 0))
    return pl.pallas_call(
        functools.partial(_qkv_kernel, per_row),
        grid=(t // tm,),
        in_specs=[pl.BlockSpec((tm, d), row), mod_spec, _const_spec(ng.shape), _const_spec(w.shape),
                  _const_spec(b.shape)] + [tab_spec] * 3,
        out_specs=[pl.BlockSpec((tm, d), row), pl.BlockSpec((tm, nkv), row), pl.BlockSpec((tm, nkv), row)],
        out_shape=[jax.ShapeDtypeStruct((t, d), F32), jax.ShapeDtypeStruct((t, nkv), F32),
                   jax.ShapeDtypeStruct((t, nkv), F32)],
        compiler_params=pltpu.CompilerParams(dimension_semantics=("parallel",), vmem_limit_bytes=VMEM_LIMIT),
        name="qkv_step" if per_row else "qkv_seq",
    )(x, mod, ng, w, b, *tables)


def _swa_seq_kernel(q_ref, kc_ref, kp_ref, vc_ref, vp_ref, sink_ref, o_ref):
    n = pl.program_id(1)
    blk = ATT_BLOCK
    nq = q_ref.shape[0] // blk
    kall = jnp.concatenate([kp_ref[...], kc_ref[...]], axis=0)
    vall = jnp.concatenate([vp_ref[...], vc_ref[...]], axis=0)
    j = lax.broadcasted_iota(jnp.int32, (2 * blk, GROUP * blk), 0)
    i = lax.broadcasted_iota(jnp.int32, (2 * blk, GROUP * blk), 1) % blk
    band = (j >= i + (blk - WINDOW)) & (j <= i + blk)
    first = band & ((n > 0) | (j >= blk))
    hsl = [slice(h * HEAD_DIM, (h + 1) * HEAD_DIM) for h in range(KV_HEADS * GROUP)]
    for t in range(nq):
        _swa_block(q_ref[t * blk:(t + 1) * blk, :], kall[t * blk:(t + 2) * blk], vall[t * blk:(t + 2) * blk],
                   first if t == 0 else band, sink_ref, o_ref, slice(t * blk, (t + 1) * blk), hsl)


def _swa_block(q, kall, vall, mask, sink_ref, o_ref, rows, hsl):
    blk = ATT_BLOCK
    kvs = range(KV_HEADS)
    eye_d = (lax.broadcasted_iota(jnp.int32, (HEAD_DIM, HEAD_DIM), 0)
             == lax.broadcasted_iota(jnp.int32, (HEAD_DIM, HEAD_DIM), 1)).astype(BF16)
    eye_q = (lax.broadcasted_iota(jnp.int32, (blk, blk), 0)
             == lax.broadcasted_iota(jnp.int32, (blk, blk), 1)).astype(BF16)
    ones = jnp.ones((16, 2 * blk), BF16)
    scale = HEAD_DIM ** -0.5
    kh = [_bf(kall[:, hsl[c]]) for c in kvs]
    vte = [jnp.concatenate([_bf(_dot_nt(eye_d, _bf(vall[:, hsl[c]]))), ones], axis=0) for c in kvs]
    qg = [_bf(jnp.concatenate([q[:, hsl[c * GROUP + g]] for g in range(GROUP)], axis=0) * scale) for c in kvs]
    s = [jnp.where(mask, _dot_nt(kh[c], qg[c]), -jnp.inf) for c in kvs]
    sink = [jnp.concatenate([jnp.broadcast_to(sink_ref[0:1, h:h + 1], (1, blk))
                             for h in range(c * GROUP, (c + 1) * GROUP)], axis=1) for c in kvs]
    m = [jnp.maximum(jnp.max(s[c], axis=0, keepdims=True), sink[c]) for c in kvs]
    p = [_bf(jnp.exp(s[c] - m[c])) for c in kvs]
    oe = [_dot(vte[c], p[c]) for c in kvs]
    ot = [_bf(oe[c][:HEAD_DIM] / (oe[c][HEAD_DIM:HEAD_DIM + 1] + jnp.exp(sink[c] - m[c]))) for c in kvs]
    for c in kvs:
        for pair in range(GROUP // 2):
            both = jnp.concatenate([ot[c][:, (2 * pair + e) * blk:(2 * pair + e + 1) * blk] for e in range(2)],
                                   axis=0)
            col = (c * GROUP + 2 * pair) * HEAD_DIM
            o_ref[rows, col:col + 2 * HEAD_DIM] = _dot_nt(eye_q, both).astype(o_ref.dtype)


def _swa_seq(q, k, v, sink, batch, seq_len):
    t, d = q.shape
    nkv = k.shape[1]
    nq = SWA_BLOCKS_PER_STEP
    rows = nq * ATT_BLOCK
    nb = seq_len // rows
    cur = lambda b, n: (b * nb + n, 0)
    prv = lambda b, n: ((b * nb + n) * nq - jnp.minimum(n, 1), 0)
    return pl.pallas_call(
        _swa_seq_kernel,
        grid=(batch, nb),
        in_specs=[pl.BlockSpec((rows, d), cur),
                  pl.BlockSpec((rows, nkv), cur), pl.BlockSpec((ATT_BLOCK, nkv), prv),
                  pl.BlockSpec((rows, nkv), cur), pl.BlockSpec((ATT_BLOCK, nkv), prv),
                  pl.BlockSpec(sink.shape, lambda b, n: (0, 0))],
        out_specs=pl.BlockSpec((rows, d), cur),
        out_shape=jax.ShapeDtypeStruct((t, d), BF16),
        compiler_params=pltpu.CompilerParams(dimension_semantics=("parallel", "parallel")),
        name="swa_seq",
    )(q, k, k, v, v, sink)


def _swa_step_kernel(q_ref, kn_ref, vn_ref, knt_ref, vnt_ref, kc_ref, vc_ref, sink_ref, o_ref, ko_ref, vo_ref):
    nb, _, w = kc_ref.shape
    kc = kc_ref[...]
    vc = vc_ref[...]
    kn = kn_ref[...]
    vn = vn_ref[...]
    q = q_ref[...]
    scale = HEAD_DIM ** -0.5
    kvs = range(KV_HEADS)
    ksl = [slice(h * HEAD_DIM, (h + 1) * HEAD_DIM) for h in kvs]
    gsl = [slice(h * GROUP, (h + 1) * GROUP) for h in kvs]
    qh = [_bf(q[:, gsl[h], :]) for h in kvs]
    s_old = [jnp.einsum("bgd,bdw->bgw", qh[h], _bf(kc[:, ksl[h], :]), preferred_element_type=F32) * scale
             for h in kvs]
    s_new = [jnp.sum(qh[h].astype(F32) * _bf(kn[:, :, ksl[h]]).astype(F32), axis=-1, keepdims=True) * scale
             for h in kvs]
    sink = [sink_ref[h] for h in kvs]
    m = [jnp.maximum(jnp.maximum(jnp.max(s_old[h], axis=-1, keepdims=True), s_new[h]), sink[h]) for h in kvs]
    p_old = [jnp.exp(s_old[h] - m[h]) for h in kvs]
    p_new = [jnp.exp(s_new[h] - m[h]) for h in kvs]
    den = [jnp.sum(p_old[h], axis=-1, keepdims=True) + p_new[h] + jnp.exp(sink[h] - m[h]) for h in kvs]
    o = [jnp.einsum("bgw,bdw->bgd", _bf(p_old[h] / den[h]), _bf(vc[:, ksl[h], :]), preferred_element_type=F32)
         for h in kvs]
    for h in kvs:
        o_ref[:, gsl[h], :] = o[h] + _bf(p_new[h] / den[h]).astype(F32) * _bf(vn[:, :, ksl[h]]).astype(F32)
    last = lax.broadcasted_iota(jnp.int32, kc.shape, 2) == w - 1

    def column(t_ref):
        t = t_ref[0]
        return jnp.stack([jnp.broadcast_to(t[:, b:b + 1], t.shape[:1] + (w,)) for b in range(nb)])

    ko_ref[...] = jnp.where(last, column(knt_ref), pltpu.roll(kc, w - 1, axis=2))
    vo_ref[...] = jnp.where(last, column(vnt_ref), pltpu.roll(vc, w - 1, axis=2))


def _swa_step(q, kn, vn, cache_k, cache_v, sink):
    b, nh, hd = q.shape
    nkv, w = cache_k.shape[1], cache_k.shape[2]
    nb = min(b, DECODE_ROWS)
    i3 = lambda i: (i, 0, 0)
    cspec = pl.BlockSpec((nb, nkv, w), i3)
    nspec = pl.BlockSpec((nb, 1, nkv), i3)
    tspec = pl.BlockSpec((1, nkv, nb), i3)
    qspec = pl.BlockSpec((nb, nh, hd), i3)
    cols = lambda z: z.reshape(b // nb, nb, nkv).transpose(0, 2, 1)
    return pl.pallas_call(
        _swa_step_kernel,
        grid=(b // nb,),
        in_specs=[qspec, nspec, nspec, tspec, tspec, cspec, cspec, pl.BlockSpec(sink.shape, lambda i: (0, 0, 0))],
        out_specs=[qspec, cspec, cspec],
        out_shape=[jax.ShapeDtypeStruct(q.shape, F32), jax.ShapeDtypeStruct(cache_k.shape, F32),
                   jax.ShapeDtypeStruct(cache_v.shape, F32)],
        compiler_params=pltpu.CompilerParams(dimension_semantics=("parallel",)),
        name="swa_step",
    )(q, kn.reshape(b, 1, nkv), vn.reshape(b, 1, nkv), cols(kn), cols(vn), cache_k, cache_v, sink)


def kernel(x_prompt, x_sample, c_prompt, c_sample, state_wkv, state_shift, cache_k, cache_v, norm1_g, norm2_g, ada_w, ada_b, mlp_up, mlp_down, final_g, rw_mix, rw_wr, rw_wk, rw_wv, rw_wo, rw_w0, rw_w1, rw_w2, rw_a0, rw_a1, rw_a2, rw_g1, rw_g2, rw_kk, rw_ka, rw_rk, rw_lnx_g, rw_lnx_b, at_wqkv, at_bqkv, at_wo, at_sink):
    bp, seq, d = x_prompt.shape
    bs = x_sample.shape[0]
    assert x_sample.shape[1] == 1 and d % GW == 0 and d == rw_wr.shape[1]
    for rows in (TM_PROJ, TM_POST, TM_POST_RWKV, WKV_CHUNKS_PER_STEP * CHUNK, SWA_BLOCKS_PER_STEP * ATT_BLOCK):
        assert seq % rows == 0, (seq, rows)
    assert bs % 8 == 0 and cache_k.shape[2] <= WINDOW
    nh = d // HEAD_DIM
    tp = bp * seq
    row = lambda vec: vec.reshape(1, -1)

    n_c = bp + bs
    pad = (-n_c) % 8
    c_all = jnp.concatenate([c_prompt, c_sample, jnp.zeros((pad, d), F32)], axis=0)
    ada = _ada(c_all, ada_w, ada_b)
    mod_p = [ada[i, :bp].reshape(bp, N_ADA, d) for i in range(2)]
    mod_s = [ada[i, bp:bp + bs] for i in range(2)]

    xp = x_prompt.reshape(tp, d)
    xs = x_sample.reshape(bs, d)

    head_id = np.arange(GW) // HEAD_DIM
    ones_bd = jnp.asarray(head_id[:, None] == head_id[None, :], BF16)

    rwp = dict(mix=rw_mix[0], wr=_bf(rw_wr[0]), wk=_bf(rw_wk[0]), wv=_bf(rw_wv[0]),
               w1=_bf(rw_w1[0]), w2=_bf(rw_w2[0]), w0=row(rw_w0[0]),
               a1=_bf(rw_a1[0]), a2=_bf(rw_a2[0]), a0=row(rw_a0[0]),
               g1=_bf(rw_g1[0]), g2=_bf(rw_g2[0]), kk=row(rw_kk[0]), ka=row(rw_ka[0]), ones=ones_bd)
    n1 = row(norm1_g[0])
    post_rw = (row(rw_lnx_g[0]), row(rw_lnx_b[0]), row(rw_rk[0]), ones_bd)
    wo0, up, down = _bf(rw_wo[0]), _bf(mlp_up), _bf(mlp_down)
    n2 = row(norm2_g[0])

    r, k, v, lw, kk, a, g, hlast = _rwkv_proj(xp, None, mod_p[0], n1, rwp, TM_PROJ, seq)
    shift_p = hlast.reshape(bp, seq // TM_PROJ, 8, d)[:, -1, -1]
    y, st = _wkv_seq(r, k, v, lw, kk, a, bp, seq)
    wkv_p = st.reshape(bp, d // GW, HEAD_DIM, HPG, HEAD_DIM).transpose(0, 1, 3, 2, 4).reshape(bp, nh, HEAD_DIM, HEAD_DIM)
    (xp,) = _post(xp, y, (r, k, v, g), mod_p[0], False, post_rw, wo0, n2, 0, up, down, None, TM_POST_RWKV, seq)

    rs, ks, vs, lws, kks, as_, gs, shift_s = _rwkv_proj(xs, state_shift[0], mod_s[0], n1, rwp, bs, 1)
    vecs = jnp.stack([rs, jnp.exp(lws), ks, vs, -kks, kks * as_]).transpose(0, 2, 1).reshape(6, nh, HEAD_DIM, bs)
    st_s, ys = _wkv_step(state_wkv[0].transpose(1, 2, 3, 0), *vecs)
    wkv_s = st_s.transpose(3, 0, 1, 2)
    (xs,) = _post(xs, ys.reshape(d, bs).T, (rs, ks, vs, gs), mod_s[0], True, post_rw, wo0, n2, 0, up, down, None, bs, 1)

    wqkv, bqkv, wo1 = _bf(at_wqkv[0]), row(at_bqkv[0]), _bf(at_wo[0])
    n1, n2 = row(norm1_g[1]), row(norm2_g[1])
    fg = row(final_g)
    nkv = KV_HEADS * HEAD_DIM

    q, kq, vq = _qkv(xp, mod_p[1], False, n1, wqkv, bqkv, _rope_tables(seq, 0, 1), TM_PROJ, seq)
    o = _swa_seq(q, kq, vq, row(at_sink[0]), bp, seq)
    keep = min(WINDOW, seq)
    k_p = kq.reshape(bp, seq, nkv)[:, seq - keep:].reshape(bp, keep, KV_HEADS, HEAD_DIM)
    v_p = vq.reshape(bp, seq, nkv)[:, seq - keep:].reshape(bp, keep, KV_HEADS, HEAD_DIM)
    xp, y_p = _post(xp, o, None, mod_p[1], False, None, wo1, n2, 1, up, down, fg, TM_POST, seq)

    qs, kn, vn = _qkv(xs, mod_s[1], True, n1, wqkv, bqkv, _rope_tables(bs, PAST_LEN, 0), bs, 1)
    w_buf = cache_k.shape[2]
    to_minor = lambda c: c.transpose(0, 2, 3, 1).reshape(bs, nkv, w_buf)
    from_minor = lambda c: c.reshape(bs, KV_HEADS, HEAD_DIM, w_buf).transpose(0, 3, 1, 2)
    os_, k_s, v_s = _swa_step(qs.reshape(bs, nh, HEAD_DIM), kn, vn, to_minor(cache_k[0]), to_minor(cache_v[0]),
                              at_sink[0].reshape(KV_HEADS, GROUP, 1))
    xs, y_s = _post(xs, os_.reshape(bs, d), None, mod_s[1], True, None, wo1, n2, 1, up, down, fg, bs, 1)

    return (y_p.reshape(bp, seq, d), y_s.reshape(bs, 1, d),
            wkv_p[None], wkv_s[None], shift_p[None], shift_s[None],
            k_p[None], from_minor(k_s)[None], v_p[None], from_minor(v_s)[None])
```
